```python
import math
import jax, jax.numpy as jnp
from jax import lax
import numpy as np

D_MODEL = 1024
BATCH = 4
SEQ = 8192
DEPTH = 1
DEC_BATCH = 32
DEC_SEQ = 8
PAST_LEN = 16384
PAGE_SIZE = 128

HEAD_DIM = 64
N_ATT_HEADS = 12
ATT_WIDTH = N_ATT_HEADS * HEAD_DIM
CONV_WIDTH = D_MODEL - ATT_WIDTH
MIX_WIDTH = ATT_WIDTH + CONV_WIDTH
CONV_K = 31
DILATED_BRANCHES = ((128, 1), (512, 4), (2048, 16))
MAX_WINDOW = 2048
N_BUCKETS = 32
MAX_EXACT = N_BUCKETS // 2
MAX_DISTANCE = 2048
N_MEM = 256
X_HEADS = 4
X_HEAD_DIM = D_MODEL // X_HEADS
D_FF = -(-(8 * D_MODEL) // (3 * 256)) * 256
EPS = 1e-6
ATT_SCALE = HEAD_DIM ** -0.5
X_SCALE = X_HEAD_DIM ** -0.5

kernel_name = 'hybrid_dilated_conformer_decoder_step'


def rmsnorm(x, g):
    xf = x.astype(jnp.float32)
    xf = xf * lax.rsqrt(jnp.mean(xf * xf, axis=-1, keepdims=True) + EPS)
    return (xf * g.astype(jnp.float32)).astype(x.dtype)


def layernorm(x, g, b):
    xf = x.astype(jnp.float32)
    mu = jnp.mean(xf, axis=-1, keepdims=True)
    xc = xf - mu
    var = jnp.mean(xc * xc, axis=-1, keepdims=True)
    return (xc * lax.rsqrt(var + EPS) * g.astype(jnp.float32) + b.astype(jnp.float32)).astype(x.dtype)


def t5_bucket(dist):
    n = dist.astype(jnp.int32)
    nf = jnp.maximum(n, MAX_EXACT).astype(jnp.float32)
    large = MAX_EXACT + (jnp.log(nf / MAX_EXACT) / math.log(MAX_DISTANCE / MAX_EXACT)
                         * (N_BUCKETS - MAX_EXACT)).astype(jnp.int32)
    large = jnp.minimum(large, N_BUCKETS - 1)
    return jnp.where(n < MAX_EXACT, n, large)


def rel_bias(table, dist):
    return jnp.moveaxis(table[t5_bucket(dist)], -1, 0).astype(jnp.float32)


def masked_softmax_stats(s, valid):
    s = jnp.where(valid, s, -jnp.inf)
    m = jnp.max(s, axis=-1, keepdims=True)
    p = jnp.exp(s - m)
    den = jnp.sum(p, axis=-1, keepdims=True)
    return p / den, (m + jnp.log(den))[..., 0]


def combine_branches(outs, lses):
    lse = jnp.stack(lses, 0)
    w = jnp.exp(lse - jnp.max(lse, axis=0, keepdims=True))
    w = w / jnp.sum(w, axis=0, keepdims=True)
    o = jnp.sum(w[..., None] * jnp.stack(outs, 0).astype(jnp.float32), axis=0)
    return o.astype(outs[0].dtype)


def dilated_branch_prompt(q, k, v, table, dilation, steps):
    S, H, hd = q.shape
    L = S // dilation
    nb = -(-L // steps)
    Lp = nb * steps

    def to_blocks(t):
        t = t.reshape(L, dilation, H, hd).transpose(1, 0, 2, 3)
        t = jnp.pad(t, ((0, 0), (0, Lp - L), (0, 0), (0, 0)))
        return t.reshape(dilation, nb, steps, H, hd)

    def with_prev(t):
        prev = jnp.pad(t[:, :-1], ((0, 0), (1, 0), (0, 0), (0, 0), (0, 0)))
        return jnp.concatenate([prev, t], axis=2)

    qb = to_blocks(q)
    kk = with_prev(to_blocks(k))
    vv = with_prev(to_blocks(v))
    qi = jnp.arange(steps)[:, None]
    kj = jnp.arange(2 * steps)[None, :]
    sub_dist = steps + qi - kj
    band = (sub_dist >= 0) & (sub_dist <= steps)
    blk = jnp.arange(nb)[:, None, None]
    valid = band[None] & (blk * steps + kj[None] - steps >= 0)
    bias = rel_bias(table, dilation * jnp.maximum(sub_dist, 0))
    s = jnp.einsum('gnqhd,gnkhd->gnhqk', qb, kk).astype(jnp.float32) * ATT_SCALE + bias[None, None]
    p, lse = masked_softmax_stats(s, valid[None, :, None])
    o = jnp.einsum('gnhqk,gnkhd->gnqhd', p.astype(v.dtype), vv)
    o = o.reshape(dilation, Lp, H, hd)[:, :L].transpose(1, 0, 2, 3).reshape(S, H, hd)
    lse = lse.transpose(0, 1, 3, 2).reshape(dilation, Lp, H)[:, :L].transpose(1, 0, 2).reshape(S, H)
    return o, lse


def dilated_attention_prompt(q, k, v, table):
    def one(args):
        qs, ks, vs = args
        outs, lses = [], []
        for window, dil in DILATED_BRANCHES:
            o, l = dilated_branch_prompt(qs, ks, vs, table, dil, window // dil)
            outs.append(o)
            lses.append(l)
        return combine_branches(outs, lses)
    return lax.map(one, (q, k, v))


def dilated_attention_sample(q, kc, vc, table, buf_len):
    T = q.shape[1]
    outs, lses = [], []
    for window, dil in DILATED_BRANCHES:
        steps = window // dil
        i = jnp.arange(T)[:, None]
        j = jnp.arange(steps + 1)
        idx = buf_len + i - j[None, :] * dil
        valid = idx >= 0
        idxc = jnp.maximum(idx, 0)
        kg = kc[:, idxc]
        vg = vc[:, idxc]
        bias = rel_bias(table, j * dil)
        s = jnp.einsum('bthd,btjhd->bhtj', q, kg).astype(jnp.float32) * ATT_SCALE + bias[None, :, None, :]
        p, lse = masked_softmax_stats(s, valid[None, None])
        outs.append(jnp.einsum('bhtj,btjhd->bthd', p.astype(vc.dtype), vg))
        lses.append(lse.transpose(0, 2, 1))
    return combine_branches(outs, lses)


def mixer_inputs(h, w_in):
    N, T, _ = h.shape
    q, k, v, a, g = jnp.split(h @ w_in, [ATT_WIDTH, 2 * ATT_WIDTH, 3 * ATT_WIDTH,
                                         3 * ATT_WIDTH + CONV_WIDTH], axis=-1)
    shp = (N, T, N_ATT_HEADS, HEAD_DIM)
    u = a * jax.nn.sigmoid(g)
    return q.reshape(shp), k.reshape(shp), v.reshape(shp), u


def causal_dwconv(u_full, conv_w, conv_b):
    y = lax.conv_general_dilated(u_full, conv_w[:, None, :].astype(u_full.dtype), window_strides=(1,),
                                 padding='VALID', dimension_numbers=('NWC', 'WIO', 'NWC'),
                                 feature_group_count=CONV_WIDTH)
    return y + conv_b


def mixer_outputs(att, u_full, conv_w, conv_b, ln_g, ln_b, w_out):
    N, T = att.shape[:2]
    c = jax.nn.silu(layernorm(causal_dwconv(u_full, conv_w, conv_b), ln_g, ln_b))
    return jnp.concatenate([att.reshape(N, T, ATT_WIDTH), c], axis=-1) @ w_out


def mem_kv(mem, g, w_k, w_v):
    N, M, _ = mem.shape
    m = rmsnorm(mem, g)
    return ((m @ w_k).reshape(N, M, X_HEADS, X_HEAD_DIM),
            (m @ w_v).reshape(N, M, X_HEADS, X_HEAD_DIM))


def cross_attend(h, mk, mv, w_q, w_o):
    N, T, _ = h.shape
    q = (h @ w_q).reshape(N, T, X_HEADS, X_HEAD_DIM)
    s = jnp.einsum('bthd,bmhd->bhtm', q, mk).astype(jnp.float32) * X_SCALE
    p = jax.nn.softmax(s, axis=-1).astype(mv.dtype)
    o = jnp.einsum('bhtm,bmhd->bthd', p, mv).reshape(N, T, X_HEADS * X_HEAD_DIM)
    return o @ w_o


def swiglu(h, w_gate, w_up, w_down):
    return (jax.nn.silu(h @ w_gate) * (h @ w_up)) @ w_down


def setup_inputs(seed: int = 0) -> dict:
    key = jax.random.key(seed)
    ks = jax.random.split(key, 32)
    f32 = jnp.float32
    buf_len = min(MAX_WINDOW, PAST_LEN)

    def nrm(k, shape, scale=1.0):
        return jax.random.normal(k, shape, f32) * scale

    def gain(k, shape):
        return 1.0 + 0.05 * jax.random.normal(k, shape, f32)

    return {
        'x_prompt': nrm(ks[0], (BATCH, SEQ, D_MODEL)),
        'x_sample': nrm(ks[1], (DEC_BATCH, DEC_SEQ, D_MODEL)),
        'mem_prompt': nrm(ks[2], (BATCH, N_MEM, D_MODEL)),
        'cache_win_k': nrm(ks[3], (DEPTH, DEC_BATCH, buf_len, N_ATT_HEADS, HEAD_DIM)),
        'cache_win_v': nrm(ks[4], (DEPTH, DEC_BATCH, buf_len, N_ATT_HEADS, HEAD_DIM)),
        'cache_conv': nrm(ks[5], (DEPTH, DEC_BATCH, CONV_K - 1, CONV_WIDTH), 0.5),
        'cache_mem_k': nrm(ks[6], (DEPTH, DEC_BATCH, N_MEM, X_HEADS, X_HEAD_DIM)),
        'cache_mem_v': nrm(ks[7], (DEPTH, DEC_BATCH, N_MEM, X_HEADS, X_HEAD_DIM)),
        'rpb_table': nrm(ks[8], (N_BUCKETS, N_ATT_HEADS), 0.5),
        'norm_mix_g': gain(ks[9], (DEPTH, D_MODEL)),
        'w_in': nrm(ks[10], (DEPTH, D_MODEL, 3 * ATT_WIDTH + 2 * CONV_WIDTH), D_MODEL ** -0.5),
        'conv_w': nrm(ks[11], (DEPTH, CONV_K, CONV_WIDTH), CONV_K ** -0.5),
        'conv_b': nrm(ks[12], (DEPTH, CONV_WIDTH), 0.02),
        'conv_ln_g': gain(ks[13], (DEPTH, CONV_WIDTH)),
        'conv_ln_b': nrm(ks[14], (DEPTH, CONV_WIDTH), 0.02),
        'w_out': nrm(ks[15], (DEPTH, MIX_WIDTH, D_MODEL), MIX_WIDTH ** -0.5),
        'norm_x_g': gain(ks[16], (DEPTH, D_MODEL)),
        'norm_mem_g': gain(ks[17], (DEPTH, D_MODEL)),
        'w_xq': nrm(ks[18], (DEPTH, D_MODEL, X_HEADS * X_HEAD_DIM), D_MODEL ** -0.5),
        'w_xk': nrm(ks[19], (DEPTH, D_MODEL, X_HEADS * X_HEAD_DIM), D_MODEL ** -0.5),
        'w_xv': nrm(ks[20], (DEPTH, D_MODEL, X_HEADS * X_HEAD_DIM), D_MODEL ** -0.5),
        'w_xo': nrm(ks[21], (DEPTH, X_HEADS * X_HEAD_DIM, D_MODEL), (X_HEADS * X_HEAD_DIM) ** -0.5),
        'norm_ffn_g': gain(ks[22], (DEPTH, D_MODEL)),
        'w_ffn_gate': nrm(ks[23], (DEPTH, D_MODEL, D_FF), D_MODEL ** -0.5),
        'w_ffn_up': nrm(ks[24], (DEPTH, D_MODEL, D_FF), D_MODEL ** -0.5),
        'w_ffn_down': nrm(ks[25], (DEPTH, D_FF, D_MODEL), D_FF ** -0.5),
        'norm_final_g': gain(ks[26], (D_MODEL,)),
    }


def reference(x_prompt, x_sample, mem_prompt, cache_win_k, cache_win_v, cache_conv,
              cache_mem_k, cache_mem_v, rpb_table, norm_mix_g, w_in, conv_w, conv_b,
              conv_ln_g, conv_ln_b, w_out, norm_x_g, norm_mem_g, w_xq, w_xk, w_xv, w_xo,
              norm_ffn_g, w_ffn_gate, w_ffn_up, w_ffn_down, norm_final_g):
    buf_len = cache_win_k.shape[2]
    keep_p = min(MAX_WINDOW, x_prompt.shape[1])
    xp, xs = x_prompt, x_sample
    p_wk, p_wv, p_conv, p_mk, p_mv = [], [], [], [], []
    s_wk, s_wv, s_conv = [], [], []
    for l in range(DEPTH):
        q, k, v, u = mixer_inputs(rmsnorm(xp, norm_mix_g[l]), w_in[l])
        att = dilated_attention_prompt(q, k, v, rpb_table)
        u_full = jnp.pad(u, ((0, 0), (CONV_K - 1, 0), (0, 0)))
        xp = xp + mixer_outputs(att, u_full, conv_w[l], conv_b[l], conv_ln_g[l], conv_ln_b[l], w_out[l])
        p_wk.append(k[:, -keep_p:])
        p_wv.append(v[:, -keep_p:])
        p_conv.append(u_full[:, -(CONV_K - 1):])
        mk, mv = mem_kv(mem_prompt, norm_mem_g[l], w_xk[l], w_xv[l])
        p_mk.append(mk)
        p_mv.append(mv)
        xp = xp + cross_attend(rmsnorm(xp, norm_x_g[l]), mk, mv, w_xq[l], w_xo[l])
        xp = xp + swiglu(rmsnorm(xp, norm_ffn_g[l]), w_ffn_gate[l], w_ffn_up[l], w_ffn_down[l])

        q, k, v, u = mixer_inputs(rmsnorm(xs, norm_mix_g[l]), w_in[l])
        kc = jnp.concatenate([cache_win_k[l].astype(k.dtype), k], axis=1)
        vc = jnp.concatenate([cache_win_v[l].astype(v.dtype), v], axis=1)
        att = dilated_attention_sample(q, kc, vc, rpb_table, buf_len)
        u_full = jnp.concatenate([cache_conv[l].astype(u.dtype), u], axis=1)
        xs = xs + mixer_outputs(att, u_full, conv_w[l], conv_b[l], conv_ln_g[l], conv_ln_b[l], w_out[l])
        s_wk.append(kc[:, -buf_len:])
        s_wv.append(vc[:, -buf_len:])
        s_conv.append(u_full[:, -(CONV_K - 1):])
        xs = xs + cross_attend(rmsnorm(xs, norm_x_g[l]), cache_mem_k[l], cache_mem_v[l], w_xq[l], w_xo[l])
        xs = xs + swiglu(rmsnorm(xs, norm_ffn_g[l]), w_ffn_gate[l], w_ffn_up[l], w_ffn_down[l])

    y_prompt = rmsnorm(xp, norm_final_g)
    y_sample = rmsnorm(xs, norm_final_g)
    return (y_prompt, y_sample,
            jnp.stack(p_wk), jnp.stack(p_wv), jnp.stack(p_conv), jnp.stack(p_mk), jnp.stack(p_mv),
            jnp.stack(s_wk), jnp.stack(s_wv), jnp.stack(s_conv))
```

```python
import functools
import math

import numpy as np
import jax
import jax.numpy as jnp
from jax import lax
from jax.experimental import pallas as pl
from jax.experimental.pallas import tpu as pltpu

F32 = jnp.float32
BF16 = jnp.bfloat16

HEAD_DIM = 64
N_ATT_HEADS = 12
ATT_WIDTH = N_ATT_HEADS * HEAD_DIM
PAIR_W = 2 * HEAD_DIM
N_PAIRS = N_ATT_HEADS // 2
CONV_K = 31
CONV_HALO = 32
DILATED_BRANCHES = ((128, 1), (512, 4), (2048, 16))
STEPS = 128
NEW_LANES = 128
N_BUCKETS = 32
MAX_EXACT = N_BUCKETS // 2
MAX_DISTANCE = 2048
X_HEADS = 4
EPS = 1e-6
ATT_SCALE = HEAD_DIM ** -0.5
MASKED = -1e30

V7X_VMEM_BYTES = 64 * 1024 * 1024
VMEM_LIMIT = V7X_VMEM_BYTES * 3 // 4


def _params(n_grid_dims):
    return pltpu.CompilerParams(dimension_semantics=("arbitrary",) * n_grid_dims,
                                vmem_limit_bytes=VMEM_LIMIT)


def _rms(x, g):
    return x * lax.rsqrt(jnp.mean(x * x, axis=-1, keepdims=True) + EPS) * g


def _sigmoid(x):
    return 1.0 / (1.0 + jnp.exp(-x))


def _dot(a, b):
    return jnp.dot(a, b, preferred_element_type=F32)


def _dot_nt(a, b):
    return lax.dot_general(a, b, (((1,), (1,)), ((), ())), preferred_element_type=F32)


def _t5_bucket_np(dist):
    n = dist.astype(np.int32)
    nf = np.maximum(n, MAX_EXACT).astype(np.float32)
    large = MAX_EXACT + (np.log(nf / np.float32(MAX_EXACT)) / np.float32(math.log(MAX_DISTANCE / MAX_EXACT))
                         * np.float32(N_BUCKETS - MAX_EXACT)).astype(np.int32)
    large = np.minimum(large, N_BUCKETS - 1)
    return np.where(n < MAX_EXACT, n, large)


def _prompt_bucket_index():
    qi = np.arange(STEPS)[:, None]
    kj = np.arange(2 * STEPS)[None, :]
    sub = STEPS + qi - kj
    band = (sub >= 0) & (sub <= STEPS)
    out = []
    for first in (False, True):
        for _, dil in DILATED_BRANCHES:
            b = _t5_bucket_np(dil * np.maximum(sub, 0))
            ok = band & (kj >= STEPS) if first else band
            out.append(np.where(ok, b, -1))
    return np.stack(out).astype(np.int32)


def _sample_bucket_index(buf_len, t_new):
    col = np.arange(buf_len + NEW_LANES)
    key = np.where(col < buf_len, (col + t_new) % buf_len, col - NEW_LANES + t_new)
    keep = (col < buf_len) | (col >= buf_len + NEW_LANES - t_new)
    i = np.arange(t_new)[:, None]
    dist = buf_len + i - key[None, :]
    out = []
    for window, dil in DILATED_BRANCHES:
        ok = keep[None, :] & (dist >= 0) & (dist % dil == 0) & (dist // dil <= window // dil)
        out.append(np.where(ok, _t5_bucket_np(np.maximum(dist, 0)), -1))
    return np.stack(out).astype(np.int32)


def _bias_kernel(tab_ref, idx_ref, o_ref):
    pair = pl.program_id(1)
    idx = idx_ref[0]
    rows = idx.shape[0]
    for half in range(2):
        head = 2 * pair + half
        acc = jnp.full(idx.shape, MASKED, F32)
        for b in range(N_BUCKETS):
            acc = jnp.where(idx == b, tab_ref[b, head], acc)
        o_ref[0, 0, half * rows:(half + 1) * rows, :] = acc


def _bias_tables(table, idx):
    n_var, rows, cols = idx.shape
    return pl.pallas_call(
        _bias_kernel,
        grid=(n_var, N_PAIRS),
        in_specs=[pl.BlockSpec(memory_space=pltpu.SMEM),
                  pl.BlockSpec((1, rows, cols), lambda v, p: (v, 0, 0))],
        out_specs=pl.BlockSpec((1, 1, 2 * rows, cols), lambda v, p: (v, p, 0, 0)),
        out_shape=jax.ShapeDtypeStruct((n_var, N_PAIRS, 2 * rows, cols), F32),
        compiler_params=_params(2),
        name="bias_tables",
    )(table, jnp.asarray(idx))


def _in_proj_kernel(x_ref, g_ref, w_ref, q_ref, k_ref, v_ref, u_ref):
    h = _rms(x_ref[...], g_ref[...]).astype(BF16)
    aw = q_ref.shape[1]
    cw = u_ref.shape[1]
    q_ref[...] = _dot(h, w_ref[:, 0:aw]) * ATT_SCALE
    k_ref[...] = _dot(h, w_ref[:, aw:2 * aw])
    v_ref[...] = _dot(h, w_ref[:, 2 * aw:3 * aw])
    a = _dot(h, w_ref[:, 3 * aw:3 * aw + cw])
    gate = _dot(h, w_ref[:, 3 * aw + cw:3 * aw + 2 * cw])
    u_ref[...] = a * _sigmoid(gate)


def _in_proj(x, g, w_bf16, tm):
    n, d = x.shape
    cw = (w_bf16.shape[1] - 3 * ATT_WIDTH) // 2
    row = lambda i: (i, 0)
    fixed = lambda i: (0, 0)
    return pl.pallas_call(
        _in_proj_kernel,
        grid=(n // tm,),
        in_specs=[pl.BlockSpec((tm, d), row),
                  pl.BlockSpec((1, d), fixed),
                  pl.BlockSpec(w_bf16.shape, fixed)],
        out_specs=[pl.BlockSpec((tm, ATT_WIDTH), row)] * 3 + [pl.BlockSpec((tm, cw), row)],
        out_shape=[jax.ShapeDtypeStruct((n, ATT_WIDTH), F32)] * 3 + [jax.ShapeDtypeStruct((n, cw), F32)],
        compiler_params=_params(1),
        name="in_proj",
    )(x, g, w_bf16)


def _stack_heads(x, first):
    zero = jnp.zeros_like(x)
    return jnp.concatenate([jnp.where(first, x, zero), jnp.where(first, zero, x)], axis=0)


def _att_kernel(q_ref, kc_ref, kp_ref, vc_ref, vp_ref, bias_ref, bias0_ref, o_ref, num_s, m_s, l_s):
    chunk = q_ref.shape[0]
    lane = lax.broadcasted_iota(jnp.int32, (STEPS, PAIR_W), 1)
    first = lane < HEAD_DIM
    ones = jnp.ones((2 * STEPS, PAIR_W), BF16)

    def attend(g, dil, q_start, kcat, vcat, bias):
        rows = pl.ds(q_start, STEPS, stride=dil)
        q_st = _stack_heads(q_ref[rows, :], first).astype(BF16)
        s = _dot_nt(q_st, kcat.astype(BF16)) + bias
        m = jnp.max(s, axis=-1, keepdims=True)
        p = jnp.exp(s - m).astype(BF16)
        o = _dot(p, jnp.concatenate([vcat.astype(BF16), ones], axis=1))
        num_s[g, rows, :] = jnp.where(first, o[:STEPS, :PAIR_W], o[STEPS:, :PAIR_W])
        l_s[g, rows, :] = jnp.where(first, o[:STEPS, PAIR_W:], o[STEPS:, PAIR_W:])
        m_s[g, rows, :] = jnp.where(first, jnp.broadcast_to(m[:STEPS], (STEPS, PAIR_W)),
                                    jnp.broadcast_to(m[STEPS:], (STEPS, PAIR_W)))

    for g, (_, dil) in enumerate(DILATED_BRANCHES):
        span = STEPS * dil
        n_blk = chunk // span
        for r in range(dil):
            prev = pl.ds(chunk - span + r, STEPS, stride=dil)
            own = pl.ds(r, STEPS, stride=dil)
            attend(g, dil, r,
                   jnp.concatenate([kp_ref[prev, :], kc_ref[own, :]], axis=0),
                   jnp.concatenate([vp_ref[prev, :], vc_ref[own, :]], axis=0),
                   bias0_ref[0, g, 0])
            if n_blk > 1:
                def body(b, carry, g=g, dil=dil, r=r, span=span):
                    kv_rows = pl.ds(r + (b - 1) * span, 2 * STEPS, stride=dil)
                    attend(g, dil, r + b * span, kc_ref[kv_rows, :], vc_ref[kv_rows, :], bias_ref[g, 0])
                    return carry
                lax.fori_loop(1, n_blk, body, 0)

    m_all = jnp.maximum(jnp.maximum(m_s[0], m_s[1]), m_s[2])
    num = jnp.zeros(m_all.shape, F32)
    den = jnp.zeros(m_all.shape, F32)
    for g in range(len(DILATED_BRANCHES)):
        w = jnp.exp(m_s[g] - m_all)
        num = num + w * num_s[g]
        den = den + w * l_s[g]
    o_ref[...] = (num / den).astype(o_ref.dtype)


def _dilated_attention_prompt(q, k, v, bias, seq, chunk):
    n = q.shape[0]
    n_chunk = seq // chunk
    n_br = len(DILATED_BRANCHES)
    cur = lambda b, c, p: (b * n_chunk + c, p)
    prev = lambda b, c, p: (b * n_chunk + jnp.maximum(c - 1, 0), p)
    blk = pl.BlockSpec((chunk, PAIR_W), cur)
    blk_prev = pl.BlockSpec((chunk, PAIR_W), prev)
    bias4 = bias.reshape(2, n_br, N_PAIRS, 2 * STEPS, 2 * STEPS)
    return pl.pallas_call(
        _att_kernel,
        grid=(n // seq, n_chunk, N_PAIRS),
        in_specs=[blk, blk, blk_prev, blk, blk_prev,
                  pl.BlockSpec((n_br, 1, 2 * STEPS, 2 * STEPS), lambda b, c, p: (0, p, 0, 0)),
                  pl.BlockSpec((1, n_br, 1, 2 * STEPS, 2 * STEPS),
                               lambda b, c, p: (jnp.where(c == 0, 1, 0), 0, p, 0, 0))],
        out_specs=blk,
        out_shape=jax.ShapeDtypeStruct((n, ATT_WIDTH), BF16),
        scratch_shapes=[pltpu.VMEM((n_br, chunk, PAIR_W), F32)] * 3,
        compiler_params=_params(3),
        name="dilated_attention_prompt",
    )(q, k, k, v, v, bias4[0], bias4)


def _sample_att_kernel(q_ref, kn_ref, vn_ref, ck_ref, cv_ref, bias_ref, ok_ref, ov_ref, o_ref):
    buf_len = ck_ref.shape[2]
    t_new = q_ref.shape[0]
    tail = buf_len - NEW_LANES
    is_new = lax.broadcasted_iota(jnp.int32, (PAIR_W, NEW_LANES), 1) >= NEW_LANES - t_new

    def shift_in(c_ref, n_ref, o_ref):
        rot = pltpu.roll(c_ref[0], buf_len - t_new, axis=1)
        new = n_ref[0]
        o_ref[0, :, :tail] = rot[:, :tail]
        o_ref[0, :, tail:] = jnp.where(is_new, new, rot[:, tail:])
        return rot.astype(BF16), new.astype(BF16)

    rot_k, new_k = shift_in(ck_ref, kn_ref, ok_ref)
    rot_v, new_v = shift_in(cv_ref, vn_ref, ov_ref)

    first = lax.broadcasted_iota(jnp.int32, (t_new, PAIR_W), 1) < HEAD_DIM
    q_st = _stack_heads(q_ref[...], first).astype(BF16)
    s = jnp.concatenate([_dot(q_st, rot_k), _dot(q_st, new_k)], axis=1)
    sg = [s + bias_ref[g, 0] for g in range(len(DILATED_BRANCHES))]
    m = functools.reduce(jnp.maximum, [jnp.max(x, axis=-1, keepdims=True) for x in sg])
    p = functools.reduce(jnp.add, [jnp.exp(x - m) for x in sg])
    den = jnp.sum(p, axis=-1, keepdims=True)
    pb = p.astype(BF16)
    o = (_dot_nt(pb[:, :buf_len], rot_v) + _dot_nt(pb[:, buf_len:], new_v)) / den
    o_ref[...] = jnp.where(first, o[:t_new], o[t_new:]).astype(o_ref.dtype)


def _dilated_attention_sample(q, k_new_t, v_new_t, cache_k_t, cache_v_t, bias, t_new):
    nb, _, buf_len = cache_k_t.shape
    n_br = len(DILATED_BRANCHES)
    tok = pl.BlockSpec((t_new, PAIR_W), lambda n, p: (n, p))
    new = pl.BlockSpec((1, PAIR_W, NEW_LANES), lambda n, p: (n, p, 0))
    buf = pl.BlockSpec((1, PAIR_W, buf_len), lambda n, p: (n, p, 0))
    return pl.pallas_call(
        _sample_att_kernel,
        grid=(nb, N_PAIRS),
        in_specs=[tok, new, new, buf, buf,
                  pl.BlockSpec((n_br, 1, 2 * t_new, buf_len + NEW_LANES), lambda n, p: (0, p, 0, 0))],
        out_specs=[buf, buf, tok],
        out_shape=[jax.ShapeDtypeStruct(cache_k_t.shape, F32), jax.ShapeDtypeStruct(cache_v_t.shape, F32),
                   jax.ShapeDtypeStruct((nb * t_new, ATT_WIDTH), BF16)],
        compiler_params=_params(2),
        name="dilated_attention_sample",
    )(q, k_new_t, v_new_t, cache_k_t, cache_v_t, bias)


def _conv_tail(y, cb_ref, lg_ref, lb_ref):
    y = y + cb_ref[...]
    mu = jnp.mean(y, axis=-1, keepdims=True)
    yc = y - mu
    var = jnp.mean(yc * yc, axis=-1, keepdims=True)
    yn = yc * lax.rsqrt(var + EPS) * lg_ref[...] + lb_ref[...]
    return yn * _sigmoid(yn)


def _mix_out(x, att, c, wo_ref):
    aw = att.shape[1]
    return x + _dot(att, wo_ref[:aw, :]) + _dot(c.astype(BF16), wo_ref[aw:, :])


def _cross_attend(qx, mk, mv):
    hd = qx.shape[1] // X_HEADS
    outs = []
    for h in range(X_HEADS):
        sl = slice(h * hd, (h + 1) * hd)
        s = _dot_nt(qx[:, sl].astype(BF16), mk[:, sl])
        p = jnp.exp(s - jnp.max(s, axis=-1, keepdims=True))
        den = jnp.sum(p, axis=-1, keepdims=True)
        outs.append((_dot(p.astype(BF16), mv[:, sl]) / den).astype(BF16))
    return jnp.concatenate(outs, axis=1)


def _prompt_post_kernel(x_ref, att_ref, u_ref, uh_ref, cw_ref, cb_ref, lg_ref, lb_ref, wo_ref,
                        gx_ref, wq_ref, mk_ref, mv_ref, wxo_ref, o_ref, ubuf, *, tiles_per_seq):
    tm = x_ref.shape[0]
    seq_start = (pl.program_id(0) % tiles_per_seq) == 0
    ubuf[:CONV_HALO, :] = jnp.where(seq_start, 0.0, uh_ref[...])
    ubuf[CONV_HALO:, :] = u_ref[...]
    y = jnp.zeros(u_ref.shape, F32)
    for j in range(CONV_K):
        y = y + cw_ref[j:j + 1, :] * ubuf[pl.ds(CONV_HALO - (CONV_K - 1) + j, tm), :]
    c = _conv_tail(y, cb_ref, lg_ref, lb_ref)
    x1 = _mix_out(x_ref[...], att_ref[...], c, wo_ref)
    hd = wq_ref.shape[1] // X_HEADS
    qx = _dot(_rms(x1, gx_ref[...]).astype(BF16), wq_ref[...]) * (hd ** -0.5)
    o_ref[...] = x1 + _dot(_cross_attend(qx, mk_ref[...], mv_ref[...]), wxo_ref[...])


def _prompt_post(x, att, u, conv_w, conv_b, ln_g, ln_b, w_out, gx, w_xq, mk, mv, w_xo, seq, tm):
    n, d = x.shape
    cw = u.shape[1]
    n_mem = mk.shape[0] // (n // seq)
    tiles_per_seq = seq // tm
    row = lambda i: (i, 0)
    fixed = lambda i: (0, 0)
    halo = lambda i: (jnp.maximum(i * (tm // CONV_HALO) - 1, 0), 0)
    per_seq = lambda i: (i // tiles_per_seq, 0)
    full = lambda a: pl.BlockSpec(a.shape, fixed)
    return pl.pallas_call(
        functools.partial(_prompt_post_kernel, tiles_per_seq=tiles_per_seq),
        grid=(n // tm,),
        in_specs=[pl.BlockSpec((tm, d), row), pl.BlockSpec((tm, ATT_WIDTH), row),
                  pl.BlockSpec((tm, cw), row), pl.BlockSpec((CONV_HALO, cw), halo),
                  full(conv_w), full(conv_b), full(ln_g), full(ln_b), full(w_out),
                  full(gx), full(w_xq),
                  pl.BlockSpec((n_mem, d), per_seq), pl.BlockSpec((n_mem, d), per_seq), full(w_xo)],
        out_specs=pl.BlockSpec((tm, d), row),
        out_shape=jax.ShapeDtypeStruct((n, d), F32),
        scratch_shapes=[pltpu.VMEM((CONV_HALO + tm, cw), F32)],
        compiler_params=_params(1),
        name="prompt_mix_cross",
    )(x, att, u, u, conv_w, conv_b, ln_g, ln_b, w_out, gx, w_xq, mk, mv, w_xo)


def _sample_mix_kernel(x_ref, att_ref, uf_ref, cw_ref, cb_ref, lg_ref, lb_ref, wo_ref, gx_ref, wq_ref,
                       x1_ref, qx_ref):
    nb, t_full, cw = uf_ref.shape
    t_new = t_full - (CONV_K - 1)
    y = jnp.zeros((nb, t_new, cw), F32)
    for j in range(CONV_K):
        y = y + cw_ref[j:j + 1, :] * uf_ref[:, j:j + t_new, :]
    c = _conv_tail(y.reshape(nb * t_new, cw), cb_ref, lg_ref, lb_ref)
    x1 = _mix_out(x_ref[...], att_ref[...], c, wo_ref)
    x1_ref[...] = x1
    hd = wq_ref.shape[1] // X_HEADS
    qx_ref[...] = _dot(_rms(x1, gx_ref[...]).astype(BF16), wq_ref[...]) * (hd ** -0.5)


def _sample_mix(x, att, u_full, conv_w, conv_b, ln_g, ln_b, w_out, gx, w_xq):
    n, d = x.shape
    return pl.pallas_call(
        _sample_mix_kernel,
        out_shape=[jax.ShapeDtypeStruct((n, d), F32), jax.ShapeDtypeStruct((n, w_xq.shape[1]), F32)],
        compiler_params=pltpu.CompilerParams(vmem_limit_bytes=VMEM_LIMIT),
        name="sample_mix",
    )(x, att, u_full, conv_w, conv_b, ln_g, ln_b, w_out, gx, w_xq)


def _sample_cross_kernel(qx_ref, mk_ref, mv_ref, o_ref):
    o_ref[...] = _cross_attend(qx_ref[...], mk_ref[0].astype(BF16), mv_ref[0].astype(BF16))


def _sample_cross(qx, mem_k, mem_v, t_new):
    nb, n_mem, d = mem_k.shape
    tok = pl.BlockSpec((t_new, d), lambda n: (n, 0))
    mem = pl.BlockSpec((1, n_mem, d), lambda n: (n, 0, 0))
    return pl.pallas_call(
        _sample_cross_kernel,
        grid=(nb,),
        in_specs=[tok, mem, mem],
        out_specs=tok,
        out_shape=jax.ShapeDtypeStruct(qx.shape, BF16),
        compiler_params=_params(1),
        name="sample_cross",
    )(qx, mem_k, mem_v)


def _swiglu_final(x2, gf_ref, wg_ref, wu_ref, wd_ref, gfin_ref, ff_chunk):
    h = _rms(x2, gf_ref[...]).astype(BF16)
    acc = x2
    for lo in range(0, wg_ref.shape[1], ff_chunk):
        gate = _dot(h, wg_ref[:, lo:lo + ff_chunk])
        up = _dot(h, wu_ref[:, lo:lo + ff_chunk])
        acc = acc + _dot((gate * _sigmoid(gate) * up).astype(BF16), wd_ref[lo:lo + ff_chunk, :])
    return _rms(acc, gfin_ref[...])


def _ffn_kernel(x_ref, gf_ref, wg_ref, wu_ref, wd_ref, gfin_ref, o_ref, *, ff_chunk):
    o_ref[...] = _swiglu_final(x_ref[...], gf_ref, wg_ref, wu_ref, wd_ref, gfin_ref, ff_chunk)


def _sample_tail_kernel(x_ref, xo_ref, wxo_ref, gf_ref, wg_ref, wu_ref, wd_ref, gfin_ref, o_ref, *, ff_chunk):
    x2 = x_ref[...] + _dot(xo_ref[...], wxo_ref[...])
    o_ref[...] = _swiglu_final(x2, gf_ref, wg_ref, wu_ref, wd_ref, gfin_ref, ff_chunk)


FF_CHUNK = 256


def _ffn(x, gf, wg, wu, wd, gfin, tm):
    n, d = x.shape
    row = lambda i: (i, 0)
    full = lambda a: pl.BlockSpec(a.shape, lambda i: (0, 0))
    return pl.pallas_call(
        functools.partial(_ffn_kernel, ff_chunk=FF_CHUNK),
        grid=(n // tm,),
        in_specs=[pl.BlockSpec((tm, d), row), full(gf), full(wg), full(wu), full(wd), full(gfin)],
        out_specs=pl.BlockSpec((tm, d), row),
        out_shape=jax.ShapeDtypeStruct((n, d), F32),
        compiler_params=_params(1),
        name="swiglu_final_norm",
    )(x, gf, wg, wu, wd, gfin)


def _sample_tail(x1, xo, w_xo, gf, wg, wu, wd, gfin):
    return pl.pallas_call(
        functools.partial(_sample_tail_kernel, ff_chunk=FF_CHUNK),
        out_shape=jax.ShapeDtypeStruct(x1.shape, F32),
        compiler_params=pltpu.CompilerParams(vmem_limit_bytes=VMEM_LIMIT),
        name="sample_cross_out_swiglu",
    )(x1, xo, w_xo, gf, wg, wu, wd, gfin)


def _mem_kv_kernel(m_ref, g_ref, wk_ref, wv_ref, k_ref, v_ref, kb_ref, vb_ref):
    h = _rms(m_ref[...], g_ref[...]).astype(BF16)
    k = _dot(h, wk_ref[...])
    v = _dot(h, wv_ref[...])
    k_ref[...] = k
    v_ref[...] = v
    kb_ref[...] = k.astype(BF16)
    vb_ref[...] = v.astype(BF16)


def _mem_kv(mem, g, w_k, w_v, tm):
    n, d = mem.shape
    row = lambda i: (i, 0)
    full = lambda a: pl.BlockSpec(a.shape, lambda i: (0, 0))
    dk = w_k.shape[1]
    return pl.pallas_call(
        _mem_kv_kernel,
        grid=(n // tm,),
        in_specs=[pl.BlockSpec((tm, d), row), full(g), full(w_k), full(w_v)],
        out_specs=[pl.BlockSpec((tm, dk), row)] * 4,
        out_shape=[jax.ShapeDtypeStruct((n, dk), F32)] * 2 + [jax.ShapeDtypeStruct((n, dk), BF16)] * 2,
        compiler_params=_params(1),
        name="mem_kv",
    )(mem, g, w_k, w_v)


ROW_TILE = 512
ATT_CHUNK = STEPS * DILATED_BRANCHES[-1][1]


def kernel(x_prompt, x_sample, mem_prompt, cache_win_k, cache_win_v, cache_conv, cache_mem_k, cache_mem_v,
           rpb_table, norm_mix_g, w_in, conv_w, conv_b, conv_ln_g, conv_ln_b, w_out, norm_x_g, norm_mem_g,
           w_xq, w_xk, w_xv, w_xo, norm_ffn_g, w_ffn_gate, w_ffn_up, w_ffn_down, norm_final_g):
    depth = w_in.shape[0]
    assert depth == 1, "single-layer stack"
    batch, seq, d = x_prompt.shape
    nb, t_new, _ = x_sample.shape
    buf_len = cache_win_k.shape[2]
    keep_p = min(MAX_DISTANCE, seq)
    n_mem = mem_prompt.shape[1]
    conv_hist = CONV_K - 1
    assert seq % ATT_CHUNK == 0 and keep_p % ATT_CHUNK == 0 and buf_len == MAX_DISTANCE

    row = lambda a: a.reshape(1, -1)
    bf = lambda a: a.astype(BF16)
    l = 0
    w_in_b, w_out_b = bf(w_in[l]), bf(w_out[l])
    w_xq_b, w_xk_b, w_xv_b, w_xo_b = bf(w_xq[l]), bf(w_xk[l]), bf(w_xv[l]), bf(w_xo[l])
    w_g_b, w_u_b, w_d_b = bf(w_ffn_gate[l]), bf(w_ffn_up[l]), bf(w_ffn_down[l])
    g_mix, g_x, g_mem, g_ffn, g_fin = (row(norm_mix_g[l]), row(norm_x_g[l]), row(norm_mem_g[l]),
                                       row(norm_ffn_g[l]), row(norm_final_g))
    cv_w, cv_b, ln_g, ln_b = conv_w[l], row(conv_b[l]), row(conv_ln_g[l]), row(conv_ln_b[l])

    bias_p = _bias_tables(rpb_table, _prompt_bucket_index())
    bias_s = _bias_tables(rpb_table, _sample_bucket_index(buf_len, t_new))

    xp = x_prompt.reshape(batch * seq, d)
    q, k, v, u = _in_proj(xp, g_mix, w_in_b, ROW_TILE)
    att = _dilated_attention_prompt(q, k, v, bias_p, seq, ATT_CHUNK)
    mk, mv, mk_b, mv_b = _mem_kv(mem_prompt.reshape(batch * n_mem, d), g_mem, w_xk_b, w_xv_b, n_mem)
    x2 = _prompt_post(xp, att, u, cv_w, cv_b, ln_g, ln_b, w_out_b, g_x, w_xq_b, mk_b, mv_b, w_xo_b,
                      seq, ROW_TILE)
    y_prompt = _ffn(x2, g_ffn, w_g_b, w_u_b, w_d_b, g_fin, ROW_TILE).reshape(batch, seq, d)

    heads = lambda a, n: a.reshape(1, n, -1, N_ATT_HEADS, HEAD_DIM)
    p_wk = heads(k.reshape(batch, seq, ATT_WIDTH)[:, seq - keep_p:], batch)
    p_wv = heads(v.reshape(batch, seq, ATT_WIDTH)[:, seq - keep_p:], batch)
    u3 = u.reshape(batch, seq, -1)
    p_conv = u3[:, seq - conv_hist:][None]
    xh = lambda a, n: a.reshape(1, n, n_mem, X_HEADS, -1)
    p_mk, p_mv = xh(mk, batch), xh(mv, batch)

    xs = x_sample.reshape(nb * t_new, d)
    qs, ks, vs, us = _in_proj(xs, g_mix, w_in_b, nb * t_new)
    chan_major = lambda a: jnp.transpose(a, (0, 2, 3, 1)).reshape(nb, ATT_WIDTH, -1)
    new_rows = lambda a: jnp.pad(jnp.transpose(a.reshape(nb, t_new, ATT_WIDTH), (0, 2, 1)),
                                 ((0, 0), (0, 0), (NEW_LANES - t_new, 0)))
    tok_major = lambda a: jnp.transpose(a.reshape(nb, N_ATT_HEADS, HEAD_DIM, -1), (0, 3, 1, 2))[None]
    s_wk_t, s_wv_t, att_s = _dilated_attention_sample(
        qs, new_rows(ks), new_rows(vs), chan_major(cache_win_k[l]), chan_major(cache_win_v[l]), bias_s, t_new)
    u_full = jnp.concatenate([cache_conv[l], us.reshape(nb, t_new, -1)], axis=1)
    x1s, qxs = _sample_mix(xs, att_s, u_full, cv_w, cv_b, ln_g, ln_b, w_out_b, g_x, w_xq_b)
    xo_s = _sample_cross(qxs, cache_mem_k[l].reshape(nb, n_mem, -1), cache_mem_v[l].reshape(nb, n_mem, -1), t_new)
    y_sample = _sample_tail(x1s, xo_s, w_xo_b, g_ffn, w_g_b, w_u_b, w_d_b, g_fin).reshape(nb, t_new, d)

    return (y_prompt, y_sample, p_wk, p_wv, p_conv, p_mk, p_mv,
            tok_major(s_wk_t), tok_major(s_wv_t), u_full[:, t_new:][None])
```

```python
import functools
import math

import numpy as np
import jax
import jax.numpy as jnp
from jax import lax
from jax.experimental import pallas as pl
from jax.experimental.pallas import tpu as pltpu

F32 = jnp.float32
BF16 = jnp.bfloat16

HEAD_DIM = 64
N_ATT_HEADS = 12
ATT_WIDTH = N_ATT_HEADS * HEAD_DIM
PAIR_W = 2 * HEAD_DIM
N_PAIRS = N_ATT_HEADS // 2
CONV_K = 31
CONV_HALO = 32
DILATED_BRANCHES = ((128, 1), (512, 4), (2048, 16))
STEPS = 128
N_CLS = DILATED_BRANCHES[-1][1]
NEW_LANES = 128
N_BUCKETS = 32
MAX_EXACT = N_BUCKETS // 2
MAX_DISTANCE = 2048
X_HEADS = 4
EPS = 1e-6
ATT_SCALE = HEAD_DIM ** -0.5
MASKED = -1e30

V7X_VMEM_BYTES = 64 * 1024 * 1024
VMEM_LIMIT = V7X_VMEM_BYTES * 3 // 4


def _params(n_grid_dims):
    return pltpu.CompilerParams(dimension_semantics=("arbitrary",) * n_grid_dims,
                                vmem_limit_bytes=VMEM_LIMIT)


def _rms(x, g):
    return x * lax.rsqrt(jnp.mean(x * x, axis=-1, keepdims=True) + EPS) * g


def _sigmoid(x):
    return 1.0 / (1.0 + jnp.exp(-x))


def _dot(a, b):
    return jnp.dot(a, b, preferred_element_type=F32)


def _dot_nt(a, b):
    return lax.dot_general(a, b, (((1,), (1,)), ((), ())), preferred_element_type=F32)


def _t5_bucket_np(dist):
    n = dist.astype(np.int32)
    nf = np.maximum(n, MAX_EXACT).astype(np.float32)
    large = MAX_EXACT + (np.log(nf / np.float32(MAX_EXACT)) / np.float32(math.log(MAX_DISTANCE / MAX_EXACT))
                         * np.float32(N_BUCKETS - MAX_EXACT)).astype(np.int32)
    large = np.minimum(large, N_BUCKETS - 1)
    return np.where(n < MAX_EXACT, n, large)


def _block_order(dil):
    groups = N_CLS // dil
    per = STEPS // groups
    i = np.arange(STEPS)
    return (i % per) * groups + i // per


def _prompt_bucket_index():
    out = []
    for first in (False, True):
        for _, dil in DILATED_BRANCHES:
            n = _block_order(dil)
            kj = np.concatenate([n, STEPS + n])[None, :]
            sub = STEPS + n[:, None] - kj
            band = (sub >= 0) & (sub <= STEPS)
            b = _t5_bucket_np(dil * np.maximum(sub, 0))
            ok = band & (kj >= STEPS) if first else band
            out.append(np.where(ok, b, -1))
    return np.stack(out).astype(np.int32)


def _sample_bucket_index(buf_len, t_new):
    col = np.arange(buf_len + NEW_LANES)
    key = np.where(col < buf_len, (col + t_new) % buf_len, col - NEW_LANES + t_new)
    keep = (col < buf_len) | (col >= buf_len + NEW_LANES - t_new)
    i = np.arange(t_new)[:, None]
    dist = buf_len + i - key[None, :]
    out = []
    for window, dil in DILATED_BRANCHES:
        ok = keep[None, :] & (dist >= 0) & (dist % dil == 0) & (dist // dil <= window // dil)
        out.append(np.where(ok, _t5_bucket_np(np.maximum(dist, 0)), -1))
    return np.stack(out).astype(np.int32)


def _bias_kernel(tab_ref, idx_ref, o_ref):
    pair = pl.program_id(1)
    idx = idx_ref[0]
    rows = idx.shape[0]
    for half in range(2):
        head = 2 * pair + half
        acc = jnp.full(idx.shape, MASKED, F32)
        for b in range(N_BUCKETS):
            acc = jnp.where(idx == b, tab_ref[b, head], acc)
        o_ref[0, 0, half * rows:(half + 1) * rows, :] = acc


def _bias_tables(table, idx):
    n_var, rows, cols = idx.shape
    return pl.pallas_call(
        _bias_kernel,
        grid=(n_var, N_PAIRS),
        in_specs=[pl.BlockSpec(memory_space=pltpu.SMEM),
                  pl.BlockSpec((1, rows, cols), lambda v, p: (v, 0, 0))],
        out_specs=pl.BlockSpec((1, 1, 2 * rows, cols), lambda v, p: (v, p, 0, 0)),
        out_shape=jax.ShapeDtypeStruct((n_var, N_PAIRS, 2 * rows, cols), F32),
        compiler_params=_params(2),
        name="bias_tables",
    )(table, jnp.asarray(idx))


def _in_proj_kernel(x_ref, g_ref, w_ref, q_ref, k_ref, v_ref, u_ref):
    h = _rms(x_ref[...], g_ref[...]).astype(BF16)
    aw = q_ref.shape[1]
    cw = u_ref.shape[1]
    q_ref[...] = _dot(h, w_ref[:, 0:aw]) * ATT_SCALE
    k_ref[...] = _dot(h, w_ref[:, aw:2 * aw])
    v_ref[...] = _dot(h, w_ref[:, 2 * aw:3 * aw])
    a = _dot(h, w_ref[:, 3 * aw:3 * aw + cw])
    gate = _dot(h, w_ref[:, 3 * aw + cw:3 * aw + 2 * cw])
    u_ref[...] = a * _sigmoid(gate)


def _in_proj(x, g, w_bf16, tm):
    n, d = x.shape
    cw = (w_bf16.shape[1] - 3 * ATT_WIDTH) // 2
    row = lambda i: (i, 0)
    fixed = lambda i: (0, 0)
    return pl.pallas_call(
        _in_proj_kernel,
        grid=(n // tm,),
        in_specs=[pl.BlockSpec((tm, d), row),
                  pl.BlockSpec((1, d), fixed),
                  pl.BlockSpec(w_bf16.shape, fixed)],
        out_specs=[pl.BlockSpec((tm, ATT_WIDTH), row)] * 3 + [pl.BlockSpec((tm, cw), row)],
        out_shape=[jax.ShapeDtypeStruct((n, ATT_WIDTH), F32)] * 3 + [jax.ShapeDtypeStruct((n, cw), F32)],
        compiler_params=_params(1),
        name="in_proj",
    )(x, g, w_bf16)


def _in_proj_prompt_kernel(x_ref, g_ref, w_ref, wkv_t_ref, q_ref, k_ref, v_ref, u_ref, kt_ref, vt_ref, xs, *,
                           tiles_per_seq, tail_first):
    tm, d = x_ref.shape
    per = tm // N_CLS
    aw = q_ref.shape[3]
    cw = u_ref.shape[1]
    g = g_ref[...]
    x = x_ref[...]
    hn = _rms(x, g).astype(BF16)
    gate = _dot(hn, w_ref[:, 3 * aw + cw:3 * aw + 2 * cw])
    u_ref[...] = _dot(hn, w_ref[:, 3 * aw:3 * aw + cw]) * _sigmoid(gate)

    for c in range(d // PAIR_W):
        xs[c] = x[:, c * PAIR_W:(c + 1) * PAIR_W]
    x_cls = jnp.concatenate(
        [jnp.concatenate([xs[c, pl.ds(r, per, stride=N_CLS), :] for r in range(N_CLS)], axis=0)
         for c in range(d // PAIR_W)], axis=1)
    h = _rms(x_cls, g).astype(BF16)
    q_ref[0] = (_dot(h, w_ref[:, 0:aw]) * ATT_SCALE).reshape(N_CLS, per, aw)
    k_ref[0] = _dot(h, w_ref[:, aw:2 * aw]).reshape(N_CLS, per, aw)
    v_ref[0] = _dot(h, w_ref[:, 2 * aw:3 * aw]).reshape(N_CLS, per, aw)

    @pl.when(pl.program_id(0) % tiles_per_seq >= tail_first)
    def _():
        kt_ref[0] = _dot_nt(wkv_t_ref[:aw, :], hn)
        vt_ref[0] = _dot_nt(wkv_t_ref[aw:, :], hn)


def _in_proj_prompt(x, g, w_bf16, wkv_t_bf16, seq, keep, tm):
    n, d = x.shape
    aw = ATT_WIDTH
    cw = (w_bf16.shape[1] - 3 * aw) // 2
    chunk = N_CLS * STEPS
    tiles_per_chunk = chunk // tm
    tiles_per_seq = seq // tm
    tail_first = (seq - keep) // tm
    row = lambda i: (i, 0)
    fixed = lambda i: (0, 0)
    cls = pl.BlockSpec((1, N_CLS, tm // N_CLS, aw), lambda i: (i // tiles_per_chunk, 0, i % tiles_per_chunk, 0))
    tail = pl.BlockSpec((1, aw, tm),
                        lambda i: (i // tiles_per_seq, 0, jnp.maximum(i % tiles_per_seq - tail_first, 0)))
    cls_shape = jax.ShapeDtypeStruct((n // chunk, N_CLS, STEPS, aw), F32)
    tail_shape = jax.ShapeDtypeStruct((n // seq, aw, keep), F32)
    return pl.pallas_call(
        functools.partial(_in_proj_prompt_kernel, tiles_per_seq=tiles_per_seq, tail_first=tail_first),
        grid=(n // tm,),
        in_specs=[pl.BlockSpec((tm, d), row), pl.BlockSpec((1, d), fixed),
                  pl.BlockSpec(w_bf16.shape, fixed), pl.BlockSpec(wkv_t_bf16.shape, fixed)],
        out_specs=[cls, cls, cls, pl.BlockSpec((tm, cw), row), tail, tail],
        out_shape=[cls_shape, cls_shape, cls_shape, jax.ShapeDtypeStruct((n, cw), F32), tail_shape, tail_shape],
        scratch_shapes=[pltpu.VMEM((d // PAIR_W, tm, PAIR_W), F32)],
        compiler_params=_params(1),
        name="in_proj_prompt",
    )(x, g, w_bf16, wkv_t_bf16)


def _stack_heads(x, first):
    zero = jnp.zeros_like(x)
    return jnp.concatenate([jnp.where(first, x, zero), jnp.where(first, zero, x)], axis=0)


def _att_kernel(q_ref, kc_ref, kp_ref, vc_ref, vp_ref, bias_ref, bias0_ref, o_ref, num_s, m_s, l_s):
    first = lax.broadcasted_iota(jnp.int32, (STEPS, PAIR_W), 1) < HEAD_DIM
    ones = jnp.ones((2 * STEPS, PAIR_W), BF16)

    def pieces(dil, cls, blk):
        groups = N_CLS // dil
        per = STEPS // groups
        return [(cls + dil * j, pl.ds(blk * per, per)) for j in range(groups)], per

    def load(ref, dil, cls, blk):
        idx, _ = pieces(dil, cls, blk)
        return jnp.concatenate([ref[0, r, rows, :] for r, rows in idx], axis=0)

    def store(ref, g, dil, cls, blk, val):
        idx, per = pieces(dil, cls, blk)
        for j, (r, rows) in enumerate(idx):
            ref[g, r, rows, :] = val[j * per:(j + 1) * per]

    def attend(g, dil, cls, blk, k_prev, v_prev, bias):
        q_st = _stack_heads(load(q_ref, dil, cls, blk), first).astype(BF16)
        kcat = jnp.concatenate([k_prev, load(kc_ref, dil, cls, blk)], axis=0).astype(BF16)
        vcat = jnp.concatenate([v_prev, load(vc_ref, dil, cls, blk)], axis=0).astype(BF16)
        s = _dot_nt(q_st, kcat) + bias
        m = jnp.max(s, axis=-1, keepdims=True)
        p = jnp.exp(s - m).astype(BF16)
        o = _dot(p, jnp.concatenate([vcat, ones], axis=1))
        store(num_s, g, dil, cls, blk, jnp.where(first, o[:STEPS, :PAIR_W], o[STEPS:, :PAIR_W]))
        store(l_s, g, dil, cls, blk, jnp.where(first, o[:STEPS, PAIR_W:], o[STEPS:, PAIR_W:]))
        store(m_s, g, dil, cls, blk, jnp.where(first, jnp.broadcast_to(m[:STEPS], (STEPS, PAIR_W)),
                                               jnp.broadcast_to(m[STEPS:], (STEPS, PAIR_W))))

    for g, (_, dil) in enumerate(DILATED_BRANCHES):
        n_blk = N_CLS // dil
        for cls in range(dil):
            attend(g, dil, cls, 0, load(kp_ref, dil, cls, n_blk - 1), load(vp_ref, dil, cls, n_blk - 1),
                   bias0_ref[0, g, 0])

            for blk in range(1, n_blk):
                attend(g, dil, cls, blk, load(kc_ref, dil, cls, blk - 1), load(vc_ref, dil, cls, blk - 1),
                       bias_ref[g, 0])

    for r in range(N_CLS):
        m_all = jnp.maximum(jnp.maximum(m_s[0, r], m_s[1, r]), m_s[2, r])
        num = jnp.zeros(m_all.shape, F32)
        den = jnp.zeros(m_all.shape, F32)
        for g in range(len(DILATED_BRANCHES)):
            w = jnp.exp(m_s[g, r] - m_all)
            num = num + w * num_s[g, r]
            den = den + w * l_s[g, r]
        o_ref[pl.ds(r, STEPS, stride=N_CLS), :] = num / den


def _dilated_attention_prompt(q, k, v, bias, chunks_per_seq):
    n_chunks = q.shape[0]
    n_br = len(DILATED_BRANCHES)
    cur = lambda b, c, p: (b * chunks_per_seq + c, 0, 0, p)
    prev = lambda b, c, p: (b * chunks_per_seq + jnp.maximum(c - 1, 0), 0, 0, p)
    blk = pl.BlockSpec((1, N_CLS, STEPS, PAIR_W), cur)
    blk_prev = pl.BlockSpec((1, N_CLS, STEPS, PAIR_W), prev)
    bias4 = bias.reshape(2, n_br, N_PAIRS, 2 * STEPS, 2 * STEPS)
    return pl.pallas_call(
        _att_kernel,
        grid=(n_chunks // chunks_per_seq, chunks_per_seq, N_PAIRS),
        in_specs=[blk, blk, blk_prev, blk, blk_prev,
                  pl.BlockSpec((n_br, 1, 2 * STEPS, 2 * STEPS), lambda b, c, p: (0, p, 0, 0)),
                  pl.BlockSpec((1, n_br, 1, 2 * STEPS, 2 * STEPS),
                               lambda b, c, p: (jnp.where(c == 0, 1, 0), 0, p, 0, 0))],
        out_specs=pl.BlockSpec((N_CLS * STEPS, PAIR_W), lambda b, c, p: (b * chunks_per_seq + c, p)),
        out_shape=jax.ShapeDtypeStruct((n_chunks * N_CLS * STEPS, ATT_WIDTH), F32),
        scratch_shapes=[pltpu.VMEM((n_br, N_CLS, STEPS, PAIR_W), F32)] * 3,
        compiler_params=_params(3),
        name="dilated_attention_prompt",
    )(q, k, k, v, v, bias4[0], bias4)


def _sample_att_kernel(q_ref, kn_ref, vn_ref, ck_ref, cv_ref, bias_ref, ok_ref, ov_ref, o_ref):
    buf_len = ck_ref.shape[2]
    t_new = q_ref.shape[0]
    tail = buf_len - NEW_LANES
    is_new = lax.broadcasted_iota(jnp.int32, (PAIR_W, NEW_LANES), 1) >= NEW_LANES - t_new

    def shift_in(c_ref, n_ref, o_ref):
        rot = pltpu.roll(c_ref[0], buf_len - t_new, axis=1)
        new = n_ref[0]
        o_ref[0, :, :tail] = rot[:, :tail]
        o_ref[0, :, tail:] = jnp.where(is_new, new, rot[:, tail:])
        return rot.astype(BF16), new.astype(BF16)

    rot_k, new_k = shift_in(ck_ref, kn_ref, ok_ref)
    rot_v, new_v = shift_in(cv_ref, vn_ref, ov_ref)

    first = lax.broadcasted_iota(jnp.int32, (t_new, PAIR_W), 1) < HEAD_DIM
    q_st = _stack_heads(q_ref[...], first).astype(BF16)
    s = jnp.concatenate([_dot(q_st, rot_k), _dot(q_st, new_k)], axis=1)
    sg = [s + bias_ref[g, 0] for g in range(len(DILATED_BRANCHES))]
    m = functools.reduce(jnp.maximum, [jnp.max(x, axis=-1, keepdims=True) for x in sg])
    p = functools.reduce(jnp.add, [jnp.exp(x - m) for x in sg])
    den = jnp.sum(p, axis=-1, keepdims=True)
    pb = p.astype(BF16)
    o = (_dot_nt(pb[:, :buf_len], rot_v) + _dot_nt(pb[:, buf_len:], new_v)) / den
    o_ref[...] = jnp.where(first, o[:t_new], o[t_new:]).astype(o_ref.dtype)


def _dilated_attention_sample(q, k_new_t, v_new_t, cache_k_t, cache_v_t, bias, t_new):
    nb, _, buf_len = cache_k_t.shape
    n_br = len(DILATED_BRANCHES)
    tok = pl.BlockSpec((t_new, PAIR_W), lambda n, p: (n, p))
    new = pl.BlockSpec((1, PAIR_W, NEW_LANES), lambda n, p: (n, p, 0))
    buf = pl.BlockSpec((1, PAIR_W, buf_len), lambda n, p: (n, p, 0))
    return pl.pallas_call(
        _sample_att_kernel,
        grid=(nb, N_PAIRS),
        in_specs=[tok, new, new, buf, buf,
                  pl.BlockSpec((n_br, 1, 2 * t_new, buf_len + NEW_LANES), lambda n, p: (0, p, 0, 0))],
        out_specs=[buf, buf, tok],
        out_shape=[jax.ShapeDtypeStruct(cache_k_t.shape, F32), jax.ShapeDtypeStruct(cache_v_t.shape, F32),
                   jax.ShapeDtypeStruct((nb * t_new, ATT_WIDTH), BF16)],
        compiler_params=_params(2),
        name="dilated_attention_sample",
    )(q, k_new_t, v_new_t, cache_k_t, cache_v_t, bias)


def _conv_tail(y, cb_ref, lg_ref, lb_ref):
    y = y + cb_ref[...]
    mu = jnp.mean(y, axis=-1, keepdims=True)
    yc = y - mu
    var = jnp.mean(yc * yc, axis=-1, keepdims=True)
    yn = yc * lax.rsqrt(var + EPS) * lg_ref[...] + lb_ref[...]
    return yn * _sigmoid(yn)


def _mix_out(x, att, c, wo_ref):
    aw = att.shape[1]
    return x + _dot(att.astype(BF16), wo_ref[:aw, :]) + _dot(c.astype(BF16), wo_ref[aw:, :])


def _cross_attend(qx, mk, mv):
    hd = qx.shape[1] // X_HEADS
    outs = []
    for h in range(X_HEADS):
        sl = slice(h * hd, (h + 1) * hd)
        s = _dot_nt(qx[:, sl].astype(BF16), mk[:, sl])
        p = jnp.exp(s - jnp.max(s, axis=-1, keepdims=True))
        den = jnp.sum(p, axis=-1, keepdims=True)
        outs.append((_dot(p.astype(BF16), mv[:, sl]) / den).astype(BF16))
    return jnp.concatenate(outs, axis=1)


def _prompt_post_kernel(x_ref, att_ref, u_ref, uh_ref, cw_ref, cb_ref, lg_ref, lb_ref, wo_ref,
                        gx_ref, wq_ref, mk_ref, mv_ref, wxo_ref, o_ref, ubuf, *, tiles_per_seq):
    tm = x_ref.shape[0]
    seq_start = (pl.program_id(0) % tiles_per_seq) == 0
    ubuf[:CONV_HALO, :] = jnp.where(seq_start, 0.0, uh_ref[...])
    ubuf[CONV_HALO:, :] = u_ref[...]
    y = jnp.zeros(u_ref.shape, F32)
    for j in range(CONV_K):
        y = y + cw_ref[j:j + 1, :] * ubuf[pl.ds(CONV_HALO - (CONV_K - 1) + j, tm), :]
    c = _conv_tail(y, cb_ref, lg_ref, lb_ref)
    x1 = _mix_out(x_ref[...], att_ref[...], c, wo_ref)
    hd = wq_ref.shape[1] // X_HEADS
    qx = _dot(_rms(x1, gx_ref[...]).astype(BF16), wq_ref[...]) * (hd ** -0.5)
    o_ref[...] = x1 + _dot(_cross_attend(qx, mk_ref[...], mv_ref[...]), wxo_ref[...])


def _prompt_post(x, att, u, conv_w, conv_b, ln_g, ln_b, w_out, gx, w_xq, mk, mv, w_xo, seq, tm):
    n, d = x.shape
    cw = u.shape[1]
    n_mem = mk.shape[0] // (n // seq)
    tiles_per_seq = seq // tm
    row = lambda i: (i, 0)
    fixed = lambda i: (0, 0)
    halo = lambda i: (jnp.maximum(i * (tm // CONV_HALO) - 1, 0), 0)
    per_seq = lambda i: (i // tiles_per_seq, 0)
    full = lambda a: pl.BlockSpec(a.shape, fixed)
    return pl.pallas_call(
        functools.partial(_prompt_post_kernel, tiles_per_seq=tiles_per_seq),
        grid=(n // tm,),
        in_specs=[pl.BlockSpec((tm, d), row), pl.BlockSpec((tm, ATT_WIDTH), row),
                  pl.BlockSpec((tm, cw), row), pl.BlockSpec((CONV_HALO, cw), halo),
                  full(conv_w), full(conv_b), full(ln_g), full(ln_b), full(w_out),
                  full(gx), full(w_xq),
                  pl.BlockSpec((n_mem, d), per_seq), pl.BlockSpec((n_mem, d), per_seq), full(w_xo)],
        out_specs=pl.BlockSpec((tm, d), row),
        out_shape=jax.ShapeDtypeStruct((n, d), F32),
        scratch_shapes=[pltpu.VMEM((CONV_HALO + tm, cw), F32)],
        compiler_params=_params(1),
        name="prompt_mix_cross",
    )(x, att, u, u, conv_w, conv_b, ln_g, ln_b, w_out, gx, w_xq, mk, mv, w_xo)


def _sample_mix_kernel(x_ref, att_ref, uf_ref, cw_ref, cb_ref, lg_ref, lb_ref, wo_ref, gx_ref, wq_ref,
                       x1_ref, qx_ref):
    nb, t_full, cw = uf_ref.shape
    t_new = t_full - (CONV_K - 1)
    y = jnp.zeros((nb, t_new, cw), F32)
    for j in range(CONV_K):
        y = y + cw_ref[j:j + 1, :] * uf_ref[:, j:j + t_new, :]
    c = _conv_tail(y.reshape(nb * t_new, cw), cb_ref, lg_ref, lb_ref)
    x1 = _mix_out(x_ref[...], att_ref[...], c, wo_ref)
    x1_ref[...] = x1
    hd = wq_ref.shape[1] // X_HEADS
    qx_ref[...] = _dot(_rms(x1, gx_ref[...]).astype(BF16), wq_ref[...]) * (hd ** -0.5)


def _sample_mix(x, att, u_full, conv_w, conv_b, ln_g, ln_b, w_out, gx, w_xq):
    n, d = x.shape
    return pl.pallas_call(
        _sample_mix_kernel,
        out_shape=[jax.ShapeDtypeStruct((n, d), F32), jax.ShapeDtypeStruct((n, w_xq.shape[1]), F32)],
        compiler_params=pltpu.CompilerParams(vmem_limit_bytes=VMEM_LIMIT),
        name="sample_mix",
    )(x, att, u_full, conv_w, conv_b, ln_g, ln_b, w_out, gx, w_xq)


def _sample_cross_kernel(qx_ref, mk_ref, mv_ref, o_ref):
    o_ref[...] = _cross_attend(qx_ref[...], mk_ref[0].astype(BF16), mv_ref[0].astype(BF16))


def _sample_cross(qx, mem_k, mem_v, t_new):
    nb, n_mem, d = mem_k.shape
    tok = pl.BlockSpec((t_new, d), lambda n: (n, 0))
    mem = pl.BlockSpec((1, n_mem, d), lambda n: (n, 0, 0))
    return pl.pallas_call(
        _sample_cross_kernel,
        grid=(nb,),
        in_specs=[tok, mem, mem],
        out_specs=tok,
        out_shape=jax.ShapeDtypeStruct(qx.shape, BF16),
        compiler_params=_params(1),
        name="sample_cross",
    )(qx, mem_k, mem_v)


def _swiglu_final(x2, gf_ref, wg_ref, wu_ref, wd_ref, gfin_ref, ff_chunk):
    h = _rms(x2, gf_ref[...]).astype(BF16)
    acc = x2
    for lo in range(0, wg_ref.shape[1], ff_chunk):
        gate = _dot(h, wg_ref[:, lo:lo + ff_chunk])
        up = _dot(h, wu_ref[:, lo:lo + ff_chunk])
        acc = acc + _dot((gate * _sigmoid(gate) * up).astype(BF16), wd_ref[lo:lo + ff_chunk, :])
    return _rms(acc, gfin_ref[...])


def _ffn_kernel(x_ref, gf_ref, wg_ref, wu_ref, wd_ref, gfin_ref, o_ref, *, ff_chunk):
    o_ref[...] = _swiglu_final(x_ref[...], gf_ref, wg_ref, wu_ref, wd_ref, gfin_ref, ff_chunk)


def _sample_tail_kernel(x_ref, xo_ref, wxo_ref, gf_ref, wg_ref, wu_ref, wd_ref, gfin_ref, o_ref, *, ff_chunk):
    x2 = x_ref[...] + _dot(xo_ref[...], wxo_ref[...])
    o_ref[...] = _swiglu_final(x2, gf_ref, wg_ref, wu_ref, wd_ref, gfin_ref, ff_chunk)


FF_CHUNK = 256


def _ffn(x, gf, wg, wu, wd, gfin, tm):
    n, d = x.shape
    row = lambda i: (i, 0)
    full = lambda a: pl.BlockSpec(a.shape, lambda i: (0, 0))
    return pl.pallas_call(
        functools.partial(_ffn_kernel, ff_chunk=FF_CHUNK),
        grid=(n // tm,),
        in_specs=[pl.BlockSpec((tm, d), row), full(gf), full(wg), full(wu), full(wd), full(gfin)],
        out_specs=pl.BlockSpec((tm, d), row),
        out_shape=jax.ShapeDtypeStruct((n, d), F32),
        compiler_params=_params(1),
        name="swiglu_final_norm",
    )(x, gf, wg, wu, wd, gfin)


def _sample_tail(x1, xo, w_xo, gf, wg, wu, wd, gfin):
    return pl.pallas_call(
        functools.partial(_sample_tail_kernel, ff_chunk=FF_CHUNK),
        out_shape=jax.ShapeDtypeStruct(x1.shape, F32),
        compiler_params=pltpu.CompilerParams(vmem_limit_bytes=VMEM_LIMIT),
        name="sample_cross_out_swiglu",
    )(x1, xo, w_xo, gf, wg, wu, wd, gfin)


def _mem_kv_kernel(m_ref, g_ref, wk_ref, wv_ref, k_ref, v_ref, kb_ref, vb_ref):
    h = _rms(m_ref[...], g_ref[...]).astype(BF16)
    k = _dot(h, wk_ref[...])
    v = _dot(h, wv_ref[...])
    k_ref[...] = k
    v_ref[...] = v
    kb_ref[...] = k.astype(BF16)
    vb_ref[...] = v.astype(BF16)


def _mem_kv(mem, g, w_k, w_v, tm):
    n, d = mem.shape
    row = lambda i: (i, 0)
    full = lambda a: pl.BlockSpec(a.shape, lambda i: (0, 0))
    dk = w_k.shape[1]
    return pl.pallas_call(
        _mem_kv_kernel,
        grid=(n // tm,),
        in_specs=[pl.BlockSpec((tm, d), row), full(g), full(w_k), full(w_v)],
        out_specs=[pl.BlockSpec((tm, dk), row)] * 4,
        out_shape=[jax.ShapeDtypeStruct((n, dk), F32)] * 2 + [jax.ShapeDtypeStruct((n, dk), BF16)] * 2,
        compiler_params=_params(1),
        name="mem_kv",
    )(mem, g, w_k, w_v)


ROW_TILE = 512
ATT_CHUNK = STEPS * DILATED_BRANCHES[-1][1]


def kernel(x_prompt, x_sample, mem_prompt, cache_win_k, cache_win_v, cache_conv, cache_mem_k, cache_mem_v,
           rpb_table, norm_mix_g, w_in, conv_w, conv_b, conv_ln_g, conv_ln_b, w_out, norm_x_g, norm_mem_g,
           w_xq, w_xk, w_xv, w_xo, norm_ffn_g, w_ffn_gate, w_ffn_up, w_ffn_down, norm_final_g):
    depth = w_in.shape[0]
    assert depth == 1, "single-layer stack"
    batch, seq, d = x_prompt.shape
    nb, t_new, _ = x_sample.shape
    buf_len = cache_win_k.shape[2]
    keep_p = min(MAX_DISTANCE, seq)
    n_mem = mem_prompt.shape[1]
    conv_hist = CONV_K - 1
    assert seq % ATT_CHUNK == 0 and keep_p % ATT_CHUNK == 0 and buf_len == MAX_DISTANCE

    row = lambda a: a.reshape(1, -1)
    bf = lambda a: a.astype(BF16)
    l = 0
    w_in_b, w_out_b = bf(w_in[l]), bf(w_out[l])
    w_xq_b, w_xk_b, w_xv_b, w_xo_b = bf(w_xq[l]), bf(w_xk[l]), bf(w_xv[l]), bf(w_xo[l])
    w_g_b, w_u_b, w_d_b = bf(w_ffn_gate[l]), bf(w_ffn_up[l]), bf(w_ffn_down[l])
    g_mix, g_x, g_mem, g_ffn, g_fin = (row(norm_mix_g[l]), row(norm_x_g[l]), row(norm_mem_g[l]),
                                       row(norm_ffn_g[l]), row(norm_final_g))
    cv_w, cv_b, ln_g, ln_b = conv_w[l], row(conv_b[l]), row(conv_ln_g[l]), row(conv_ln_b[l])

    bias_p = _bias_tables(rpb_table, _prompt_bucket_index())
    bias_s = _bias_tables(rpb_table, _sample_bucket_index(buf_len, t_new))

    xp = x_prompt.reshape(batch * seq, d)
    wkv_t_b = bf(jnp.transpose(w_in[l][:, ATT_WIDTH:3 * ATT_WIDTH]))
    q, k, v, u, p_wk_t, p_wv_t = _in_proj_prompt(xp, g_mix, w_in_b, wkv_t_b, seq, keep_p, ROW_TILE)
    att = _dilated_attention_prompt(q, k, v, bias_p, seq // ATT_CHUNK)
    mk, mv, mk_b, mv_b = _mem_kv(mem_prompt.reshape(batch * n_mem, d), g_mem, w_xk_b, w_xv_b, n_mem)
    x2 = _prompt_post(xp, att, u, cv_w, cv_b, ln_g, ln_b, w_out_b, g_x, w_xq_b, mk_b, mv_b, w_xo_b,
                      seq, ROW_TILE)
    y_prompt = _ffn(x2, g_ffn, w_g_b, w_u_b, w_d_b, g_fin, ROW_TILE).reshape(batch, seq, d)

    tok_major = lambda a: jnp.transpose(a.reshape(a.shape[0], N_ATT_HEADS, HEAD_DIM, -1), (0, 3, 1, 2))[None]
    p_wk, p_wv = tok_major(p_wk_t), tok_major(p_wv_t)
    u3 = u.reshape(batch, seq, -1)
    p_conv = u3[:, seq - conv_hist:][None]
    xh = lambda a, n: a.reshape(1, n, n_mem, X_HEADS, -1)
    p_mk, p_mv = xh(mk, batch), xh(mv, batch)

    xs = x_sample.reshape(nb * t_new, d)
    qs, ks, vs, us = _in_proj(xs, g_mix, w_in_b, nb * t_new)
    chan_major = lambda a: jnp.transpose(a, (0, 2, 3, 1)).reshape(nb, ATT_WIDTH, -1)
    new_rows = lambda a: jnp.pad(jnp.transpose(a.reshape(nb, t_new, ATT_WIDTH), (0, 2, 1)),
                                 ((0, 0), (0, 0), (NEW_LANES - t_new, 0)))
    s_wk_t, s_wv_t, att_s = _dilated_attention_sample(
        qs, new_rows(ks), new_rows(vs), chan_major(cache_win_k[l]), chan_major(cache_win_v[l]), bias_s, t_new)
    u_full = jnp.concatenate([cache_conv[l], us.reshape(nb, t_new, -1)], axis=1)
    x1s, qxs = _sample_mix(xs, att_s, u_full, cv_w, cv_b, ln_g, ln_b, w_out_b, g_x, w_xq_b)
    xo_s = _sample_cross(qxs, cache_mem_k[l].reshape(nb, n_mem, -1), cache_mem_v[l].reshape(nb, n_mem, -1), t_new)
    y_sample = _sample_tail(x1s, xo_s, w_xo_b, g_ffn, w_g_b, w_u_b, w_d_b, g_fin).reshape(nb, t_new, d)

    return (y_prompt, y_sample, p_wk, p_wv, p_conv, p_mk, p_mv,
            tok_major(s_wk_t), tok_major(s_wv_t), u_full[:, t_new:][None])
```

```python
import functools
import math

import numpy as np
import jax
import jax.numpy as jnp
from jax import lax
from jax.experimental import pallas as pl
from jax.experimental.pallas import tpu as pltpu

F32 = jnp.float32
BF16 = jnp.bfloat16

HEAD_DIM = 64
N_ATT_HEADS = 12
ATT_WIDTH = N_ATT_HEADS * HEAD_DIM
PAIR_W = 2 * HEAD_DIM
N_PAIRS = N_ATT_HEADS // 2
CONV_K = 31
CONV_HALO = 32
F32_SUBLANES = 8
DILATED_BRANCHES = ((128, 1), (512, 4), (2048, 16))
STEPS = 128
N_CLS = DILATED_BRANCHES[-1][1]
NEW_LANES = 128
N_BUCKETS = 32
MAX_EXACT = N_BUCKETS // 2
MAX_DISTANCE = 2048
X_HEADS = 4
EPS = 1e-6
ATT_SCALE = HEAD_DIM ** -0.5
MASKED = -1e30

V7X_VMEM_BYTES = 64 * 1024 * 1024
VMEM_LIMIT = V7X_VMEM_BYTES * 3 // 4


def _params(n_grid_dims):
    return pltpu.CompilerParams(dimension_semantics=("arbitrary",) * n_grid_dims,
                                vmem_limit_bytes=VMEM_LIMIT)


def _rms(x, g):
    return x * lax.rsqrt(jnp.mean(x * x, axis=-1, keepdims=True) + EPS) * g


def _sigmoid(x):
    return 1.0 / (1.0 + jnp.exp(-x))


def _dot(a, b):
    return jnp.dot(a, b, preferred_element_type=F32)


def _dot_nt(a, b):
    return lax.dot_general(a, b, (((1,), (1,)), ((), ())), preferred_element_type=F32)


def _t5_bucket_np(dist):
    n = dist.astype(np.int32)
    nf = np.maximum(n, MAX_EXACT).astype(np.float32)
    large = MAX_EXACT + (np.log(nf / np.float32(MAX_EXACT)) / np.float32(math.log(MAX_DISTANCE / MAX_EXACT))
                         * np.float32(N_BUCKETS - MAX_EXACT)).astype(np.int32)
    large = np.minimum(large, N_BUCKETS - 1)
    return np.where(n < MAX_EXACT, n, large)


def _block_order(dil):
    groups = N_CLS // dil
    per = STEPS // groups
    i = np.arange(STEPS)
    return (i % per) * groups + i // per


def _prompt_bucket_index():
    out = []
    for _, dil in DILATED_BRANCHES:
        n = _block_order(dil)
        kj = np.concatenate([n, STEPS + n])[None, :]
        sub = STEPS + n[:, None] - kj
        band = (sub >= 0) & (sub <= STEPS)
        out.append(np.where(band, _t5_bucket_np(dil * np.maximum(sub, 0)), -1))
    return np.stack(out).astype(np.int32)


def _sample_bucket_index(buf_len, t_new):
    col = np.arange(buf_len + NEW_LANES)
    key = np.where(col < buf_len, (col + t_new) % buf_len, col - NEW_LANES + t_new)
    keep = (col < buf_len) | (col >= buf_len + NEW_LANES - t_new)
    i = np.arange(t_new)[:, None]
    dist = buf_len + i - key[None, :]
    out = []
    for window, dil in DILATED_BRANCHES:
        ok = keep[None, :] & (dist >= 0) & (dist % dil == 0) & (dist // dil <= window // dil)
        out.append(np.where(ok, _t5_bucket_np(np.maximum(dist, 0)), -1))
    return np.stack(out).astype(np.int32)


def _bias_kernel(tab_ref, idx_ref, o_ref, *, masked_cols):
    pair = pl.program_id(1)
    idx = idx_ref[0]
    rows = idx.shape[0]
    for half in range(2):
        head = 2 * pair + half
        acc = jnp.full(idx.shape, MASKED, F32)
        for b in range(N_BUCKETS):
            acc = jnp.where(idx == b, tab_ref[b, head], acc)
        o_ref[0, 0, 0, half * rows:(half + 1) * rows, :] = acc
        if masked_cols:
            col = lax.broadcasted_iota(jnp.int32, idx.shape, 1)
            o_ref[1, 0, 0, half * rows:(half + 1) * rows, :] = jnp.where(col < masked_cols, MASKED, acc)


def _bias_tables(table, idx, masked_cols=0):
    n_var, rows, cols = idx.shape
    n_out = 2 if masked_cols else 1
    return pl.pallas_call(
        functools.partial(_bias_kernel, masked_cols=masked_cols),
        grid=(n_var, N_PAIRS),
        in_specs=[pl.BlockSpec(memory_space=pltpu.SMEM),
                  pl.BlockSpec((1, rows, cols), lambda v, p: (v, 0, 0))],
        out_specs=pl.BlockSpec((n_out, 1, 1, 2 * rows, cols), lambda v, p: (0, v, p, 0, 0)),
        out_shape=jax.ShapeDtypeStruct((n_out, n_var, N_PAIRS, 2 * rows, cols), F32),
        compiler_params=_params(2),
        name="bias_tables",
    )(table, jnp.asarray(idx))


def _in_proj_kernel(x_ref, g_ref, w_ref, q_ref, k_ref, v_ref, u_ref):
    h = _rms(x_ref[...], g_ref[...]).astype(BF16)
    aw = q_ref.shape[1]
    cw = u_ref.shape[1]
    q_ref[...] = _dot(h, w_ref[:, 0:aw]) * ATT_SCALE
    k_ref[...] = _dot(h, w_ref[:, aw:2 * aw])
    v_ref[...] = _dot(h, w_ref[:, 2 * aw:3 * aw])
    a = _dot(h, w_ref[:, 3 * aw:3 * aw + cw])
    gate = _dot(h, w_ref[:, 3 * aw + cw:3 * aw + 2 * cw])
    u_ref[...] = a * _sigmoid(gate)


def _in_proj(x, g, w_bf16, tm):
    n, d = x.shape
    cw = (w_bf16.shape[1] - 3 * ATT_WIDTH) // 2
    row = lambda i: (i, 0)
    fixed = lambda i: (0, 0)
    return pl.pallas_call(
        _in_proj_kernel,
        grid=(n // tm,),
        in_specs=[pl.BlockSpec((tm, d), row),
                  pl.BlockSpec((1, d), fixed),
                  pl.BlockSpec(w_bf16.shape, fixed)],
        out_specs=[pl.BlockSpec((tm, ATT_WIDTH), row)] * 3 + [pl.BlockSpec((tm, cw), row)],
        out_shape=[jax.ShapeDtypeStruct((n, ATT_WIDTH), F32)] * 3 + [jax.ShapeDtypeStruct((n, cw), F32)],
        compiler_params=_params(1),
        name="in_proj",
    )(x, g, w_bf16)


def _in_proj_prompt_kernel(x_ref, g_ref, w_ref, wkv_t_ref, q_ref, k_ref, v_ref, u_ref, kt_ref, vt_ref, xs, *,
                           tiles_per_seq, tail_first):
    tm, d = x_ref.shape
    per = tm // N_CLS
    aw = q_ref.shape[3]
    cw = u_ref.shape[1]
    g = g_ref[...]
    x = x_ref[...]
    hn = _rms(x, g).astype(BF16)
    gate = _dot(hn, w_ref[:, 3 * aw + cw:3 * aw + 2 * cw])
    u_ref[...] = _dot(hn, w_ref[:, 3 * aw:3 * aw + cw]) * _sigmoid(gate)

    for c in range(d // PAIR_W):
        xs[c] = x[:, c * PAIR_W:(c + 1) * PAIR_W]
    x_cls = jnp.concatenate(
        [jnp.concatenate([xs[c, pl.ds(r, per, stride=N_CLS), :] for r in range(N_CLS)], axis=0)
         for c in range(d // PAIR_W)], axis=1)
    h = _rms(x_cls, g).astype(BF16)
    q_ref[0] = (_dot(h, w_ref[:, 0:aw]) * ATT_SCALE).reshape(N_CLS, per, aw)
    k_ref[0] = _dot(h, w_ref[:, aw:2 * aw]).reshape(N_CLS, per, aw)
    v_ref[0] = _dot(h, w_ref[:, 2 * aw:3 * aw]).reshape(N_CLS, per, aw)

    @pl.when(pl.program_id(0) % tiles_per_seq >= tail_first)
    def _():
        kt_ref[0] = _dot_nt(wkv_t_ref[:aw, :], hn)
        vt_ref[0] = _dot_nt(wkv_t_ref[aw:, :], hn)


def _in_proj_prompt(x, g, w_bf16, wkv_t_bf16, seq, keep, tm):
    n, d = x.shape
    aw = ATT_WIDTH
    cw = (w_bf16.shape[1] - 3 * aw) // 2
    chunk = N_CLS * STEPS
    tiles_per_chunk = chunk // tm
    tiles_per_seq = seq // tm
    tail_first = (seq - keep) // tm
    row = lambda i: (i, 0)
    fixed = lambda i: (0, 0)
    cls = pl.BlockSpec((1, N_CLS, tm // N_CLS, aw), lambda i: (i // tiles_per_chunk, 0, i % tiles_per_chunk, 0))
    tail = pl.BlockSpec((1, aw, tm),
                        lambda i: (i // tiles_per_seq, 0, jnp.maximum(i % tiles_per_seq - tail_first, 0)))
    cls_shape = jax.ShapeDtypeStruct((n // chunk, N_CLS, STEPS, aw), F32)
    tail_shape = jax.ShapeDtypeStruct((n // seq, aw, keep), F32)
    return pl.pallas_call(
        functools.partial(_in_proj_prompt_kernel, tiles_per_seq=tiles_per_seq, tail_first=tail_first),
        grid=(n // tm,),
        in_specs=[pl.BlockSpec((tm, d), row), pl.BlockSpec((1, d), fixed),
                  pl.BlockSpec(w_bf16.shape, fixed), pl.BlockSpec(wkv_t_bf16.shape, fixed)],
        out_specs=[cls, cls, cls, pl.BlockSpec((tm, cw), row), tail, tail],
        out_shape=[cls_shape, cls_shape, cls_shape, jax.ShapeDtypeStruct((n, cw), F32), tail_shape, tail_shape],
        scratch_shapes=[pltpu.VMEM((d // PAIR_W, tm, PAIR_W), F32)],
        compiler_params=_params(1),
        name="in_proj_prompt",
    )(x, g, w_bf16, wkv_t_bf16)


def _stack_heads(x, first):
    zero = jnp.zeros_like(x)
    return jnp.concatenate([jnp.where(first, x, zero), jnp.where(first, zero, x)], axis=0)


def _att_kernel(q_ref, kc_ref, kp_ref, vc_ref, vp_ref, bias_ref, bias0_ref, o_ref, num_s, m_s, l_s):
    first = lax.broadcasted_iota(jnp.int32, (STEPS, PAIR_W), 1) < HEAD_DIM
    ones = jnp.ones((2 * STEPS, PAIR_W), BF16)

    def pieces(dil, cls, blk):
        groups = N_CLS // dil
        per = STEPS // groups
        return [(cls + dil * j, pl.ds(blk * per, per)) for j in range(groups)], per

    def load(ref, dil, cls, blk):
        idx, _ = pieces(dil, cls, blk)
        return jnp.concatenate([ref[0, r, rows, :] for r, rows in idx], axis=0)

    def store(ref, g, dil, cls, blk, val):
        idx, per = pieces(dil, cls, blk)
        for j, (r, rows) in enumerate(idx):
            ref[g, r, rows, :] = val[j * per:(j + 1) * per]

    def attend(g, dil, cls, blk, k_prev, v_prev, bias):
        q_st = _stack_heads(load(q_ref, dil, cls, blk), first).astype(BF16)
        kcat = jnp.concatenate([k_prev, load(kc_ref, dil, cls, blk)], axis=0).astype(BF16)
        vcat = jnp.concatenate([v_prev, load(vc_ref, dil, cls, blk)], axis=0).astype(BF16)
        s = _dot_nt(q_st, kcat) + bias
        m = jnp.max(s, axis=-1, keepdims=True)
        p = jnp.exp(s - m).astype(BF16)
        o = _dot(p, jnp.concatenate([vcat, ones], axis=1))
        store(num_s, g, dil, cls, blk, jnp.where(first, o[:STEPS, :PAIR_W], o[STEPS:, :PAIR_W]))
        store(l_s, g, dil, cls, blk, jnp.where(first, o[:STEPS, PAIR_W:], o[STEPS:, PAIR_W:]))
        store(m_s, g, dil, cls, blk, jnp.where(first, jnp.broadcast_to(m[:STEPS], (STEPS, PAIR_W)),
                                               jnp.broadcast_to(m[STEPS:], (STEPS, PAIR_W))))

    for g, (_, dil) in enumerate(DILATED_BRANCHES):
        n_blk = N_CLS // dil
        for cls in range(dil):
            attend(g, dil, cls, 0, load(kp_ref, dil, cls, n_blk - 1), load(vp_ref, dil, cls, n_blk - 1),
                   bias0_ref[0, g, 0])

            for blk in range(1, n_blk):
                attend(g, dil, cls, blk, load(kc_ref, dil, cls, blk - 1), load(vc_ref, dil, cls, blk - 1),
                       bias_ref[g, 0])

    for r in range(N_CLS):
        m_all = jnp.maximum(jnp.maximum(m_s[0, r], m_s[1, r]), m_s[2, r])
        num = jnp.zeros(m_all.shape, F32)
        den = jnp.zeros(m_all.shape, F32)
        for g in range(len(DILATED_BRANCHES)):
            w = jnp.exp(m_s[g, r] - m_all)
            num = num + w * num_s[g, r]
            den = den + w * l_s[g, r]
        o_ref[pl.ds(r, STEPS, stride=N_CLS), :] = num / den


def _dilated_attention_prompt(q, k, v, bias, chunks_per_seq):
    n_chunks = q.shape[0]
    n_br = len(DILATED_BRANCHES)
    cur = lambda b, c, p: (b * chunks_per_seq + c, 0, 0, p)
    prev = lambda b, c, p: (b * chunks_per_seq + jnp.maximum(c - 1, 0), 0, 0, p)
    blk = pl.BlockSpec((1, N_CLS, STEPS, PAIR_W), cur)
    blk_prev = pl.BlockSpec((1, N_CLS, STEPS, PAIR_W), prev)
    return pl.pallas_call(
        _att_kernel,
        grid=(n_chunks // chunks_per_seq, chunks_per_seq, N_PAIRS),
        in_specs=[blk, blk, blk_prev, blk, blk_prev,
                  pl.BlockSpec((n_br, 1, 2 * STEPS, 2 * STEPS), lambda b, c, p: (0, p, 0, 0)),
                  pl.BlockSpec((1, n_br, 1, 2 * STEPS, 2 * STEPS),
                               lambda b, c, p: (jnp.where(c == 0, 1, 0), 0, p, 0, 0))],
        out_specs=pl.BlockSpec((N_CLS * STEPS, PAIR_W), lambda b, c, p: (b * chunks_per_seq + c, p)),
        out_shape=jax.ShapeDtypeStruct((n_chunks * N_CLS * STEPS, ATT_WIDTH), F32),
        scratch_shapes=[pltpu.VMEM((n_br, N_CLS, STEPS, PAIR_W), F32)] * 3,
        compiler_params=_params(3),
        name="dilated_attention_prompt",
    )(q, k, k, v, v, bias[0], bias)


def _sample_att_kernel(q_ref, kn_ref, vn_ref, ck_ref, cv_ref, bias_ref, ok_ref, ov_ref, o_ref):
    buf_len = ck_ref.shape[2]
    t_new = q_ref.shape[0]
    tail = buf_len - NEW_LANES
    is_new = lax.broadcasted_iota(jnp.int32, (PAIR_W, NEW_LANES), 1) >= NEW_LANES - t_new
    first = lax.broadcasted_iota(jnp.int32, (t_new, PAIR_W), 1) < HEAD_DIM

    for j in range(ck_ref.shape[1] // PAIR_W):
        ch = slice(j * PAIR_W, (j + 1) * PAIR_W)

        def shift_in(c_ref, n_ref, o_ref):
            rot = pltpu.roll(c_ref[0, ch, :], buf_len - t_new, axis=1)
            new = n_ref[0, ch, :]
            o_ref[0, ch, :tail] = rot[:, :tail]
            o_ref[0, ch, tail:] = jnp.where(is_new, new, rot[:, tail:])
            return rot.astype(BF16), new.astype(BF16)

        rot_k, new_k = shift_in(ck_ref, kn_ref, ok_ref)
        rot_v, new_v = shift_in(cv_ref, vn_ref, ov_ref)

        q_st = _stack_heads(q_ref[:, ch], first).astype(BF16)
        s = jnp.concatenate([_dot(q_st, rot_k), _dot(q_st, new_k)], axis=1)
        sg = [s + bias_ref[g, j] for g in range(len(DILATED_BRANCHES))]
        m = functools.reduce(jnp.maximum, [jnp.max(x, axis=-1, keepdims=True) for x in sg])
        p = functools.reduce(jnp.add, [jnp.exp(x - m) for x in sg])
        den = jnp.sum(p, axis=-1, keepdims=True)
        pb = p.astype(BF16)
        o = (_dot_nt(pb[:, :buf_len], rot_v) + _dot_nt(pb[:, buf_len:], new_v)) / den
        o_ref[:, ch] = jnp.where(first, o[:t_new], o[t_new:]).astype(o_ref.dtype)


SAMPLE_PAIRS_PER_STEP = 2


def _dilated_attention_sample(q, k_new_t, v_new_t, cache_k_t, cache_v_t, bias, t_new):
    nb, _, buf_len = cache_k_t.shape
    n_br = len(DILATED_BRANCHES)
    pps = SAMPLE_PAIRS_PER_STEP
    width = pps * PAIR_W
    tok = pl.BlockSpec((t_new, width), lambda n, p: (n, p))
    new = pl.BlockSpec((1, width, NEW_LANES), lambda n, p: (n, p, 0))
    buf = pl.BlockSpec((1, width, buf_len), lambda n, p: (n, p, 0))
    return pl.pallas_call(
        _sample_att_kernel,
        grid=(nb, N_PAIRS // pps),
        in_specs=[tok, new, new, buf, buf,
                  pl.BlockSpec((n_br, pps, 2 * t_new, buf_len + NEW_LANES), lambda n, p: (0, p, 0, 0))],
        out_specs=[buf, buf, tok],
        out_shape=[jax.ShapeDtypeStruct(cache_k_t.shape, F32), jax.ShapeDtypeStruct(cache_v_t.shape, F32),
                   jax.ShapeDtypeStruct((nb * t_new, ATT_WIDTH), BF16)],
        compiler_params=_params(2),
        name="dilated_attention_sample",
    )(q, k_new_t, v_new_t, cache_k_t, cache_v_t, bias)


def _conv_tail(y, cb_ref, lg_ref, lb_ref):
    y = y + cb_ref[...]
    mu = jnp.mean(y, axis=-1, keepdims=True)
    yc = y - mu
    var = jnp.mean(yc * yc, axis=-1, keepdims=True)
    yn = yc * lax.rsqrt(var + EPS) * lg_ref[...] + lb_ref[...]
    return yn * _sigmoid(yn)


def _mix_out(x, att, c, wo_ref):
    aw = att.shape[1]
    return x + _dot(att.astype(BF16), wo_ref[:aw, :]) + _dot(c.astype(BF16), wo_ref[aw:, :])


def _cross_attend(qx, head_kv):
    hd = qx.shape[1] // X_HEADS
    outs = []
    for h in range(X_HEADS):
        mk, mv = head_kv(h)
        s = _dot_nt(qx[:, h * hd:(h + 1) * hd].astype(BF16), mk)
        p = jnp.exp(s - jnp.max(s, axis=-1, keepdims=True))
        den = jnp.sum(p, axis=-1, keepdims=True)
        outs.append((_dot(p.astype(BF16), mv) / den).astype(BF16))
    return jnp.concatenate(outs, axis=1)


def _prompt_post_kernel(x_ref, att_ref, u_ref, uh_ref, cw_ref, cb_ref, lg_ref, lb_ref, wo_ref,
                        gx_ref, wq_ref, mk_ref, mv_ref, wxo_ref, o_ref, ubuf, *, tiles_per_seq):
    tm = x_ref.shape[0]
    seq_start = (pl.program_id(0) % tiles_per_seq) == 0
    ubuf[:CONV_HALO, :] = jnp.where(seq_start, 0.0, uh_ref[...])
    ubuf[CONV_HALO:, :] = u_ref[...]
    hist = ubuf[...]
    n_rows = hist.shape[0]
    first_tap = CONV_HALO - (CONV_K - 1)
    y = jnp.zeros(u_ref.shape, F32)
    for s in range(F32_SUBLANES):
        shifted = hist if s == 0 else pltpu.roll(hist, n_rows - s, axis=0)
        for j in range(CONV_K):
            if (first_tap + j) % F32_SUBLANES == s:
                lo = first_tap + j - s
                y = y + cw_ref[j:j + 1, :] * shifted[lo:lo + tm]
    c = _conv_tail(y, cb_ref, lg_ref, lb_ref)
    x1 = _mix_out(x_ref[...], att_ref[...], c, wo_ref)
    hd = wq_ref.shape[1] // X_HEADS
    qx = _dot(_rms(x1, gx_ref[...]).astype(BF16), wq_ref[...]) * (hd ** -0.5)
    head_kv = lambda h: (mk_ref[:, h * hd:(h + 1) * hd], mv_ref[:, h * hd:(h + 1) * hd])
    o_ref[...] = x1 + _dot(_cross_attend(qx, head_kv), wxo_ref[...])


def _prompt_post(x, att, u, conv_w, conv_b, ln_g, ln_b, w_out, gx, w_xq, mk, mv, w_xo, seq, tm):
    n, d = x.shape
    cw = u.shape[1]
    n_mem = mk.shape[0] // (n // seq)
    tiles_per_seq = seq // tm
    row = lambda i: (i, 0)
    fixed = lambda i: (0, 0)
    halo = lambda i: (jnp.maximum(i * (tm // CONV_HALO) - 1, 0), 0)
    per_seq = lambda i: (i // tiles_per_seq, 0)
    full = lambda a: pl.BlockSpec(a.shape, fixed)
    return pl.pallas_call(
        functools.partial(_prompt_post_kernel, tiles_per_seq=tiles_per_seq),
        grid=(n // tm,),
        in_specs=[pl.BlockSpec((tm, d), row), pl.BlockSpec((tm, ATT_WIDTH), row),
                  pl.BlockSpec((tm, cw), row), pl.BlockSpec((CONV_HALO, cw), halo),
                  full(conv_w), full(conv_b), full(ln_g), full(ln_b), full(w_out),
                  full(gx), full(w_xq),
                  pl.BlockSpec((n_mem, d), per_seq), pl.BlockSpec((n_mem, d), per_seq), full(w_xo)],
        out_specs=pl.BlockSpec((tm, d), row),
        out_shape=jax.ShapeDtypeStruct((n, d), F32),
        scratch_shapes=[pltpu.VMEM((CONV_HALO + tm, cw), F32)],
        compiler_params=_params(1),
        name="prompt_mix_cross",
    )(x, att, u, u, conv_w, conv_b, ln_g, ln_b, w_out, gx, w_xq, mk, mv, w_xo)


def _sample_mix_kernel(x_ref, att_ref, uf_ref, cw_ref, cb_ref, lg_ref, lb_ref, wo_ref, gx_ref, wq_ref,
                       x1_ref, qx_ref):
    nb, t_full, cw = uf_ref.shape
    t_new = t_full - (CONV_K - 1)
    y = jnp.zeros((nb, t_new, cw), F32)
    for j in range(CONV_K):
        y = y + cw_ref[j:j + 1, :] * uf_ref[:, j:j + t_new, :]
    c = _conv_tail(y.reshape(nb * t_new, cw), cb_ref, lg_ref, lb_ref)
    x1 = _mix_out(x_ref[...], att_ref[...], c, wo_ref)
    x1_ref[...] = x1
    hd = wq_ref.shape[1] // X_HEADS
    qx_ref[...] = _dot(_rms(x1, gx_ref[...]).astype(BF16), wq_ref[...]) * (hd ** -0.5)


def _sample_mix(x, att, u_full, conv_w, conv_b, ln_g, ln_b, w_out, gx, w_xq):
    n, d = x.shape
    return pl.pallas_call(
        _sample_mix_kernel,
        out_shape=[jax.ShapeDtypeStruct((n, d), F32), jax.ShapeDtypeStruct((n, w_xq.shape[1]), F32)],
        compiler_params=pltpu.CompilerParams(vmem_limit_bytes=VMEM_LIMIT),
        name="sample_mix",
    )(x, att, u_full, conv_w, conv_b, ln_g, ln_b, w_out, gx, w_xq)


def _sample_cross_kernel(qx_ref, mk_ref, mv_ref, o_ref):
    hd = mk_ref.shape[2] // X_HEADS
    head_kv = lambda h: (mk_ref[0, :, h * hd:(h + 1) * hd].astype(BF16),
                         mv_ref[0, :, h * hd:(h + 1) * hd].astype(BF16))
    o_ref[...] = _cross_attend(qx_ref[...], head_kv)


def _sample_cross(qx, mem_k, mem_v, t_new):
    nb, n_mem, d = mem_k.shape
    tok = pl.BlockSpec((t_new, d), lambda n: (n, 0))
    mem = pl.BlockSpec((1, n_mem, d), lambda n: (n, 0, 0))
    return pl.pallas_call(
        _sample_cross_kernel,
        grid=(nb,),
        in_specs=[tok, mem, mem],
        out_specs=tok,
        out_shape=jax.ShapeDtypeStruct(qx.shape, BF16),
        compiler_params=_params(1),
        name="sample_cross",
    )(qx, mem_k, mem_v)


def _swiglu_final(x2, gf_ref, wg_ref, wu_ref, wd_ref, gfin_ref, ff_chunk):
    h = _rms(x2, gf_ref[...]).astype(BF16)
    acc = x2
    for lo in range(0, wg_ref.shape[1], ff_chunk):
        gate = _dot(h, wg_ref[:, lo:lo + ff_chunk])
        up = _dot(h, wu_ref[:, lo:lo + ff_chunk])
        acc = acc + _dot((gate * _sigmoid(gate) * up).astype(BF16), wd_ref[lo:lo + ff_chunk, :])
    return _rms(acc, gfin_ref[...])


def _ffn_kernel(x_ref, gf_ref, wg_ref, wu_ref, wd_ref, gfin_ref, o_ref, *, ff_chunk):
    o_ref[...] = _swiglu_final(x_ref[...], gf_ref, wg_ref, wu_ref, wd_ref, gfin_ref, ff_chunk)


def _sample_tail_kernel(x_ref, xo_ref, wxo_ref, gf_ref, wg_ref, wu_ref, wd_ref, gfin_ref, o_ref, *, ff_chunk):
    x2 = x_ref[...] + _dot(xo_ref[...], wxo_ref[...])
    o_ref[...] = _swiglu_final(x2, gf_ref, wg_ref, wu_ref, wd_ref, gfin_ref, ff_chunk)


FF_CHUNK = 256


def _ffn(x, gf, wg, wu, wd, gfin, tm):
    n, d = x.shape
    row = lambda i: (i, 0)
    full = lambda a: pl.BlockSpec(a.shape, lambda i: (0, 0))
    return pl.pallas_call(
        functools.partial(_ffn_kernel, ff_chunk=FF_CHUNK),
        grid=(n // tm,),
        in_specs=[pl.BlockSpec((tm, d), row), full(gf), full(wg), full(wu), full(wd), full(gfin)],
        out_specs=pl.BlockSpec((tm, d), row),
        out_shape=jax.ShapeDtypeStruct((n, d), F32),
        compiler_params=_params(1),
        name="swiglu_final_norm",
    )(x, gf, wg, wu, wd, gfin)


def _sample_tail(x1, xo, w_xo, gf, wg, wu, wd, gfin):
    return pl.pallas_call(
        functools.partial(_sample_tail_kernel, ff_chunk=FF_CHUNK),
        out_shape=jax.ShapeDtypeStruct(x1.shape, F32),
        compiler_params=pltpu.CompilerParams(vmem_limit_bytes=VMEM_LIMIT),
        name="sample_cross_out_swiglu",
    )(x1, xo, w_xo, gf, wg, wu, wd, gfin)


def _mem_kv_kernel(m_ref, g_ref, wk_ref, wv_ref, k_ref, v_ref, kb_ref, vb_ref):
    h = _rms(m_ref[...], g_ref[...]).astype(BF16)
    k = _dot(h, wk_ref[...])
    v = _dot(h, wv_ref[...])
    k_ref[...] = k
    v_ref[...] = v
    kb_ref[...] = k.astype(BF16)
    vb_ref[...] = v.astype(BF16)


def _mem_kv(mem, g, w_k, w_v, tm):
    n, d = mem.shape
    row = lambda i: (i, 0)
    full = lambda a: pl.BlockSpec(a.shape, lambda i: (0, 0))
    dk = w_k.shape[1]
    return pl.pallas_call(
        _mem_kv_kernel,
        grid=(n // tm,),
        in_specs=[pl.BlockSpec((tm, d), row), full(g), full(w_k), full(w_v)],
        out_specs=[pl.BlockSpec((tm, dk), row)] * 4,
        out_shape=[jax.ShapeDtypeStruct((n, dk), F32)] * 2 + [jax.ShapeDtypeStruct((n, dk), BF16)] * 2,
        compiler_params=_params(1),
        name="mem_kv",
    )(mem, g, w_k, w_v)


ROW_TILE = 512
ATT_CHUNK = STEPS * DILATED_BRANCHES[-1][1]


def kernel(x_prompt, x_sample, mem_prompt, cache_win_k, cache_win_v, cache_conv, cache_mem_k, cache_mem_v,
           rpb_table, norm_mix_g, w_in, conv_w, conv_b, conv_ln_g, conv_ln_b, w_out, norm_x_g, norm_mem_g,
           w_xq, w_xk, w_xv, w_xo, norm_ffn_g, w_ffn_gate, w_ffn_up, w_ffn_down, norm_final_g):
    depth = w_in.shape[0]
    assert depth == 1, "single-layer stack"
    batch, seq, d = x_prompt.shape
    nb, t_new, _ = x_sample.shape
    buf_len = cache_win_k.shape[2]
    keep_p = min(MAX_DISTANCE, seq)
    n_mem = mem_prompt.shape[1]
    conv_hist = CONV_K - 1
    assert seq % ATT_CHUNK == 0 and keep_p % ATT_CHUNK == 0 and buf_len == MAX_DISTANCE

    row = lambda a: a.reshape(1, -1)
    bf = lambda a: a.astype(BF16)
    l = 0
    w_in_b, w_out_b = bf(w_in[l]), bf(w_out[l])
    w_xq_b, w_xk_b, w_xv_b, w_xo_b = bf(w_xq[l]), bf(w_xk[l]), bf(w_xv[l]), bf(w_xo[l])
    w_g_b, w_u_b, w_d_b = bf(w_ffn_gate[l]), bf(w_ffn_up[l]), bf(w_ffn_down[l])
    g_mix, g_x, g_mem, g_ffn, g_fin = (row(norm_mix_g[l]), row(norm_x_g[l]), row(norm_mem_g[l]),
                                       row(norm_ffn_g[l]), row(norm_final_g))
    cv_w, cv_b, ln_g, ln_b = conv_w[l], row(conv_b[l]), row(conv_ln_g[l]), row(conv_ln_b[l])

    bias_p = _bias_tables(rpb_table, _prompt_bucket_index(), masked_cols=STEPS)
    bias_s = _bias_tables(rpb_table, _sample_bucket_index(buf_len, t_new))[0]

    xp = x_prompt.reshape(batch * seq, d)
    wkv_t_b = bf(jnp.transpose(w_in[l][:, ATT_WIDTH:3 * ATT_WIDTH]))
    q, k, v, u, p_wk_t, p_wv_t = _in_proj_prompt(xp, g_mix, w_in_b, wkv_t_b, seq, keep_p, ROW_TILE)
    att = _dilated_attention_prompt(q, k, v, bias_p, seq // ATT_CHUNK)
    mk, mv, mk_b, mv_b = _mem_kv(mem_prompt.reshape(batch * n_mem, d), g_mem, w_xk_b, w_xv_b, n_mem)
    x2 = _prompt_post(xp, att, u, cv_w, cv_b, ln_g, ln_b, w_out_b, g_x, w_xq_b, mk_b, mv_b, w_xo_b,
                      seq, ROW_TILE)
    y_prompt = _ffn(x2, g_ffn, w_g_b, w_u_b, w_d_b, g_fin, ROW_TILE).reshape(batch, seq, d)

    tok_major = lambda a: jnp.transpose(a.reshape(a.shape[0], N_ATT_HEADS, HEAD_DIM, -1), (0, 3, 1, 2))[None]
    p_wk, p_wv = tok_major(p_wk_t), tok_major(p_wv_t)
    u3 = u.reshape(batch, seq, -1)
    p_conv = u3[:, seq - conv_hist:][None]
    xh = lambda a, n: a.reshape(1, n, n_mem, X_HEADS, -1)
    p_mk, p_mv = xh(mk, batch), xh(mv, batch)

    xs = x_sample.reshape(nb * t_new, d)
    qs, ks, vs, us = _in_proj(xs, g_mix, w_in_b, nb * t_new)
    chan_major = lambda a: jnp.transpose(a, (0, 2, 3, 1)).reshape(nb, ATT_WIDTH, -1)
    new_rows = lambda a: jnp.pad(jnp.transpose(a.reshape(nb, t_new, ATT_WIDTH), (0, 2, 1)),
                                 ((0, 0), (0, 0), (NEW_LANES - t_new, 0)))
    s_wk_t, s_wv_t, att_s = _dilated_attention_sample(
        qs, new_rows(ks), new_rows(vs), chan_major(cache_win_k[l]), chan_major(cache_win_v[l]), bias_s, t_new)
    u_full = jnp.concatenate([cache_conv[l], us.reshape(nb, t_new, -1)], axis=1)
    x1s, qxs = _sample_mix(xs, att_s, u_full, cv_w, cv_b, ln_g, ln_b, w_out_b, g_x, w_xq_b)
    xo_s = _sample_cross(qxs, cache_mem_k[l].reshape(nb, n_mem, -1), cache_mem_v[l].reshape(nb, n_mem, -1), t_new)
    y_sample = _sample_tail(x1s, xo_s, w_xo_b, g_ffn, w_g_b, w_u_b, w_d_b, g_fin).reshape(nb, t_new, d)

    return (y_prompt, y_sample, p_wk, p_wv, p_conv, p_mk, p_mv,
            tok_major(s_wk_t), tok_major(s_wv_t), u_full[:, t_new:][None])
```

```python
import functools
import math

import numpy as np
import jax
import jax.numpy as jnp
from jax import lax
from jax.experimental import pallas as pl
from jax.experimental.pallas import tpu as pltpu

F32 = jnp.float32
BF16 = jnp.bfloat16

HEAD_DIM = 64
N_ATT_HEADS = 12
ATT_WIDTH = N_ATT_HEADS * HEAD_DIM
PAIR_W = 2 * HEAD_DIM
N_PAIRS = N_ATT_HEADS // 2
CONV_K = 31
CONV_HALO = 32
F32_SUBLANES = 8
DILATED_BRANCHES = ((128, 1), (512, 4), (2048, 16))
STEPS = 128
N_CLS = DILATED_BRANCHES[-1][1]
NEW_LANES = 128
N_BUCKETS = 32
MAX_EXACT = N_BUCKETS // 2
MAX_DISTANCE = 2048
X_HEADS = 4
EPS = 1e-6
ATT_SCALE = HEAD_DIM ** -0.5
MASKED = -1e30

V7X_VMEM_BYTES = 64 * 1024 * 1024
VMEM_LIMIT = V7X_VMEM_BYTES * 3 // 4


def _params(n_grid_dims):
    return pltpu.CompilerParams(dimension_semantics=("arbitrary",) * n_grid_dims,
                                vmem_limit_bytes=VMEM_LIMIT)


def _rms(x, g):
    return x * lax.rsqrt(jnp.mean(x * x, axis=-1, keepdims=True) + EPS) * g


def _sigmoid(x):
    return 1.0 / (1.0 + jnp.exp(-x))


def _dot(a, b):
    return jnp.dot(a, b, preferred_element_type=F32)


def _dot_nt(a, b):
    return lax.dot_general(a, b, (((1,), (1,)), ((), ())), preferred_element_type=F32)


def _t5_bucket_np(dist):
    n = dist.astype(np.int32)
    nf = np.maximum(n, MAX_EXACT).astype(np.float32)
    large = MAX_EXACT + (np.log(nf / np.float32(MAX_EXACT)) / np.float32(math.log(MAX_DISTANCE / MAX_EXACT))
                         * np.float32(N_BUCKETS - MAX_EXACT)).astype(np.int32)
    large = np.minimum(large, N_BUCKETS - 1)
    return np.where(n < MAX_EXACT, n, large)


def _block_order(dil):
    groups = N_CLS // dil
    per = STEPS // groups
    i = np.arange(STEPS)
    return (i % per) * groups + i // per


def _prompt_bucket_index():
    out = []
    for _, dil in DILATED_BRANCHES:
        n = _block_order(dil)
        kj = np.concatenate([n, STEPS + n])[None, :]
        sub = STEPS + n[:, None] - kj
        band = (sub >= 0) & (sub <= STEPS)
        out.append(np.where(band, _t5_bucket_np(dil * np.maximum(sub, 0)), -1))
    return np.stack(out).astype(np.int32)


def _sample_bucket_index(buf_len, t_new):
    col = np.arange(buf_len + NEW_LANES)
    key = np.where(col < buf_len, (col + t_new) % buf_len, col - NEW_LANES + t_new)
    keep = (col < buf_len) | (col >= buf_len + NEW_LANES - t_new)
    i = np.arange(t_new)[:, None]
    dist = buf_len + i - key[None, :]
    out = []
    for window, dil in DILATED_BRANCHES:
        ok = keep[None, :] & (dist >= 0) & (dist % dil == 0) & (dist // dil <= window // dil)
        out.append(np.where(ok, _t5_bucket_np(np.maximum(dist, 0)), -1))
    return np.stack(out).astype(np.int32)


def _bias_kernel(tab_ref, idx_ref, o_ref, *, masked_cols):
    pair = pl.program_id(1)
    idx = idx_ref[0]
    rows = idx.shape[0]
    for half in range(2):
        head = 2 * pair + half
        acc = jnp.full(idx.shape, MASKED, F32)
        for b in range(N_BUCKETS):
            acc = jnp.where(idx == b, tab_ref[b, head], acc)
        o_ref[0, 0, 0, half * rows:(half + 1) * rows, :] = acc
        if masked_cols:
            col = lax.broadcasted_iota(jnp.int32, idx.shape, 1)
            o_ref[1, 0, 0, half * rows:(half + 1) * rows, :] = jnp.where(col < masked_cols, MASKED, acc)


def _bias_tables(table, idx, masked_cols=0):
    n_var, rows, cols = idx.shape
    n_out = 2 if masked_cols else 1
    return pl.pallas_call(
        functools.partial(_bias_kernel, masked_cols=masked_cols),
        grid=(n_var, N_PAIRS),
        in_specs=[pl.BlockSpec(memory_space=pltpu.SMEM),
                  pl.BlockSpec((1, rows, cols), lambda v, p: (v, 0, 0))],
        out_specs=pl.BlockSpec((n_out, 1, 1, 2 * rows, cols), lambda v, p: (0, v, p, 0, 0)),
        out_shape=jax.ShapeDtypeStruct((n_out, n_var, N_PAIRS, 2 * rows, cols), F32),
        compiler_params=_params(2),
        name="bias_tables",
    )(table, jnp.asarray(idx))


def _in_proj_sample_kernel(x_ref, g_ref, w_ref, wkv_t_ref, q_ref, u_ref, kt_ref, vt_ref, *, t_new):
    h = _rms(x_ref[...], g_ref[...]).astype(BF16)
    aw = q_ref.shape[1]
    cw = u_ref.shape[1]
    q_ref[...] = _dot(h, w_ref[:, 0:aw]) * ATT_SCALE
    a = _dot(h, w_ref[:, 3 * aw:3 * aw + cw])
    gate = _dot(h, w_ref[:, 3 * aw + cw:3 * aw + 2 * cw])
    u_ref[...] = a * _sigmoid(gate)

    is_new = lax.broadcasted_iota(jnp.int32, (aw, NEW_LANES), 1) >= NEW_LANES - t_new
    per_group = NEW_LANES // t_new
    for w_rows, o_ref in ((slice(0, aw), kt_ref), (slice(aw, 2 * aw), vt_ref)):
        kv_t = _dot_nt(wkv_t_ref[w_rows, :], h)
        for n in range(x_ref.shape[0] // t_new):
            grp = kv_t[:, (n // per_group) * NEW_LANES:(n // per_group + 1) * NEW_LANES]
            shift = (NEW_LANES - t_new - (n % per_group) * t_new) % NEW_LANES
            if shift:
                grp = pltpu.roll(grp, shift, axis=1)
            o_ref[:, n * NEW_LANES:(n + 1) * NEW_LANES] = jnp.where(is_new, grp, 0.0)


def _in_proj_sample(x, g, w_bf16, wkv_t_bf16, t_new):
    n, d = x.shape
    aw = ATT_WIDTH
    cw = (w_bf16.shape[1] - 3 * aw) // 2
    new_shape = jax.ShapeDtypeStruct((aw, n // t_new * NEW_LANES), F32)
    return pl.pallas_call(
        functools.partial(_in_proj_sample_kernel, t_new=t_new),
        out_shape=[jax.ShapeDtypeStruct((n, aw), F32), jax.ShapeDtypeStruct((n, cw), F32), new_shape, new_shape],
        compiler_params=pltpu.CompilerParams(vmem_limit_bytes=VMEM_LIMIT),
        name="in_proj_sample",
    )(x, g, w_bf16, wkv_t_bf16)


def _in_proj_prompt_kernel(x_ref, g_ref, w_ref, wkv_t_ref, q_ref, k_ref, v_ref, u_ref, kt_ref, vt_ref, xs, *,
                           tiles_per_seq, tail_first):
    tm, d = x_ref.shape
    per = tm // N_CLS
    aw = q_ref.shape[3]
    cw = u_ref.shape[1]
    g = g_ref[...]
    x = x_ref[...]
    hn = _rms(x, g).astype(BF16)
    gate = _dot(hn, w_ref[:, 3 * aw + cw:3 * aw + 2 * cw])
    u_ref[...] = _dot(hn, w_ref[:, 3 * aw:3 * aw + cw]) * _sigmoid(gate)

    for c in range(d // PAIR_W):
        xs[c] = x[:, c * PAIR_W:(c + 1) * PAIR_W]
    x_cls = jnp.concatenate(
        [jnp.concatenate([xs[c, pl.ds(r, per, stride=N_CLS), :] for r in range(N_CLS)], axis=0)
         for c in range(d // PAIR_W)], axis=1)
    h = _rms(x_cls, g).astype(BF16)
    q_ref[0] = (_dot(h, w_ref[:, 0:aw]) * ATT_SCALE).reshape(N_CLS, per, aw)
    k_ref[0] = _dot(h, w_ref[:, aw:2 * aw]).reshape(N_CLS, per, aw)
    v_ref[0] = _dot(h, w_ref[:, 2 * aw:3 * aw]).reshape(N_CLS, per, aw)

    @pl.when(pl.program_id(0) % tiles_per_seq >= tail_first)
    def _():
        kt_ref[0] = _dot_nt(wkv_t_ref[:aw, :], hn)
        vt_ref[0] = _dot_nt(wkv_t_ref[aw:, :], hn)


def _in_proj_prompt(x, g, w_bf16, wkv_t_bf16, seq, keep, tm):
    n, d = x.shape
    aw = ATT_WIDTH
    cw = (w_bf16.shape[1] - 3 * aw) // 2
    chunk = N_CLS * STEPS
    tiles_per_chunk = chunk // tm
    tiles_per_seq = seq // tm
    tail_first = (seq - keep) // tm
    row = lambda i: (i, 0)
    fixed = lambda i: (0, 0)
    cls = pl.BlockSpec((1, N_CLS, tm // N_CLS, aw), lambda i: (i // tiles_per_chunk, 0, i % tiles_per_chunk, 0))
    tail = pl.BlockSpec((1, aw, tm),
                        lambda i: (i // tiles_per_seq, 0, jnp.maximum(i % tiles_per_seq - tail_first, 0)))
    cls_shape = jax.ShapeDtypeStruct((n // chunk, N_CLS, STEPS, aw), F32)
    tail_shape = jax.ShapeDtypeStruct((n // seq, aw, keep), F32)
    return pl.pallas_call(
        functools.partial(_in_proj_prompt_kernel, tiles_per_seq=tiles_per_seq, tail_first=tail_first),
        grid=(n // tm,),
        in_specs=[pl.BlockSpec((tm, d), row), pl.BlockSpec((1, d), fixed),
                  pl.BlockSpec(w_bf16.shape, fixed), pl.BlockSpec(wkv_t_bf16.shape, fixed)],
        out_specs=[cls, cls, cls, pl.BlockSpec((tm, cw), row), tail, tail],
        out_shape=[cls_shape, cls_shape, cls_shape, jax.ShapeDtypeStruct((n, cw), F32), tail_shape, tail_shape],
        scratch_shapes=[pltpu.VMEM((d // PAIR_W, tm, PAIR_W), F32)],
        compiler_params=_params(1),
        name="in_proj_prompt",
    )(x, g, w_bf16, wkv_t_bf16)


def _stack_heads(x, first):
    zero = jnp.zeros_like(x)
    return jnp.concatenate([jnp.where(first, x, zero), jnp.where(first, zero, x)], axis=0)


def _att_kernel(q_ref, kc_ref, kp_ref, vc_ref, vp_ref, bias_ref, bias0_ref, o_ref, num_s, m_s, l_s):
    first = lax.broadcasted_iota(jnp.int32, (STEPS, PAIR_W), 1) < HEAD_DIM
    ones = jnp.ones((2 * STEPS, PAIR_W), BF16)

    def pieces(dil, cls, blk):
        groups = N_CLS // dil
        per = STEPS // groups
        return [(cls + dil * j, pl.ds(blk * per, per)) for j in range(groups)], per

    def load(ref, dil, cls, blk):
        idx, _ = pieces(dil, cls, blk)
        return jnp.concatenate([ref[0, r, rows, :] for r, rows in idx], axis=0)

    def store(ref, g, dil, cls, blk, val):
        idx, per = pieces(dil, cls, blk)
        for j, (r, rows) in enumerate(idx):
            ref[g, r, rows, :] = val[j * per:(j + 1) * per]

    def attend(g, dil, cls, blk, k_prev, v_prev, bias):
        q_st = _stack_heads(load(q_ref, dil, cls, blk), first).astype(BF16)
        kcat = jnp.concatenate([k_prev, load(kc_ref, dil, cls, blk)], axis=0).astype(BF16)
        vcat = jnp.concatenate([v_prev, load(vc_ref, dil, cls, blk)], axis=0).astype(BF16)
        s = _dot_nt(q_st, kcat) + bias
        m = jnp.max(s, axis=-1, keepdims=True)
        p = jnp.exp(s - m).astype(BF16)
        o = _dot(p, jnp.concatenate([vcat, ones], axis=1))
        store(num_s, g, dil, cls, blk, jnp.where(first, o[:STEPS, :PAIR_W], o[STEPS:, :PAIR_W]))
        store(l_s, g, dil, cls, blk, jnp.where(first, o[:STEPS, PAIR_W:], o[STEPS:, PAIR_W:]))
        store(m_s, g, dil, cls, blk, jnp.where(first, jnp.broadcast_to(m[:STEPS], (STEPS, PAIR_W)),
                                               jnp.broadcast_to(m[STEPS:], (STEPS, PAIR_W))))

    for g, (_, dil) in enumerate(DILATED_BRANCHES):
        n_blk = N_CLS // dil
        for cls in range(dil):
            attend(g, dil, cls, 0, load(kp_ref, dil, cls, n_blk - 1), load(vp_ref, dil, cls, n_blk - 1),
                   bias0_ref[0, g, 0])

            for blk in range(1, n_blk):
                attend(g, dil, cls, blk, load(kc_ref, dil, cls, blk - 1), load(vc_ref, dil, cls, blk - 1),
                       bias_ref[g, 0])

    for r in range(N_CLS):
        m_all = jnp.maximum(jnp.maximum(m_s[0, r], m_s[1, r]), m_s[2, r])
        num = jnp.zeros(m_all.shape, F32)
        den = jnp.zeros(m_all.shape, F32)
        for g in range(len(DILATED_BRANCHES)):
            w = jnp.exp(m_s[g, r] - m_all)
            num = num + w * num_s[g, r]
            den = den + w * l_s[g, r]
        o_ref[pl.ds(r, STEPS, stride=N_CLS), :] = num / den


def _dilated_attention_prompt(q, k, v, bias, chunks_per_seq):
    n_chunks = q.shape[0]
    n_br = len(DILATED_BRANCHES)
    cur = lambda b, c, p: (b * chunks_per_seq + c, 0, 0, p)
    prev = lambda b, c, p: (b * chunks_per_seq + jnp.maximum(c - 1, 0), 0, 0, p)
    blk = pl.BlockSpec((1, N_CLS, STEPS, PAIR_W), cur)
    blk_prev = pl.BlockSpec((1, N_CLS, STEPS, PAIR_W), prev)
    return pl.pallas_call(
        _att_kernel,
        grid=(n_chunks // chunks_per_seq, chunks_per_seq, N_PAIRS),
        in_specs=[blk, blk, blk_prev, blk, blk_prev,
                  pl.BlockSpec((n_br, 1, 2 * STEPS, 2 * STEPS), lambda b, c, p: (0, p, 0, 0)),
                  pl.BlockSpec((1, n_br, 1, 2 * STEPS, 2 * STEPS),
                               lambda b, c, p: (jnp.where(c == 0, 1, 0), 0, p, 0, 0))],
        out_specs=pl.BlockSpec((N_CLS * STEPS, PAIR_W), lambda b, c, p: (b * chunks_per_seq + c, p)),
        out_shape=jax.ShapeDtypeStruct((n_chunks * N_CLS * STEPS, ATT_WIDTH), F32),
        scratch_shapes=[pltpu.VMEM((n_br, N_CLS, STEPS, PAIR_W), F32)] * 3,
        compiler_params=_params(3),
        name="dilated_attention_prompt",
    )(q, k, k, v, v, bias[0], bias)


def _sample_att_kernel(q_ref, kn_ref, vn_ref, ck_ref, cv_ref, bias_ref, ok_ref, ov_ref, o_ref):
    buf_len = ck_ref.shape[2]
    t_new = q_ref.shape[0]
    tail = buf_len - NEW_LANES
    is_new = lax.broadcasted_iota(jnp.int32, (PAIR_W, NEW_LANES), 1) >= NEW_LANES - t_new
    first = lax.broadcasted_iota(jnp.int32, (t_new, PAIR_W), 1) < HEAD_DIM

    for j in range(ck_ref.shape[1] // PAIR_W):
        ch = slice(j * PAIR_W, (j + 1) * PAIR_W)

        def shift_in(c_ref, n_ref, o_ref):
            rot = pltpu.roll(c_ref[0, ch, :], buf_len - t_new, axis=1)
            new = n_ref[ch, :]
            o_ref[0, ch, :tail] = rot[:, :tail]
            o_ref[0, ch, tail:] = jnp.where(is_new, new, rot[:, tail:])
            return rot.astype(BF16), new.astype(BF16)

        rot_k, new_k = shift_in(ck_ref, kn_ref, ok_ref)
        rot_v, new_v = shift_in(cv_ref, vn_ref, ov_ref)

        q_st = _stack_heads(q_ref[:, ch], first).astype(BF16)
        s = jnp.concatenate([_dot(q_st, rot_k), _dot(q_st, new_k)], axis=1)
        sg = [s + bias_ref[g, j] for g in range(len(DILATED_BRANCHES))]
        m = functools.reduce(jnp.maximum, [jnp.max(x, axis=-1, keepdims=True) for x in sg])
        p = functools.reduce(jnp.add, [jnp.exp(x - m) for x in sg])
        den = jnp.sum(p, axis=-1, keepdims=True)
        pb = p.astype(BF16)
        o = (_dot_nt(pb[:, :buf_len], rot_v) + _dot_nt(pb[:, buf_len:], new_v)) / den
        o_ref[:, ch] = jnp.where(first, o[:t_new], o[t_new:]).astype(o_ref.dtype)


SAMPLE_PAIRS_PER_STEP = 3


def _dilated_attention_sample(q, k_new_t, v_new_t, cache_k_t, cache_v_t, bias, t_new):
    nb, _, buf_len = cache_k_t.shape
    n_br = len(DILATED_BRANCHES)
    pps = SAMPLE_PAIRS_PER_STEP
    width = pps * PAIR_W
    tok = pl.BlockSpec((t_new, width), lambda n, p: (n, p))
    new = pl.BlockSpec((width, NEW_LANES), lambda n, p: (p, n))
    buf = pl.BlockSpec((1, width, buf_len), lambda n, p: (n, p, 0))
    return pl.pallas_call(
        _sample_att_kernel,
        grid=(nb, N_PAIRS // pps),
        in_specs=[tok, new, new, buf, buf,
                  pl.BlockSpec((n_br, pps, 2 * t_new, buf_len + NEW_LANES), lambda n, p: (0, p, 0, 0))],
        out_specs=[buf, buf, tok],
        out_shape=[jax.ShapeDtypeStruct(cache_k_t.shape, F32), jax.ShapeDtypeStruct(cache_v_t.shape, F32),
                   jax.ShapeDtypeStruct((nb * t_new, ATT_WIDTH), BF16)],
        compiler_params=_params(2),
        name="dilated_attention_sample",
    )(q, k_new_t, v_new_t, cache_k_t, cache_v_t, bias)


def _conv_tail(y, cb_ref, lg_ref, lb_ref):
    y = y + cb_ref[...]
    mu = jnp.mean(y, axis=-1, keepdims=True)
    yc = y - mu
    var = jnp.mean(yc * yc, axis=-1, keepdims=True)
    yn = yc * lax.rsqrt(var + EPS) * lg_ref[...] + lb_ref[...]
    return yn * _sigmoid(yn)


def _mix_out(x, att, c, wo_ref):
    aw = att.shape[1]
    return x + _dot(att.astype(BF16), wo_ref[:aw, :]) + _dot(c.astype(BF16), wo_ref[aw:, :])


def _cross_attend(qx, head_kv):
    hd = qx.shape[1] // X_HEADS
    outs = []
    for h in range(X_HEADS):
        mk, mv = head_kv(h)
        s = _dot_nt(qx[:, h * hd:(h + 1) * hd].astype(BF16), mk)
        p = jnp.exp(s - jnp.max(s, axis=-1, keepdims=True))
        den = jnp.sum(p, axis=-1, keepdims=True)
        outs.append((_dot(p.astype(BF16), mv) / den).astype(BF16))
    return jnp.concatenate(outs, axis=1)


def _prompt_post_kernel(x_ref, att_ref, u_ref, uh_ref, cw_ref, cb_ref, lg_ref, lb_ref, wo_ref,
                        gx_ref, wq_ref, mk_ref, mv_ref, wxo_ref, o_ref, ubuf, *, tiles_per_seq):
    tm = x_ref.shape[0]
    seq_start = (pl.program_id(0) % tiles_per_seq) == 0
    ubuf[:CONV_HALO, :] = jnp.where(seq_start, 0.0, uh_ref[...])
    ubuf[CONV_HALO:, :] = u_ref[...]
    hist = ubuf[...]
    n_rows = hist.shape[0]
    first_tap = CONV_HALO - (CONV_K - 1)
    y = jnp.zeros(u_ref.shape, F32)
    for s in range(F32_SUBLANES):
        shifted = hist if s == 0 else pltpu.roll(hist, n_rows - s, axis=0)
        for j in range(CONV_K):
            if (first_tap + j) % F32_SUBLANES == s:
                lo = first_tap + j - s
                y = y + cw_ref[j:j + 1, :] * shifted[lo:lo + tm]
    c = _conv_tail(y, cb_ref, lg_ref, lb_ref)
    x1 = _mix_out(x_ref[...], att_ref[...], c, wo_ref)
    hd = wq_ref.shape[1] // X_HEADS
    qx = _dot(_rms(x1, gx_ref[...]).astype(BF16), wq_ref[...]) * (hd ** -0.5)
    head_kv = lambda h: (mk_ref[:, h * hd:(h + 1) * hd], mv_ref[:, h * hd:(h + 1) * hd])
    o_ref[...] = x1 + _dot(_cross_attend(qx, head_kv), wxo_ref[...])


def _prompt_post(x, att, u, conv_w, conv_b, ln_g, ln_b, w_out, gx, w_xq, mk, mv, w_xo, seq, tm):
    n, d = x.shape
    cw = u.shape[1]
    n_mem = mk.shape[0] // (n // seq)
    tiles_per_seq = seq // tm
    row = lambda i: (i, 0)
    fixed = lambda i: (0, 0)
    halo = lambda i: (jnp.maximum(i * (tm // CONV_HALO) - 1, 0), 0)
    per_seq = lambda i: (i // tiles_per_seq, 0)
    full = lambda a: pl.BlockSpec(a.shape, fixed)
    return pl.pallas_call(
        functools.partial(_prompt_post_kernel, tiles_per_seq=tiles_per_seq),
        grid=(n // tm,),
        in_specs=[pl.BlockSpec((tm, d), row), pl.BlockSpec((tm, ATT_WIDTH), row),
                  pl.BlockSpec((tm, cw), row), pl.BlockSpec((CONV_HALO, cw), halo),
                  full(conv_w), full(conv_b), full(ln_g), full(ln_b), full(w_out),
                  full(gx), full(w_xq),
                  pl.BlockSpec((n_mem, d), per_seq), pl.BlockSpec((n_mem, d), per_seq), full(w_xo)],
        out_specs=pl.BlockSpec((tm, d), row),
        out_shape=jax.ShapeDtypeStruct((n, d), F32),
        scratch_shapes=[pltpu.VMEM((CONV_HALO + tm, cw), F32)],
        compiler_params=_params(1),
        name="prompt_mix_cross",
    )(x, att, u, u, conv_w, conv_b, ln_g, ln_b, w_out, gx, w_xq, mk, mv, w_xo)


def _sample_mix_kernel(x_ref, att_ref, uf_ref, cw_ref, cb_ref, lg_ref, lb_ref, wo_ref, gx_ref, wq_ref,
                       x1_ref, qx_ref):
    nb, t_full, cw = uf_ref.shape
    t_new = t_full - (CONV_K - 1)
    y = jnp.zeros((nb, t_new, cw), F32)
    for j in range(CONV_K):
        y = y + cw_ref[j:j + 1, :] * uf_ref[:, j:j + t_new, :]
    c = _conv_tail(y.reshape(nb * t_new, cw), cb_ref, lg_ref, lb_ref)
    x1 = _mix_out(x_ref[...], att_ref[...], c, wo_ref)
    x1_ref[...] = x1
    hd = wq_ref.shape[1] // X_HEADS
    qx_ref[...] = _dot(_rms(x1, gx_ref[...]).astype(BF16), wq_ref[...]) * (hd ** -0.5)


def _sample_mix(x, att, u_full, conv_w, conv_b, ln_g, ln_b, w_out, gx, w_xq):
    n, d = x.shape
    return pl.pallas_call(
        _sample_mix_kernel,
        out_shape=[jax.ShapeDtypeStruct((n, d), F32), jax.ShapeDtypeStruct((n, w_xq.shape[1]), F32)],
        compiler_params=pltpu.CompilerParams(vmem_limit_bytes=VMEM_LIMIT),
        name="sample_mix",
    )(x, att, u_full, conv_w, conv_b, ln_g, ln_b, w_out, gx, w_xq)


def _sample_cross_kernel(qx_ref, mk_hbm, mv_hbm, o_ref, kbuf, vbuf, sems):
    n = pl.program_id(0)

    def head_copies(seq, slot):
        return [pltpu.make_async_copy(src.at[seq, :, h, :], dst.at[slot, h], sems.at[i, slot, h])
                for i, (src, dst) in enumerate(((mk_hbm, kbuf), (mv_hbm, vbuf))) for h in range(X_HEADS)]

    @pl.when(n == 0)
    def _():
        for cp in head_copies(0, 0):
            cp.start()

    @pl.when(n + 1 < pl.num_programs(0))
    def _():
        for cp in head_copies(n + 1, (n + 1) % 2):
            cp.start()

    slot = n % 2
    for cp in head_copies(n, slot):
        cp.wait()
    head_kv = lambda h: (kbuf[slot, h].astype(BF16), vbuf[slot, h].astype(BF16))
    o_ref[...] = _cross_attend(qx_ref[...], head_kv)


def _sample_cross(qx, mem_k, mem_v, t_new):
    nb, n_mem, n_heads, hd = mem_k.shape
    tok = pl.BlockSpec((t_new, n_heads * hd), lambda n: (n, 0))
    hbm = pl.BlockSpec(memory_space=pl.ANY)
    return pl.pallas_call(
        _sample_cross_kernel,
        grid=(nb,),
        in_specs=[tok, hbm, hbm],
        out_specs=tok,
        out_shape=jax.ShapeDtypeStruct(qx.shape, BF16),
        scratch_shapes=[pltpu.VMEM((2, n_heads, n_mem, hd), F32), pltpu.VMEM((2, n_heads, n_mem, hd), F32),
                        pltpu.SemaphoreType.DMA((2, 2, n_heads))],
        compiler_params=_params(1),
        name="sample_cross",
    )(qx, mem_k, mem_v)


def _swiglu_final(x2, gf_ref, wg_ref, wu_ref, wd_ref, gfin_ref, ff_chunk):
    h = _rms(x2, gf_ref[...]).astype(BF16)
    acc = x2
    for lo in range(0, wg_ref.shape[1], ff_chunk):
        gate = _dot(h, wg_ref[:, lo:lo + ff_chunk])
        up = _dot(h, wu_ref[:, lo:lo + ff_chunk])
        acc = acc + _dot((gate * _sigmoid(gate) * up).astype(BF16), wd_ref[lo:lo + ff_chunk, :])
    return _rms(acc, gfin_ref[...])


def _ffn_kernel(x_ref, gf_ref, wg_ref, wu_ref, wd_ref, gfin_ref, o_ref, *, ff_chunk):
    o_ref[...] = _swiglu_final(x_ref[...], gf_ref, wg_ref, wu_ref, wd_ref, gfin_ref, ff_chunk)


def _sample_tail_kernel(x_ref, xo_ref, wxo_ref, gf_ref, wg_ref, wu_ref, wd_ref, gfin_ref, o_ref, *, ff_chunk):
    x2 = x_ref[...] + _dot(xo_ref[...], wxo_ref[...])
    o_ref[...] = _swiglu_final(x2, gf_ref, wg_ref, wu_ref, wd_ref, gfin_ref, ff_chunk)


FF_CHUNK = 256


def _ffn(x, gf, wg, wu, wd, gfin, tm):
    n, d = x.shape
    row = lambda i: (i, 0)
    full = lambda a: pl.BlockSpec(a.shape, lambda i: (0, 0))
    return pl.pallas_call(
        functools.partial(_ffn_kernel, ff_chunk=FF_CHUNK),
        grid=(n // tm,),
        in_specs=[pl.BlockSpec((tm, d), row), full(gf), full(wg), full(wu), full(wd), full(gfin)],
        out_specs=pl.BlockSpec((tm, d), row),
        out_shape=jax.ShapeDtypeStruct((n, d), F32),
        compiler_params=_params(1),
        name="swiglu_final_norm",
    )(x, gf, wg, wu, wd, gfin)


def _sample_tail(x1, xo, w_xo, gf, wg, wu, wd, gfin):
    return pl.pallas_call(
        functools.partial(_sample_tail_kernel, ff_chunk=FF_CHUNK),
        out_shape=jax.ShapeDtypeStruct(x1.shape, F32),
        compiler_params=pltpu.CompilerParams(vmem_limit_bytes=VMEM_LIMIT),
        name="sample_cross_out_swiglu",
    )(x1, xo, w_xo, gf, wg, wu, wd, gfin)


def _mem_kv_kernel(m_ref, g_ref, wk_ref, wv_ref, k_ref, v_ref, kb_ref, vb_ref):
    h = _rms(m_ref[...], g_ref[...]).astype(BF16)
    k = _dot(h, wk_ref[...])
    v = _dot(h, wv_ref[...])
    k_ref[...] = k
    v_ref[...] = v
    kb_ref[...] = k.astype(BF16)
    vb_ref[...] = v.astype(BF16)


def _mem_kv(mem, g, w_k, w_v, tm):
    n, d = mem.shape
    row = lambda i: (i, 0)
    full = lambda a: pl.BlockSpec(a.shape, lambda i: (0, 0))
    dk = w_k.shape[1]
    return pl.pallas_call(
        _mem_kv_kernel,
        grid=(n // tm,),
        in_specs=[pl.BlockSpec((tm, d), row), full(g), full(w_k), full(w_v)],
        out_specs=[pl.BlockSpec((tm, dk), row)] * 4,
        out_shape=[jax.ShapeDtypeStruct((n, dk), F32)] * 2 + [jax.ShapeDtypeStruct((n, dk), BF16)] * 2,
        compiler_params=_params(1),
        name="mem_kv",
    )(mem, g, w_k, w_v)


ROW_TILE = 512
ATT_CHUNK = STEPS * DILATED_BRANCHES[-1][1]


def kernel(x_prompt, x_sample, mem_prompt, cache_win_k, cache_win_v, cache_conv, cache_mem_k, cache_mem_v,
           rpb_table, norm_mix_g, w_in, conv_w, conv_b, conv_ln_g, conv_ln_b, w_out, norm_x_g, norm_mem_g,
           w_xq, w_xk, w_xv, w_xo, norm_ffn_g, w_ffn_gate, w_ffn_up, w_ffn_down, norm_final_g):
    depth = w_in.shape[0]
    assert depth == 1, "single-layer stack"
    batch, seq, d = x_prompt.shape
    nb, t_new, _ = x_sample.shape
    buf_len = cache_win_k.shape[2]
    keep_p = min(MAX_DISTANCE, seq)
    n_mem = mem_prompt.shape[1]
    conv_hist = CONV_K - 1
    assert seq % ATT_CHUNK == 0 and keep_p % ATT_CHUNK == 0 and buf_len == MAX_DISTANCE

    row = lambda a: a.reshape(1, -1)
    bf = lambda a: a.astype(BF16)
    l = 0
    w_in_b, w_out_b = bf(w_in[l]), bf(w_out[l])
    w_xq_b, w_xk_b, w_xv_b, w_xo_b = bf(w_xq[l]), bf(w_xk[l]), bf(w_xv[l]), bf(w_xo[l])
    w_g_b, w_u_b, w_d_b = bf(w_ffn_gate[l]), bf(w_ffn_up[l]), bf(w_ffn_down[l])
    g_mix, g_x, g_mem, g_ffn, g_fin = (row(norm_mix_g[l]), row(norm_x_g[l]), row(norm_mem_g[l]),
                                       row(norm_ffn_g[l]), row(norm_final_g))
    cv_w, cv_b, ln_g, ln_b = conv_w[l], row(conv_b[l]), row(conv_ln_g[l]), row(conv_ln_b[l])

    bias_p = _bias_tables(rpb_table, _prompt_bucket_index(), masked_cols=STEPS)
    bias_s = _bias_tables(rpb_table, _sample_bucket_index(buf_len, t_new))[0]

    xp = x_prompt.reshape(batch * seq, d)
    wkv_t_b = bf(jnp.transpose(w_in[l][:, ATT_WIDTH:3 * ATT_WIDTH]))
    q, k, v, u, p_wk_t, p_wv_t = _in_proj_prompt(xp, g_mix, w_in_b, wkv_t_b, seq, keep_p, ROW_TILE)
    att = _dilated_attention_prompt(q, k, v, bias_p, seq // ATT_CHUNK)
    mk, mv, mk_b, mv_b = _mem_kv(mem_prompt.reshape(batch * n_mem, d), g_mem, w_xk_b, w_xv_b, n_mem)
    x2 = _prompt_post(xp, att, u, cv_w, cv_b, ln_g, ln_b, w_out_b, g_x, w_xq_b, mk_b, mv_b, w_xo_b,
                      seq, ROW_TILE)
    y_prompt = _ffn(x2, g_ffn, w_g_b, w_u_b, w_d_b, g_fin, ROW_TILE).reshape(batch, seq, d)

    tok_major = lambda a: jnp.transpose(a.reshape(a.shape[0], N_ATT_HEADS, HEAD_DIM, -1), (0, 3, 1, 2))[None]
    p_wk, p_wv = tok_major(p_wk_t), tok_major(p_wv_t)
    u3 = u.reshape(batch, seq, -1)
    p_conv = u3[:, seq - conv_hist:][None]
    xh = lambda a, n: a.reshape(1, n, n_mem, X_HEADS, -1)
    p_mk, p_mv = xh(mk, batch), xh(mv, batch)

    xs = x_sample.reshape(nb * t_new, d)
    qs, us, ks_t, vs_t = _in_proj_sample(xs, g_mix, w_in_b, wkv_t_b, t_new)
    chan_major = lambda a: jnp.transpose(a, (0, 2, 3, 1)).reshape(nb, ATT_WIDTH, -1)
    s_wk_t, s_wv_t, att_s = _dilated_attention_sample(
        qs, ks_t, vs_t, chan_major(cache_win_k[l]), chan_major(cache_win_v[l]), bias_s, t_new)
    u_full = jnp.concatenate([cache_conv[l], us.reshape(nb, t_new, -1)], axis=1)
    x1s, qxs = _sample_mix(xs, att_s, u_full, cv_w, cv_b, ln_g, ln_b, w_out_b, g_x, w_xq_b)
    xo_s = _sample_cross(qxs, cache_mem_k[l], cache_mem_v[l], t_new)
    y_sample = _sample_tail(x1s, xo_s, w_xo_b, g_ffn, w_g_b, w_u_b, w_d_b, g_fin).reshape(nb, t_new, d)

    return (y_prompt, y_sample, p_wk, p_wv, p_conv, p_mk, p_mv,
            tok_major(s_wk_t), tok_major(s_wv_t), u_full[:, t_new:][None])
```

```python
import functools
import math

import numpy as np
import jax
import jax.numpy as jnp
from jax import lax
from jax.experimental import pallas as pl
from jax.experimental.pallas import tpu as pltpu

F32 = jnp.float32
BF16 = jnp.bfloat16

HEAD_DIM = 64
N_ATT_HEADS = 12
ATT_WIDTH = N_ATT_HEADS * HEAD_DIM
PAIR_W = 2 * HEAD_DIM
N_PAIRS = N_ATT_HEADS // 2
CONV_K = 31
CONV_HALO = 32
F32_SUBLANES = 8
DILATED_BRANCHES = ((128, 1), (512, 4), (2048, 16))
STEPS = 128
N_CLS = DILATED_BRANCHES[-1][1]
NEW_LANES = 128
N_BUCKETS = 32
MAX_EXACT = N_BUCKETS // 2
MAX_DISTANCE = 2048
X_HEADS = 4
EPS = 1e-6
LOG2E = math.log2(math.e)
ATT_SCALE = HEAD_DIM ** -0.5 * LOG2E
MASKED = -1e30

V7X_VMEM_BYTES = 64 * 1024 * 1024
VMEM_LIMIT = V7X_VMEM_BYTES * 3 // 4


def _params(n_grid_dims):
    return pltpu.CompilerParams(dimension_semantics=("arbitrary",) * n_grid_dims,
                                vmem_limit_bytes=VMEM_LIMIT)


def _rms(x, g):
    return x * lax.rsqrt(jnp.mean(x * x, axis=-1, keepdims=True) + EPS) * g


def _sigmoid(x):
    return 1.0 / (1.0 + jnp.exp(-x))


def _dot(a, b):
    return jnp.dot(a, b, preferred_element_type=F32)


def _dot_nt(a, b):
    return lax.dot_general(a, b, (((1,), (1,)), ((), ())), preferred_element_type=F32)


def _t5_bucket_np(dist):
    n = dist.astype(np.int32)
    nf = np.maximum(n, MAX_EXACT).astype(np.float32)
    large = MAX_EXACT + (np.log(nf / np.float32(MAX_EXACT)) / np.float32(math.log(MAX_DISTANCE / MAX_EXACT))
                         * np.float32(N_BUCKETS - MAX_EXACT)).astype(np.int32)
    large = np.minimum(large, N_BUCKETS - 1)
    return np.where(n < MAX_EXACT, n, large)


def _block_order(dil):
    groups = N_CLS // dil
    per = STEPS // groups
    i = np.arange(STEPS)
    return (i % per) * groups + i // per


def _prompt_bucket_index():
    out = []
    for _, dil in DILATED_BRANCHES:
        n = _block_order(dil)
        kj = np.concatenate([n, STEPS + n])[None, :]
        sub = STEPS + n[:, None] - kj
        band = (sub >= 0) & (sub <= STEPS)
        out.append(np.where(band, _t5_bucket_np(dil * np.maximum(sub, 0)), -1))
    return np.stack(out).astype(np.int32)


def _sample_bucket_index(buf_len, t_new):
    col = np.arange(buf_len + NEW_LANES)
    key = np.where(col < buf_len, (col + t_new) % buf_len, col - NEW_LANES + t_new)
    keep = (col < buf_len) | (col >= buf_len + NEW_LANES - t_new)
    i = np.arange(t_new)[:, None]
    dist = buf_len + i - key[None, :]
    out = []
    for window, dil in DILATED_BRANCHES:
        ok = keep[None, :] & (dist >= 0) & (dist % dil == 0) & (dist // dil <= window // dil)
        out.append(np.where(ok, _t5_bucket_np(np.maximum(dist, 0)), -1))
    return np.stack(out).astype(np.int32)


def _bias_kernel(tab_ref, idx_ref, o_ref, *, masked_cols):
    pair = pl.program_id(1)
    idx = idx_ref[0]
    rows = idx.shape[0]
    for half in range(2):
        head = 2 * pair + half
        acc = jnp.full(idx.shape, MASKED, F32)
        for b in range(N_BUCKETS):
            acc = jnp.where(idx == b, tab_ref[b, head] * LOG2E, acc)
        o_ref[0, 0, 0, half * rows:(half + 1) * rows, :] = acc
        if masked_cols:
            col = lax.broadcasted_iota(jnp.int32, idx.shape, 1)
            o_ref[1, 0, 0, half * rows:(half + 1) * rows, :] = jnp.where(col < masked_cols, MASKED, acc)


def _bias_tables(table, idx, masked_cols=0):
    n_var, rows, cols = idx.shape
    n_out = 2 if masked_cols else 1
    return pl.pallas_call(
        functools.partial(_bias_kernel, masked_cols=masked_cols),
        grid=(n_var, N_PAIRS),
        in_specs=[pl.BlockSpec(memory_space=pltpu.SMEM),
                  pl.BlockSpec((1, rows, cols), lambda v, p: (v, 0, 0))],
        out_specs=pl.BlockSpec((n_out, 1, 1, 2 * rows, cols), lambda v, p: (0, v, p, 0, 0)),
        out_shape=jax.ShapeDtypeStruct((n_out, n_var, N_PAIRS, 2 * rows, cols), F32),
        compiler_params=_params(2),
        name="bias_tables",
    )(table, jnp.asarray(idx))


def _in_proj_sample_kernel(x_ref, g_ref, w_ref, wkv_t_ref, q_ref, u_ref, kt_ref, vt_ref):
    h = _rms(x_ref[...], g_ref[...]).astype(BF16)
    aw = q_ref.shape[1]
    cw = u_ref.shape[1]
    q_ref[...] = _dot(h, w_ref[:, 0:aw]) * ATT_SCALE
    a = _dot(h, w_ref[:, 3 * aw:3 * aw + cw])
    gate = _dot(h, w_ref[:, 3 * aw + cw:3 * aw + 2 * cw])
    u_ref[...] = a * _sigmoid(gate)
    kt_ref[...] = _dot_nt(wkv_t_ref[:aw, :], h)
    vt_ref[...] = _dot_nt(wkv_t_ref[aw:, :], h)


def _in_proj_sample(x, g, w_bf16, wkv_t_bf16):
    n, d = x.shape
    aw = ATT_WIDTH
    cw = (w_bf16.shape[1] - 3 * aw) // 2
    new_shape = jax.ShapeDtypeStruct((aw, n), F32)
    return pl.pallas_call(
        _in_proj_sample_kernel,
        out_shape=[jax.ShapeDtypeStruct((n, aw), F32), jax.ShapeDtypeStruct((n, cw), F32), new_shape, new_shape],
        compiler_params=pltpu.CompilerParams(vmem_limit_bytes=VMEM_LIMIT),
        name="in_proj_sample",
    )(x, g, w_bf16, wkv_t_bf16)


def _in_proj_prompt_kernel(x_ref, g_ref, w_ref, wkv_t_ref, q_ref, k_ref, v_ref, u_ref, kt_ref, vt_ref, xs, *,
                           tiles_per_seq, tail_first):
    tm, d = x_ref.shape
    per = tm // N_CLS
    aw = q_ref.shape[3]
    cw = u_ref.shape[1]
    g = g_ref[...]
    x = x_ref[...]
    hn = _rms(x, g).astype(BF16)
    gate = _dot(hn, w_ref[:, 3 * aw + cw:3 * aw + 2 * cw])
    u_ref[...] = _dot(hn, w_ref[:, 3 * aw:3 * aw + cw]) * _sigmoid(gate)

    for c in range(d // PAIR_W):
        xs[c] = x[:, c * PAIR_W:(c + 1) * PAIR_W]
    x_cls = jnp.concatenate(
        [jnp.concatenate([xs[c, pl.ds(r, per, stride=N_CLS), :] for r in range(N_CLS)], axis=0)
         for c in range(d // PAIR_W)], axis=1)
    h = _rms(x_cls, g).astype(BF16)
    q_ref[0] = (_dot(h, w_ref[:, 0:aw]) * ATT_SCALE).reshape(N_CLS, per, aw)
    k_ref[0] = _dot(h, w_ref[:, aw:2 * aw]).reshape(N_CLS, per, aw)
    v_ref[0] = _dot(h, w_ref[:, 2 * aw:3 * aw]).reshape(N_CLS, per, aw)

    @pl.when(pl.program_id(0) % tiles_per_seq >= tail_first)
    def _():
        kt_ref[0] = _dot_nt(wkv_t_ref[:aw, :], hn)
        vt_ref[0] = _dot_nt(wkv_t_ref[aw:, :], hn)


def _in_proj_prompt(x, g, w_bf16, wkv_t_bf16, seq, keep, tm):
    n, d = x.shape
    aw = ATT_WIDTH
    cw = (w_bf16.shape[1] - 3 * aw) // 2
    chunk = N_CLS * STEPS
    tiles_per_chunk = chunk // tm
    tiles_per_seq = seq // tm
    tail_first = (seq - keep) // tm
    row = lambda i: (i, 0)
    fixed = lambda i: (0, 0)
    cls = pl.BlockSpec((1, N_CLS, tm // N_CLS, aw), lambda i: (i // tiles_per_chunk, 0, i % tiles_per_chunk, 0))
    tail = pl.BlockSpec((1, aw, tm),
                        lambda i: (i // tiles_per_seq, 0, jnp.maximum(i % tiles_per_seq - tail_first, 0)))
    cls_shape = jax.ShapeDtypeStruct((n // chunk, N_CLS, STEPS, aw), F32)
    tail_shape = jax.ShapeDtypeStruct((n // seq, aw, keep), F32)
    return pl.pallas_call(
        functools.partial(_in_proj_prompt_kernel, tiles_per_seq=tiles_per_seq, tail_first=tail_first),
        grid=(n // tm,),
        in_specs=[pl.BlockSpec((tm, d), row), pl.BlockSpec((1, d), fixed),
                  pl.BlockSpec(w_bf16.shape, fixed), pl.BlockSpec(wkv_t_bf16.shape, fixed)],
        out_specs=[cls, cls, cls, pl.BlockSpec((tm, cw), row), tail, tail],
        out_shape=[cls_shape, cls_shape, cls_shape, jax.ShapeDtypeStruct((n, cw), F32), tail_shape, tail_shape],
        scratch_shapes=[pltpu.VMEM((d // PAIR_W, tm, PAIR_W), F32)],
        compiler_params=_params(1),
        name="in_proj_prompt",
    )(x, g, w_bf16, wkv_t_bf16)


def _stack_heads(x, first):
    zero = jnp.zeros_like(x)
    return jnp.concatenate([jnp.where(first, x, zero), jnp.where(first, zero, x)], axis=0)


def _att_kernel(q_ref, kc_ref, kp_ref, vc_ref, vp_ref, bias_ref, bias0_ref, o_ref, num_s, m_s, l_s):
    first = lax.broadcasted_iota(jnp.int32, (STEPS, PAIR_W), 1) < HEAD_DIM
    ones = jnp.ones((2 * STEPS, PAIR_W), BF16)

    def pieces(dil, cls, blk):
        groups = N_CLS // dil
        per = STEPS // groups
        return [(cls + dil * j, pl.ds(blk * per, per)) for j in range(groups)], per

    def load(ref, dil, cls, blk):
        idx, _ = pieces(dil, cls, blk)
        return jnp.concatenate([ref[0, r, rows, :] for r, rows in idx], axis=0)

    def store(ref, g, dil, cls, blk, val):
        idx, per = pieces(dil, cls, blk)
        for j, (r, rows) in enumerate(idx):
            ref[g, r, rows, :] = val[j * per:(j + 1) * per]

    def attend(g, dil, cls, blk, k_prev, v_prev, bias):
        q_st = _stack_heads(load(q_ref, dil, cls, blk), first).astype(BF16)
        kcat = jnp.concatenate([k_prev, load(kc_ref, dil, cls, blk)], axis=0).astype(BF16)
        vcat = jnp.concatenate([v_prev, load(vc_ref, dil, cls, blk)], axis=0).astype(BF16)
        s = _dot_nt(q_st, kcat) + bias
        m = jnp.max(s, axis=-1, keepdims=True)
        p = jnp.exp2(s - m).astype(BF16)
        o = _dot(p, jnp.concatenate([vcat, ones], axis=1))
        store(num_s, g, dil, cls, blk, jnp.where(first, o[:STEPS, :PAIR_W], o[STEPS:, :PAIR_W]))
        store(l_s, g, dil, cls, blk, jnp.where(first, o[:STEPS, PAIR_W:], o[STEPS:, PAIR_W:]))
        store(m_s, g, dil, cls, blk, jnp.where(first, jnp.broadcast_to(m[:STEPS], (STEPS, PAIR_W)),
                                               jnp.broadcast_to(m[STEPS:], (STEPS, PAIR_W))))

    for g, (_, dil) in enumerate(DILATED_BRANCHES):
        n_blk = N_CLS // dil
        for cls in range(dil):
            attend(g, dil, cls, 0, load(kp_ref, dil, cls, n_blk - 1), load(vp_ref, dil, cls, n_blk - 1),
                   bias0_ref[0, g, 0])

            for blk in range(1, n_blk):
                attend(g, dil, cls, blk, load(kc_ref, dil, cls, blk - 1), load(vc_ref, dil, cls, blk - 1),
                       bias_ref[g, 0])

    for r in range(N_CLS):
        m_all = jnp.maximum(jnp.maximum(m_s[0, r], m_s[1, r]), m_s[2, r])
        num = jnp.zeros(m_all.shape, F32)
        den = jnp.zeros(m_all.shape, F32)
        for g in range(len(DILATED_BRANCHES)):
            w = jnp.exp2(m_s[g, r] - m_all)
            num = num + w * num_s[g, r]
            den = den + w * l_s[g, r]
        o_ref[pl.ds(r, STEPS, stride=N_CLS), :] = num / den


def _dilated_attention_prompt(q, k, v, bias, chunks_per_seq):
    n_chunks = q.shape[0]
    n_br = len(DILATED_BRANCHES)
    cur = lambda b, c, p: (b * chunks_per_seq + c, 0, 0, p)
    prev = lambda b, c, p: (b * chunks_per_seq + jnp.maximum(c - 1, 0), 0, 0, p)
    blk = pl.BlockSpec((1, N_CLS, STEPS, PAIR_W), cur)
    blk_prev = pl.BlockSpec((1, N_CLS, STEPS, PAIR_W), prev)
    return pl.pallas_call(
        _att_kernel,
        grid=(n_chunks // chunks_per_seq, chunks_per_seq, N_PAIRS),
        in_specs=[blk, blk, blk_prev, blk, blk_prev,
                  pl.BlockSpec((n_br, 1, 2 * STEPS, 2 * STEPS), lambda b, c, p: (0, p, 0, 0)),
                  pl.BlockSpec((1, n_br, 1, 2 * STEPS, 2 * STEPS),
                               lambda b, c, p: (jnp.where(c == 0, 1, 0), 0, p, 0, 0))],
        out_specs=pl.BlockSpec((N_CLS * STEPS, PAIR_W), lambda b, c, p: (b * chunks_per_seq + c, p)),
        out_shape=jax.ShapeDtypeStruct((n_chunks * N_CLS * STEPS, ATT_WIDTH), F32),
        scratch_shapes=[pltpu.VMEM((n_br, N_CLS, STEPS, PAIR_W), F32)] * 3,
        compiler_params=_params(3),
        name="dilated_attention_prompt",
    )(q, k, k, v, v, bias[0], bias)


def _sample_att_kernel(q_ref, kn_ref, vn_ref, ck_ref, cv_ref, bias_ref, ok_ref, ov_ref, o_ref):
    buf_len = ck_ref.shape[2]
    t_new = q_ref.shape[0]
    tail = buf_len - NEW_LANES
    is_new = lax.broadcasted_iota(jnp.int32, (PAIR_W, NEW_LANES), 1) >= NEW_LANES - t_new
    first = lax.broadcasted_iota(jnp.int32, (t_new, PAIR_W), 1) < HEAD_DIM
    col0 = (pl.program_id(0) % (NEW_LANES // t_new)) * t_new
    to_end = (NEW_LANES - t_new - col0) % NEW_LANES

    for j in range(ck_ref.shape[1] // PAIR_W):
        ch = slice(j * PAIR_W, (j + 1) * PAIR_W)

        def shift_in(c_ref, n_ref, o_ref):
            rot = pltpu.roll(c_ref[0, ch, :], buf_len - t_new, axis=1)
            new = jnp.where(is_new, pltpu.roll(n_ref[ch, :], to_end, axis=1), 0.0)
            o_ref[0, ch, :tail] = rot[:, :tail]
            o_ref[0, ch, tail:] = jnp.where(is_new, new, rot[:, tail:])
            return rot.astype(BF16), new.astype(BF16)

        rot_k, new_k = shift_in(ck_ref, kn_ref, ok_ref)
        rot_v, new_v = shift_in(cv_ref, vn_ref, ov_ref)

        q_st = _stack_heads(q_ref[:, ch], first).astype(BF16)
        s = jnp.concatenate([_dot(q_st, rot_k), _dot(q_st, new_k)], axis=1)
        sg = [s + bias_ref[g, j] for g in range(len(DILATED_BRANCHES))]
        m = functools.reduce(jnp.maximum, [jnp.max(x, axis=-1, keepdims=True) for x in sg])
        p = functools.reduce(jnp.add, [jnp.exp2(x - m) for x in sg])
        den = jnp.sum(p, axis=-1, keepdims=True)
        pb = p.astype(BF16)
        o = (_dot_nt(pb[:, :buf_len], rot_v) + _dot_nt(pb[:, buf_len:], new_v)) / den
        o_ref[:, ch] = jnp.where(first, o[:t_new], o[t_new:]).astype(o_ref.dtype)


SAMPLE_PAIRS_PER_STEP = 3


def _dilated_attention_sample(q, k_new_t, v_new_t, cache_k_t, cache_v_t, bias, t_new):
    nb, _, buf_len = cache_k_t.shape
    n_br = len(DILATED_BRANCHES)
    pps = SAMPLE_PAIRS_PER_STEP
    width = pps * PAIR_W
    tok = pl.BlockSpec((t_new, width), lambda n, p: (n, p))
    new = pl.BlockSpec((width, NEW_LANES), lambda n, p: (p, n // (NEW_LANES // t_new)))
    buf = pl.BlockSpec((1, width, buf_len), lambda n, p: (n, p, 0))
    return pl.pallas_call(
        _sample_att_kernel,
        grid=(nb, N_PAIRS // pps),
        in_specs=[tok, new, new, buf, buf,
                  pl.BlockSpec((n_br, pps, 2 * t_new, buf_len + NEW_LANES), lambda n, p: (0, p, 0, 0))],
        out_specs=[buf, buf, tok],
        out_shape=[jax.ShapeDtypeStruct(cache_k_t.shape, F32), jax.ShapeDtypeStruct(cache_v_t.shape, F32),
                   jax.ShapeDtypeStruct((nb * t_new, ATT_WIDTH), BF16)],
        compiler_params=_params(2),
        name="dilated_attention_sample",
    )(q, k_new_t, v_new_t, cache_k_t, cache_v_t, bias)


def _conv_tail(y, cb_ref, lg_ref, lb_ref):
    y = y + cb_ref[...]
    mu = jnp.mean(y, axis=-1, keepdims=True)
    yc = y - mu
    var = jnp.mean(yc * yc, axis=-1, keepdims=True)
    yn = yc * lax.rsqrt(var + EPS) * lg_ref[...] + lb_ref[...]
    return yn * _sigmoid(yn)


def _mix_out(x, att, c, wo_ref):
    aw = att.shape[1]
    return x + _dot(att.astype(BF16), wo_ref[:aw, :]) + _dot(c.astype(BF16), wo_ref[aw:, :])


def _cross_attend(qx, head_kv):
    hd = qx.shape[1] // X_HEADS
    outs = []
    for h in range(X_HEADS):
        mk, mv = head_kv(h)
        s = _dot_nt(qx[:, h * hd:(h + 1) * hd].astype(BF16), mk)
        p = jnp.exp2(s - jnp.max(s, axis=-1, keepdims=True))
        den = jnp.sum(p, axis=-1, keepdims=True)
        outs.append((_dot(p.astype(BF16), mv) / den).astype(BF16))
    return jnp.concatenate(outs, axis=1)


def _prompt_post_kernel(x_ref, att_ref, u_ref, uh_ref, cw_ref, cb_ref, lg_ref, lb_ref, wo_ref,
                        gx_ref, wq_ref, mk_ref, mv_ref, wxo_ref, o_ref, ubuf, *, tiles_per_seq):
    tm = x_ref.shape[0]
    seq_start = (pl.program_id(0) % tiles_per_seq) == 0
    ubuf[:CONV_HALO, :] = jnp.where(seq_start, 0.0, uh_ref[...])
    ubuf[CONV_HALO:, :] = u_ref[...]
    hist = ubuf[...]
    n_rows = hist.shape[0]
    first_tap = CONV_HALO - (CONV_K - 1)
    y = jnp.zeros(u_ref.shape, F32)
    for s in range(F32_SUBLANES):
        shifted = hist if s == 0 else pltpu.roll(hist, n_rows - s, axis=0)
        for j in range(CONV_K):
            if (first_tap + j) % F32_SUBLANES == s:
                lo = first_tap + j - s
                y = y + cw_ref[j:j + 1, :] * shifted[lo:lo + tm]
    c = _conv_tail(y, cb_ref, lg_ref, lb_ref)
    x1 = _mix_out(x_ref[...], att_ref[...], c, wo_ref)
    hd = wq_ref.shape[1] // X_HEADS
    qx = _dot(_rms(x1, gx_ref[...]).astype(BF16), wq_ref[...]) * (hd ** -0.5 * LOG2E)
    head_kv = lambda h: (mk_ref[:, h * hd:(h + 1) * hd], mv_ref[:, h * hd:(h + 1) * hd])
    o_ref[...] = x1 + _dot(_cross_attend(qx, head_kv), wxo_ref[...])


def _prompt_post(x, att, u, conv_w, conv_b, ln_g, ln_b, w_out, gx, w_xq, mk, mv, w_xo, seq, tm):
    n, d = x.shape
    cw = u.shape[1]
    n_mem = mk.shape[0] // (n // seq)
    tiles_per_seq = seq // tm
    row = lambda i: (i, 0)
    fixed = lambda i: (0, 0)
    halo = lambda i: (jnp.maximum(i * (tm // CONV_HALO) - 1, 0), 0)
    per_seq = lambda i: (i // tiles_per_seq, 0)
    full = lambda a: pl.BlockSpec(a.shape, fixed)
    return pl.pallas_call(
        functools.partial(_prompt_post_kernel, tiles_per_seq=tiles_per_seq),
        grid=(n // tm,),
        in_specs=[pl.BlockSpec((tm, d), row), pl.BlockSpec((tm, ATT_WIDTH), row),
                  pl.BlockSpec((tm, cw), row), pl.BlockSpec((CONV_HALO, cw), halo),
                  full(conv_w), full(conv_b), full(ln_g), full(ln_b), full(w_out),
                  full(gx), full(w_xq),
                  pl.BlockSpec((n_mem, d), per_seq), pl.BlockSpec((n_mem, d), per_seq), full(w_xo)],
        out_specs=pl.BlockSpec((tm, d), row),
        out_shape=jax.ShapeDtypeStruct((n, d), F32),
        scratch_shapes=[pltpu.VMEM((CONV_HALO + tm, cw), F32)],
        compiler_params=_params(1),
        name="prompt_mix_cross",
    )(x, att, u, u, conv_w, conv_b, ln_g, ln_b, w_out, gx, w_xq, mk, mv, w_xo)


def _sample_mix_kernel(x_ref, att_ref, uf_ref, cw_ref, cb_ref, lg_ref, lb_ref, wo_ref, gx_ref, wq_ref,
                       x1_ref, qx_ref):
    nb, t_full, cw = uf_ref.shape
    t_new = t_full - (CONV_K - 1)
    y = jnp.zeros((nb, t_new, cw), F32)
    for j in range(CONV_K):
        y = y + cw_ref[j:j + 1, :] * uf_ref[:, j:j + t_new, :]
    c = _conv_tail(y.reshape(nb * t_new, cw), cb_ref, lg_ref, lb_ref)
    x1 = _mix_out(x_ref[...], att_ref[...], c, wo_ref)
    x1_ref[...] = x1
    hd = wq_ref.shape[1] // X_HEADS
    qx_ref[...] = _dot(_rms(x1, gx_ref[...]).astype(BF16), wq_ref[...]) * (hd ** -0.5 * LOG2E)


def _sample_mix(x, att, u_full, conv_w, conv_b, ln_g, ln_b, w_out, gx, w_xq):
    n, d = x.shape
    return pl.pallas_call(
        _sample_mix_kernel,
        out_shape=[jax.ShapeDtypeStruct((n, d), F32), jax.ShapeDtypeStruct((n, w_xq.shape[1]), F32)],
        compiler_params=pltpu.CompilerParams(vmem_limit_bytes=VMEM_LIMIT),
        name="sample_mix",
    )(x, att, u_full, conv_w, conv_b, ln_g, ln_b, w_out, gx, w_xq)


def _sample_cross_kernel(qx_ref, mk_hbm, mv_hbm, o_ref, kbuf, vbuf, sems):
    n = pl.program_id(0)

    def head_copies(seq, slot):
        return [pltpu.make_async_copy(src.at[seq, :, h, :], dst.at[slot, h], sems.at[i, slot, h])
                for i, (src, dst) in enumerate(((mk_hbm, kbuf), (mv_hbm, vbuf))) for h in range(X_HEADS)]

    @pl.when(n == 0)
    def _():
        for cp in head_copies(0, 0):
            cp.start()

    @pl.when(n + 1 < pl.num_programs(0))
    def _():
        for cp in head_copies(n + 1, (n + 1) % 2):
            cp.start()

    slot = n % 2
    for cp in head_copies(n, slot):
        cp.wait()
    head_kv = lambda h: (kbuf[slot, h].astype(BF16), vbuf[slot, h].astype(BF16))
    o_ref[...] = _cross_attend(qx_ref[...], head_kv)


def _sample_cross(qx, mem_k, mem_v, t_new):
    nb, n_mem, n_heads, hd = mem_k.shape
    tok = pl.BlockSpec((t_new, n_heads * hd), lambda n: (n, 0))
    hbm = pl.BlockSpec(memory_space=pl.ANY)
    return pl.pallas_call(
        _sample_cross_kernel,
        grid=(nb,),
        in_specs=[tok, hbm, hbm],
        out_specs=tok,
        out_shape=jax.ShapeDtypeStruct(qx.shape, BF16),
        scratch_shapes=[pltpu.VMEM((2, n_heads, n_mem, hd), F32), pltpu.VMEM((2, n_heads, n_mem, hd), F32),
                        pltpu.SemaphoreType.DMA((2, 2, n_heads))],
        compiler_params=_params(1),
        name="sample_cross",
    )(qx, mem_k, mem_v)


def _swiglu_final(x2, gf_ref, wg_ref, wu_ref, wd_ref, gfin_ref, ff_chunk):
    h = _rms(x2, gf_ref[...]).astype(BF16)
    acc = x2
    for lo in range(0, wg_ref.shape[1], ff_chunk):
        gate = _dot(h, wg_ref[:, lo:lo + ff_chunk])
        up = _dot(h, wu_ref[:, lo:lo + ff_chunk])
        acc = acc + _dot((gate * _sigmoid(gate) * up).astype(BF16), wd_ref[lo:lo + ff_chunk, :])
    return _rms(acc, gfin_ref[...])


def _ffn_kernel(x_ref, gf_ref, wg_ref, wu_ref, wd_ref, gfin_ref, o_ref, *, ff_chunk):
    o_ref[...] = _swiglu_final(x_ref[...], gf_ref, wg_ref, wu_ref, wd_ref, gfin_ref, ff_chunk)


def _sample_tail_kernel(x_ref, xo_ref, wxo_ref, gf_ref, wg_ref, wu_ref, wd_ref, gfin_ref, o_ref, *, ff_chunk):
    x2 = x_ref[...] + _dot(xo_ref[...], wxo_ref[...])
    o_ref[...] = _swiglu_final(x2, gf_ref, wg_ref, wu_ref, wd_ref, gfin_ref, ff_chunk)


FF_CHUNK = 256


def _resident(a):
    return pl.BlockSpec(a.shape, lambda *_: (0,) * a.ndim, pipeline_mode=pl.Buffered(1))


def _ffn(x, gf, wg, wu, wd, gfin, tm):
    n, d = x.shape
    row = lambda i: (i, 0)
    full = _resident
    return pl.pallas_call(
        functools.partial(_ffn_kernel, ff_chunk=FF_CHUNK),
        grid=(n // tm,),
        in_specs=[pl.BlockSpec((tm, d), row), full(gf), full(wg), full(wu), full(wd), full(gfin)],
        out_specs=pl.BlockSpec((tm, d), row),
        out_shape=jax.ShapeDtypeStruct((n, d), F32),
        compiler_params=_params(1),
        name="swiglu_final_norm",
    )(x, gf, wg, wu, wd, gfin)


def _sample_tail(x1, xo, w_xo, gf, wg, wu, wd, gfin):
    return pl.pallas_call(
        functools.partial(_sample_tail_kernel, ff_chunk=FF_CHUNK),
        out_shape=jax.ShapeDtypeStruct(x1.shape, F32),
        compiler_params=pltpu.CompilerParams(vmem_limit_bytes=VMEM_LIMIT),
        name="sample_cross_out_swiglu",
    )(x1, xo, w_xo, gf, wg, wu, wd, gfin)


def _mem_kv_kernel(m_ref, g_ref, wk_ref, wv_ref, k_ref, v_ref, kb_ref, vb_ref):
    h = _rms(m_ref[...], g_ref[...]).astype(BF16)
    k = _dot(h, wk_ref[...])
    v = _dot(h, wv_ref[...])
    k_ref[...] = k
    v_ref[...] = v
    kb_ref[...] = k.astype(BF16)
    vb_ref[...] = v.astype(BF16)


def _mem_kv(mem, g, w_k, w_v, tm):
    n, d = mem.shape
    row = lambda i: (i, 0)
    full = lambda a: pl.BlockSpec(a.shape, lambda i: (0, 0))
    dk = w_k.shape[1]
    return pl.pallas_call(
        _mem_kv_kernel,
        grid=(n // tm,),
        in_specs=[pl.BlockSpec((tm, d), row), full(g), full(w_k), full(w_v)],
        out_specs=[pl.BlockSpec((tm, dk), row)] * 4,
        out_shape=[jax.ShapeDtypeStruct((n, dk), F32)] * 2 + [jax.ShapeDtypeStruct((n, dk), BF16)] * 2,
        compiler_params=_params(1),
        name="mem_kv",
    )(mem, g, w_k, w_v)


ROW_TILE = 512
FFN_ROW_TILE = 1024
ATT_CHUNK = STEPS * DILATED_BRANCHES[-1][1]


def kernel(x_prompt, x_sample, mem_prompt, cache_win_k, cache_win_v, cache_conv, cache_mem_k, cache_mem_v,
           rpb_table, norm_mix_g, w_in, conv_w, conv_b, conv_ln_g, conv_ln_b, w_out, norm_x_g, norm_mem_g,
           w_xq, w_xk, w_xv, w_xo, norm_ffn_g, w_ffn_gate, w_ffn_up, w_ffn_down, norm_final_g):
    depth = w_in.shape[0]
    assert depth == 1, "single-layer stack"
    batch, seq, d = x_prompt.shape
    nb, t_new, _ = x_sample.shape
    buf_len = cache_win_k.shape[2]
    keep_p = min(MAX_DISTANCE, seq)
    n_mem = mem_prompt.shape[1]
    conv_hist = CONV_K - 1
    assert seq % ATT_CHUNK == 0 and keep_p % ATT_CHUNK == 0 and buf_len == MAX_DISTANCE

    row = lambda a: a.reshape(1, -1)
    bf = lambda a: a.astype(BF16)
    l = 0
    w_in_b, w_out_b = bf(w_in[l]), bf(w_out[l])
    w_xq_b, w_xk_b, w_xv_b, w_xo_b = bf(w_xq[l]), bf(w_xk[l]), bf(w_xv[l]), bf(w_xo[l])
    w_g_b, w_u_b, w_d_b = bf(w_ffn_gate[l]), bf(w_ffn_up[l]), bf(w_ffn_down[l])
    g_mix, g_x, g_mem, g_ffn, g_fin = (row(norm_mix_g[l]), row(norm_x_g[l]), row(norm_mem_g[l]),
                                       row(norm_ffn_g[l]), row(norm_final_g))
    cv_w, cv_b, ln_g, ln_b = conv_w[l], row(conv_b[l]), row(conv_ln_g[l]), row(conv_ln_b[l])

    bias_p = _bias_tables(rpb_table, _prompt_bucket_index(), masked_cols=STEPS)
    bias_s = _bias_tables(rpb_table, _sample_bucket_index(buf_len, t_new))[0]

    xp = x_prompt.reshape(batch * seq, d)
    wkv_t_b = bf(jnp.transpose(w_in[l][:, ATT_WIDTH:3 * ATT_WIDTH]))
    q, k, v, u, p_wk_t, p_wv_t = _in_proj_prompt(xp, g_mix, w_in_b, wkv_t_b, seq, keep_p, ROW_TILE)
    att = _dilated_attention_prompt(q, k, v, bias_p, seq // ATT_CHUNK)
    mk, mv, mk_b, mv_b = _mem_kv(mem_prompt.reshape(batch * n_mem, d), g_mem, w_xk_b, w_xv_b, n_mem)
    x2 = _prompt_post(xp, att, u, cv_w, cv_b, ln_g, ln_b, w_out_b, g_x, w_xq_b, mk_b, mv_b, w_xo_b,
                      seq, ROW_TILE)
    y_prompt = _ffn(x2, g_ffn, w_g_b, w_u_b, w_d_b, g_fin, FFN_ROW_TILE).reshape(batch, seq, d)

    tok_major = lambda a: jnp.transpose(a.reshape(a.shape[0], N_ATT_HEADS, HEAD_DIM, -1), (0, 3, 1, 2))[None]
    p_wk, p_wv = tok_major(p_wk_t), tok_major(p_wv_t)
    u3 = u.reshape(batch, seq, -1)
    p_conv = u3[:, seq - conv_hist:][None]
    xh = lambda a, n: a.reshape(1, n, n_mem, X_HEADS, -1)
    p_mk, p_mv = xh(mk, batch), xh(mv, batch)

    xs = x_sample.reshape(nb * t_new, d)
    qs, us, ks_t, vs_t = _in_proj_sample(xs, g_mix, w_in_b, wkv_t_b)
    chan_major = lambda a: jnp.transpose(a, (0, 2, 3, 1)).reshape(nb, ATT_WIDTH, -1)
    s_wk_t, s_wv_t, att_s = _dilated_attention_sample(
        qs, ks_t, vs_t, chan_major(cache_win_k[l]), chan_major(cache_win_v[l]), bias_s, t_new)
    u_full = jnp.concatenate([cache_conv[l], us.reshape(nb, t_new, -1)], axis=1)
    x1s, qxs = _sample_mix(xs, att_s, u_full, cv_w, cv_b, ln_g, ln_b, w_out_b, g_x, w_xq_b)
    xo_s = _sample_cross(qxs, cache_mem_k[l], cache_mem_v[l], t_new)
    y_sample = _sample_tail(x1s, xo_s, w_xo_b, g_ffn, w_g_b, w_u_b, w_d_b, g_fin).reshape(nb, t_new, d)

    return (y_prompt, y_sample, p_wk, p_wv, p_conv, p_mk, p_mv,
            tok_major(s_wk_t), tok_major(s_wv_t), u_full[:, t_new:][None])
```

```python
import functools
import math

import numpy as np
import jax
import jax.numpy as jnp
from jax import lax
from jax.experimental import pallas as pl
from jax.experimental.pallas import tpu as pltpu

F32 = jnp.float32
BF16 = jnp.bfloat16

HEAD_DIM = 64
N_ATT_HEADS = 12
ATT_WIDTH = N_ATT_HEADS * HEAD_DIM
PAIR_W = 2 * HEAD_DIM
N_PAIRS = N_ATT_HEADS // 2
CONV_K = 31
CONV_HALO = 32
F32_SUBLANES = 8
DILATED_BRANCHES = ((128, 1), (512, 4), (2048, 16))
STEPS = 128
N_CLS = DILATED_BRANCHES[-1][1]
GATHER_STRIDE = 4
NEW_LANES = 128
N_BUCKETS = 32
MAX_EXACT = N_BUCKETS // 2
MAX_DISTANCE = 2048
X_HEADS = 4
EPS = 1e-6
LOG2E = math.log2(math.e)
ATT_SCALE = HEAD_DIM ** -0.5 * LOG2E
MASKED = -1e30

V7X_VMEM_BYTES = 64 * 1024 * 1024
VMEM_LIMIT = V7X_VMEM_BYTES * 3 // 4


def _params(n_grid_dims):
    return pltpu.CompilerParams(dimension_semantics=("arbitrary",) * n_grid_dims,
                                vmem_limit_bytes=VMEM_LIMIT)


def _rms(x, g):
    return x * lax.rsqrt(jnp.mean(x * x, axis=-1, keepdims=True) + EPS) * g


def _sigmoid(x):
    return 1.0 / (1.0 + jnp.exp(-x))


def _dot(a, b):
    return jnp.dot(a, b, preferred_element_type=F32)


def _dot_nt(a, b):
    return lax.dot_general(a, b, (((1,), (1,)), ((), ())), preferred_element_type=F32)


def _t5_bucket_np(dist):
    n = dist.astype(np.int32)
    nf = np.maximum(n, MAX_EXACT).astype(np.float32)
    large = MAX_EXACT + (np.log(nf / np.float32(MAX_EXACT)) / np.float32(math.log(MAX_DISTANCE / MAX_EXACT))
                         * np.float32(N_BUCKETS - MAX_EXACT)).astype(np.int32)
    large = np.minimum(large, N_BUCKETS - 1)
    return np.where(n < MAX_EXACT, n, large)


def _block_order(dil):
    groups = N_CLS // dil
    per = STEPS // groups
    i = np.arange(STEPS)
    return (i % per) * groups + i // per


def _prompt_bucket_index():
    out = []
    for _, dil in DILATED_BRANCHES:
        n = _block_order(dil)
        kj = np.concatenate([n, STEPS + n])[None, :]
        sub = STEPS + n[:, None] - kj
        band = (sub >= 0) & (sub <= STEPS)
        out.append(np.where(band, _t5_bucket_np(dil * np.maximum(sub, 0)), -1))
    return np.stack(out).astype(np.int32)


def _sample_bucket_index(buf_len, t_new):
    col = np.arange(buf_len + NEW_LANES)
    key = np.where(col < buf_len, (col + t_new) % buf_len, col - NEW_LANES + t_new)
    keep = (col < buf_len) | (col >= buf_len + NEW_LANES - t_new)
    i = np.arange(t_new)[:, None]
    dist = buf_len + i - key[None, :]
    out = []
    for window, dil in DILATED_BRANCHES:
        ok = keep[None, :] & (dist >= 0) & (dist % dil == 0) & (dist // dil <= window // dil)
        out.append(np.where(ok, _t5_bucket_np(np.maximum(dist, 0)), -1))
    return np.stack(out).astype(np.int32)


def _bias_kernel(tab_ref, idx_ref, o_ref, *, masked_cols):
    pair = pl.program_id(1)
    idx = idx_ref[0]
    rows = idx.shape[0]
    for half in range(2):
        head = 2 * pair + half
        acc = jnp.full(idx.shape, MASKED, F32)
        for b in range(N_BUCKETS):
            acc = jnp.where(idx == b, tab_ref[b, head] * LOG2E, acc)
        o_ref[0, 0, 0, half * rows:(half + 1) * rows, :] = acc
        if masked_cols:
            col = lax.broadcasted_iota(jnp.int32, idx.shape, 1)
            o_ref[1, 0, 0, half * rows:(half + 1) * rows, :] = jnp.where(col < masked_cols, MASKED, acc)


def _bias_tables(table, idx, masked_cols=0):
    n_var, rows, cols = idx.shape
    n_out = 2 if masked_cols else 1
    return pl.pallas_call(
        functools.partial(_bias_kernel, masked_cols=masked_cols),
        grid=(n_var, N_PAIRS),
        in_specs=[pl.BlockSpec(memory_space=pltpu.SMEM),
                  pl.BlockSpec((1, rows, cols), lambda v, p: (v, 0, 0))],
        out_specs=pl.BlockSpec((n_out, 1, 1, 2 * rows, cols), lambda v, p: (0, v, p, 0, 0)),
        out_shape=jax.ShapeDtypeStruct((n_out, n_var, N_PAIRS, 2 * rows, cols), F32),
        compiler_params=_params(2),
        name="bias_tables",
    )(table, jnp.asarray(idx))


def _in_proj_sample_kernel(x_ref, g_ref, w_ref, wkv_t_ref, q_ref, u_ref, kt_ref, vt_ref):
    h = _rms(x_ref[...], g_ref[...]).astype(BF16)
    aw = q_ref.shape[1]
    cw = u_ref.shape[1]
    q_ref[...] = _dot(h, w_ref[:, 0:aw]) * ATT_SCALE
    a = _dot(h, w_ref[:, 3 * aw:3 * aw + cw])
    gate = _dot(h, w_ref[:, 3 * aw + cw:3 * aw + 2 * cw])
    u_ref[...] = a * _sigmoid(gate)
    kt_ref[...] = _dot_nt(wkv_t_ref[:aw, :], h)
    vt_ref[...] = _dot_nt(wkv_t_ref[aw:, :], h)


def _in_proj_sample(x, g, w_bf16, wkv_t_bf16):
    n, d = x.shape
    aw = ATT_WIDTH
    cw = (w_bf16.shape[1] - 3 * aw) // 2
    new_shape = jax.ShapeDtypeStruct((aw, n), F32)
    return pl.pallas_call(
        _in_proj_sample_kernel,
        out_shape=[jax.ShapeDtypeStruct((n, aw), F32), jax.ShapeDtypeStruct((n, cw), F32), new_shape, new_shape],
        compiler_params=pltpu.CompilerParams(vmem_limit_bytes=VMEM_LIMIT),
        name="in_proj_sample",
    )(x, g, w_bf16, wkv_t_bf16)


def _in_proj_prompt_kernel(x_ref, g_ref, w_ref, wkv_t_ref, q_ref, k_ref, v_ref, u_ref, kt_ref, vt_ref, xs, xs2, *,
                           tiles_per_seq, tail_first):
    tm, d = x_ref.shape
    per = tm // N_CLS
    aw = q_ref.shape[3]
    cw = u_ref.shape[1]
    g = g_ref[...]
    x = x_ref[...]
    hn = _rms(x, g).astype(BF16)
    gate = _dot(hn, w_ref[:, 3 * aw + cw:3 * aw + 2 * cw])
    u_ref[...] = _dot(hn, w_ref[:, 3 * aw:3 * aw + cw]) * _sigmoid(gate)

    quarter = tm // GATHER_STRIDE
    for c in range(d // PAIR_W):
        xs[c] = x[:, c * PAIR_W:(c + 1) * PAIR_W]
    for c in range(d // PAIR_W):
        for res in range(GATHER_STRIDE):
            xs2[c, res * quarter:(res + 1) * quarter, :] = xs[c, pl.ds(res, quarter, stride=GATHER_STRIDE), :]

    def class_rows(c, r):
        start = (r % GATHER_STRIDE) * quarter + r // GATHER_STRIDE
        return xs2[c, pl.ds(start, per, stride=GATHER_STRIDE), :]

    x_cls = jnp.concatenate(
        [jnp.concatenate([class_rows(c, r) for r in range(N_CLS)], axis=0) for c in range(d // PAIR_W)], axis=1)
    h = _rms(x_cls, g).astype(BF16)
    q_ref[0] = (_dot(h, w_ref[:, 0:aw]) * ATT_SCALE).reshape(N_CLS, per, aw)
    k_ref[0] = _dot(h, w_ref[:, aw:2 * aw]).reshape(N_CLS, per, aw)
    v_ref[0] = _dot(h, w_ref[:, 2 * aw:3 * aw]).reshape(N_CLS, per, aw)

    @pl.when(pl.program_id(0) % tiles_per_seq >= tail_first)
    def _():
        kt_ref[0] = _dot_nt(wkv_t_ref[:aw, :], hn)
        vt_ref[0] = _dot_nt(wkv_t_ref[aw:, :], hn)


def _in_proj_prompt(x, g, w_bf16, wkv_t_bf16, seq, keep, tm):
    n, d = x.shape
    aw = ATT_WIDTH
    cw = (w_bf16.shape[1] - 3 * aw) // 2
    chunk = N_CLS * STEPS
    tiles_per_chunk = chunk // tm
    tiles_per_seq = seq // tm
    tail_first = (seq - keep) // tm
    row = lambda i: (i, 0)
    fixed = lambda i: (0, 0)
    cls = pl.BlockSpec((1, N_CLS, tm // N_CLS, aw), lambda i: (i // tiles_per_chunk, 0, i % tiles_per_chunk, 0))
    tail = pl.BlockSpec((1, aw, tm),
                        lambda i: (i // tiles_per_seq, 0, jnp.maximum(i % tiles_per_seq - tail_first, 0)))
    cls_shape = jax.ShapeDtypeStruct((n // chunk, N_CLS, STEPS, aw), F32)
    tail_shape = jax.ShapeDtypeStruct((n // seq, aw, keep), F32)
    return pl.pallas_call(
        functools.partial(_in_proj_prompt_kernel, tiles_per_seq=tiles_per_seq, tail_first=tail_first),
        grid=(n // tm,),
        in_specs=[pl.BlockSpec((tm, d), row), pl.BlockSpec((1, d), fixed),
                  pl.BlockSpec(w_bf16.shape, fixed), pl.BlockSpec(wkv_t_bf16.shape, fixed)],
        out_specs=[cls, cls, cls, pl.BlockSpec((tm, cw), row), tail, tail],
        out_shape=[cls_shape, cls_shape, cls_shape, jax.ShapeDtypeStruct((n, cw), F32), tail_shape, tail_shape],
        scratch_shapes=[pltpu.VMEM((d // PAIR_W, tm, PAIR_W), F32)] * 2,
        compiler_params=_params(1),
        name="in_proj_prompt",
    )(x, g, w_bf16, wkv_t_bf16)


def _stack_heads(x, first):
    zero = jnp.zeros_like(x)
    return jnp.concatenate([jnp.where(first, x, zero), jnp.where(first, zero, x)], axis=0)


def _att_kernel(q_ref, kc_ref, kp_ref, vc_ref, vp_ref, bias_ref, bias0_ref, o_ref, num_s, m_s, l_s, nat_s):
    first = lax.broadcasted_iota(jnp.int32, (STEPS, PAIR_W), 1) < HEAD_DIM
    ones = jnp.ones((2 * STEPS, PAIR_W), BF16)

    def pieces(dil, cls, blk):
        groups = N_CLS // dil
        per = STEPS // groups
        return [(cls + dil * j, pl.ds(blk * per, per)) for j in range(groups)], per

    def load(ref, dil, cls, blk):
        idx, _ = pieces(dil, cls, blk)
        return jnp.concatenate([ref[0, r, rows, :] for r, rows in idx], axis=0)

    def store(ref, g, dil, cls, blk, val):
        idx, per = pieces(dil, cls, blk)
        for j, (r, rows) in enumerate(idx):
            ref[g, r, rows, :] = val[j * per:(j + 1) * per]

    def attend(g, dil, cls, blk, k_prev, v_prev, bias):
        q_st = _stack_heads(load(q_ref, dil, cls, blk), first).astype(BF16)
        k_own = load(kc_ref, dil, cls, blk).astype(BF16)
        v_own = load(vc_ref, dil, cls, blk).astype(BF16)
        kcat = jnp.concatenate([k_prev, k_own], axis=0)
        vcat = jnp.concatenate([v_prev, v_own], axis=0)
        s = _dot_nt(q_st, kcat) + bias
        m = jnp.max(s, axis=-1, keepdims=True)
        p = jnp.exp2(s - m).astype(BF16)
        o = _dot(p, jnp.concatenate([vcat, ones], axis=1))
        store(num_s, g, dil, cls, blk, jnp.where(first, o[:STEPS, :PAIR_W], o[STEPS:, :PAIR_W]))
        store(l_s, g, dil, cls, blk, jnp.where(first, o[:STEPS, PAIR_W:], o[STEPS:, PAIR_W:]))
        store(m_s, g, dil, cls, blk, jnp.where(first, jnp.broadcast_to(m[:STEPS], (STEPS, PAIR_W)),
                                               jnp.broadcast_to(m[STEPS:], (STEPS, PAIR_W))))
        return k_own, v_own

    for g, (_, dil) in enumerate(DILATED_BRANCHES):
        n_blk = N_CLS // dil
        for cls in range(dil):
            kv = attend(g, dil, cls, 0, load(kp_ref, dil, cls, n_blk - 1).astype(BF16),
                        load(vp_ref, dil, cls, n_blk - 1).astype(BF16), bias0_ref[0, g, 0])
            for blk in range(1, n_blk):
                kv = attend(g, dil, cls, blk, *kv, bias_ref[g, 0])

    quarter = N_CLS * STEPS // GATHER_STRIDE
    for r in range(N_CLS):
        m_all = jnp.maximum(jnp.maximum(m_s[0, r], m_s[1, r]), m_s[2, r])
        num = jnp.zeros(m_all.shape, F32)
        den = jnp.zeros(m_all.shape, F32)
        for g in range(len(DILATED_BRANCHES)):
            w = jnp.exp2(m_s[g, r] - m_all)
            num = num + w * num_s[g, r]
            den = den + w * l_s[g, r]
        start = (r % GATHER_STRIDE) * quarter + r // GATHER_STRIDE
        nat_s[pl.ds(start, STEPS, stride=GATHER_STRIDE), :] = num / den
    for res in range(GATHER_STRIDE):
        o_ref[pl.ds(res, quarter, stride=GATHER_STRIDE), :] = nat_s[res * quarter:(res + 1) * quarter, :]


def _dilated_attention_prompt(q, k, v, bias, chunks_per_seq):
    n_chunks = q.shape[0]
    n_br = len(DILATED_BRANCHES)
    cur = lambda b, c, p: (b * chunks_per_seq + c, 0, 0, p)
    prev = lambda b, c, p: (b * chunks_per_seq + jnp.maximum(c - 1, 0), 0, 0, p)
    blk = pl.BlockSpec((1, N_CLS, STEPS, PAIR_W), cur)
    blk_prev = pl.BlockSpec((1, N_CLS, STEPS, PAIR_W), prev)
    return pl.pallas_call(
        _att_kernel,
        grid=(n_chunks // chunks_per_seq, chunks_per_seq, N_PAIRS),
        in_specs=[blk, blk, blk_prev, blk, blk_prev,
                  pl.BlockSpec((n_br, 1, 2 * STEPS, 2 * STEPS), lambda b, c, p: (0, p, 0, 0)),
                  pl.BlockSpec((1, n_br, 1, 2 * STEPS, 2 * STEPS),
                               lambda b, c, p: (jnp.where(c == 0, 1, 0), 0, p, 0, 0))],
        out_specs=pl.BlockSpec((N_CLS * STEPS, PAIR_W), lambda b, c, p: (b * chunks_per_seq + c, p)),
        out_shape=jax.ShapeDtypeStruct((n_chunks * N_CLS * STEPS, ATT_WIDTH), F32),
        scratch_shapes=[pltpu.VMEM((n_br, N_CLS, STEPS, PAIR_W), F32)] * 3 + [pltpu.VMEM((N_CLS * STEPS, PAIR_W), F32)],
        compiler_params=_params(3),
        name="dilated_attention_prompt",
    )(q, k, k, v, v, bias[0], bias)


def _sample_att_kernel(q_ref, kn_ref, vn_ref, ck_ref, cv_ref, bias_ref, ok_ref, ov_ref, o_ref):
    buf_len = ck_ref.shape[2]
    t_new = q_ref.shape[0]
    tail = buf_len - NEW_LANES
    is_new = lax.broadcasted_iota(jnp.int32, (PAIR_W, NEW_LANES), 1) >= NEW_LANES - t_new
    first = lax.broadcasted_iota(jnp.int32, (t_new, PAIR_W), 1) < HEAD_DIM
    col0 = (pl.program_id(0) % (NEW_LANES // t_new)) * t_new
    to_end = (NEW_LANES - t_new - col0) % NEW_LANES

    for j in range(ck_ref.shape[1] // PAIR_W):
        ch = slice(j * PAIR_W, (j + 1) * PAIR_W)

        def shift_in(c_ref, n_ref, o_ref):
            rot = pltpu.roll(c_ref[0, ch, :], buf_len - t_new, axis=1)
            new = jnp.where(is_new, pltpu.roll(n_ref[ch, :], to_end, axis=1), 0.0)
            o_ref[0, ch, :tail] = rot[:, :tail]
            o_ref[0, ch, tail:] = jnp.where(is_new, new, rot[:, tail:])
            return rot.astype(BF16), new.astype(BF16)

        rot_k, new_k = shift_in(ck_ref, kn_ref, ok_ref)
        rot_v, new_v = shift_in(cv_ref, vn_ref, ov_ref)

        q_st = _stack_heads(q_ref[:, ch], first).astype(BF16)
        s = jnp.concatenate([_dot(q_st, rot_k), _dot(q_st, new_k)], axis=1)
        sg = [s + bias_ref[g, j] for g in range(len(DILATED_BRANCHES))]
        m = functools.reduce(jnp.maximum, [jnp.max(x, axis=-1, keepdims=True) for x in sg])
        p = functools.reduce(jnp.add, [jnp.exp2(x - m) for x in sg])
        den = jnp.sum(p, axis=-1, keepdims=True)
        pb = p.astype(BF16)
        o = (_dot_nt(pb[:, :buf_len], rot_v) + _dot_nt(pb[:, buf_len:], new_v)) / den
        o_ref[:, ch] = jnp.where(first, o[:t_new], o[t_new:]).astype(o_ref.dtype)


SAMPLE_PAIRS_PER_STEP = 3


def _dilated_attention_sample(q, k_new_t, v_new_t, cache_k_t, cache_v_t, bias, t_new):
    nb, _, buf_len = cache_k_t.shape
    n_br = len(DILATED_BRANCHES)
    pps = SAMPLE_PAIRS_PER_STEP
    width = pps * PAIR_W
    tok = pl.BlockSpec((t_new, width), lambda n, p: (n, p))
    new = pl.BlockSpec((width, NEW_LANES), lambda n, p: (p, n // (NEW_LANES // t_new)))
    buf = pl.BlockSpec((1, width, buf_len), lambda n, p: (n, p, 0))
    return pl.pallas_call(
        _sample_att_kernel,
        grid=(nb, N_PAIRS // pps),
        in_specs=[tok, new, new, buf, buf,
                  pl.BlockSpec((n_br, pps, 2 * t_new, buf_len + NEW_LANES), lambda n, p: (0, p, 0, 0))],
        out_specs=[buf, buf, tok],
        out_shape=[jax.ShapeDtypeStruct(cache_k_t.shape, F32), jax.ShapeDtypeStruct(cache_v_t.shape, F32),
                   jax.ShapeDtypeStruct((nb * t_new, ATT_WIDTH), BF16)],
        compiler_params=_params(2),
        name="dilated_attention_sample",
    )(q, k_new_t, v_new_t, cache_k_t, cache_v_t, bias)


def _conv_tail(y, cb_ref, lg_ref, lb_ref):
    y = y + cb_ref[...]
    mu = jnp.mean(y, axis=-1, keepdims=True)
    yc = y - mu
    var = jnp.mean(yc * yc, axis=-1, keepdims=True)
    yn = yc * lax.rsqrt(var + EPS) * lg_ref[...] + lb_ref[...]
    return yn * _sigmoid(yn)


def _mix_out(x, att, c, wo_ref):
    aw = att.shape[1]
    return x + _dot(att.astype(BF16), wo_ref[:aw, :]) + _dot(c.astype(BF16), wo_ref[aw:, :])


def _cross_attend(qx, head_kv):
    hd = qx.shape[1] // X_HEADS
    outs = []
    for h in range(X_HEADS):
        mk, mv = head_kv(h)
        s = _dot_nt(qx[:, h * hd:(h + 1) * hd].astype(BF16), mk)
        p = jnp.exp2(s - jnp.max(s, axis=-1, keepdims=True))
        den = jnp.sum(p, axis=-1, keepdims=True)
        outs.append((_dot(p.astype(BF16), mv) / den).astype(BF16))
    return jnp.concatenate(outs, axis=1)


def _prompt_post_kernel(x_ref, att_ref, u_ref, uh_ref, cw_ref, cb_ref, lg_ref, lb_ref, wo_ref,
                        gx_ref, wq_ref, mk_ref, mv_ref, wxo_ref, o_ref, ubuf, *, tiles_per_seq):
    tm = x_ref.shape[0]
    seq_start = (pl.program_id(0) % tiles_per_seq) == 0
    ubuf[:CONV_HALO, :] = jnp.where(seq_start, 0.0, uh_ref[...])
    ubuf[CONV_HALO:, :] = u_ref[...]
    hist = ubuf[...]
    n_rows = hist.shape[0]
    first_tap = CONV_HALO - (CONV_K - 1)
    y = jnp.zeros(u_ref.shape, F32)
    for s in range(F32_SUBLANES):
        shifted = hist if s == 0 else pltpu.roll(hist, n_rows - s, axis=0)
        for j in range(CONV_K):
            if (first_tap + j) % F32_SUBLANES == s:
                lo = first_tap + j - s
                y = y + cw_ref[j:j + 1, :] * shifted[lo:lo + tm]
    c = _conv_tail(y, cb_ref, lg_ref, lb_ref)
    x1 = _mix_out(x_ref[...], att_ref[...], c, wo_ref)
    hd = wq_ref.shape[1] // X_HEADS
    qx = _dot(_rms(x1, gx_ref[...]).astype(BF16), wq_ref[...]) * (hd ** -0.5 * LOG2E)
    head_kv = lambda h: (mk_ref[:, h * hd:(h + 1) * hd], mv_ref[:, h * hd:(h + 1) * hd])
    o_ref[...] = x1 + _dot(_cross_attend(qx, head_kv), wxo_ref[...])


def _prompt_post(x, att, u, conv_w, conv_b, ln_g, ln_b, w_out, gx, w_xq, mk, mv, w_xo, seq, tm):
    n, d = x.shape
    cw = u.shape[1]
    n_mem = mk.shape[0] // (n // seq)
    tiles_per_seq = seq // tm
    row = lambda i: (i, 0)
    fixed = lambda i: (0, 0)
    halo = lambda i: (jnp.maximum(i * (tm // CONV_HALO) - 1, 0), 0)
    per_seq = lambda i: (i // tiles_per_seq, 0)
    full = _resident
    return pl.pallas_call(
        functools.partial(_prompt_post_kernel, tiles_per_seq=tiles_per_seq),
        grid=(n // tm,),
        in_specs=[pl.BlockSpec((tm, d), row), pl.BlockSpec((tm, ATT_WIDTH), row),
                  pl.BlockSpec((tm, cw), row), pl.BlockSpec((CONV_HALO, cw), halo),
                  full(conv_w), full(conv_b), full(ln_g), full(ln_b), full(w_out),
                  full(gx), full(w_xq),
                  pl.BlockSpec((n_mem, d), per_seq), pl.BlockSpec((n_mem, d), per_seq), full(w_xo)],
        out_specs=pl.BlockSpec((tm, d), row),
        out_shape=jax.ShapeDtypeStruct((n, d), F32),
        scratch_shapes=[pltpu.VMEM((CONV_HALO + tm, cw), F32)],
        compiler_params=_params(1),
        name="prompt_mix_cross",
    )(x, att, u, u, conv_w, conv_b, ln_g, ln_b, w_out, gx, w_xq, mk, mv, w_xo)


def _sample_mix_kernel(x_ref, att_ref, uf_ref, cw_ref, cb_ref, lg_ref, lb_ref, wo_ref, gx_ref, wq_ref,
                       x1_ref, qx_ref):
    nb, t_full, cw = uf_ref.shape
    t_new = t_full - (CONV_K - 1)
    y = jnp.zeros((nb, t_new, cw), F32)
    for j in range(CONV_K):
        y = y + cw_ref[j:j + 1, :] * uf_ref[:, j:j + t_new, :]
    c = _conv_tail(y.reshape(nb * t_new, cw), cb_ref, lg_ref, lb_ref)
    x1 = _mix_out(x_ref[...], att_ref[...], c, wo_ref)
    x1_ref[...] = x1
    hd = wq_ref.shape[1] // X_HEADS
    qx_ref[...] = _dot(_rms(x1, gx_ref[...]).astype(BF16), wq_ref[...]) * (hd ** -0.5 * LOG2E)


def _sample_mix(x, att, u_full, conv_w, conv_b, ln_g, ln_b, w_out, gx, w_xq):
    n, d = x.shape
    return pl.pallas_call(
        _sample_mix_kernel,
        out_shape=[jax.ShapeDtypeStruct((n, d), F32), jax.ShapeDtypeStruct((n, w_xq.shape[1]), F32)],
        compiler_params=pltpu.CompilerParams(vmem_limit_bytes=VMEM_LIMIT),
        name="sample_mix",
    )(x, att, u_full, conv_w, conv_b, ln_g, ln_b, w_out, gx, w_xq)


def _sample_cross_kernel(qx_ref, mk_hbm, mv_hbm, o_ref, kbuf, vbuf, sems):
    n = pl.program_id(0)

    def head_copies(seq, slot):
        return [pltpu.make_async_copy(src.at[seq, :, h, :], dst.at[slot, h], sems.at[i, slot, h])
                for i, (src, dst) in enumerate(((mk_hbm, kbuf), (mv_hbm, vbuf))) for h in range(X_HEADS)]

    @pl.when(n == 0)
    def _():
        for cp in head_copies(0, 0):
            cp.start()

    @pl.when(n + 1 < pl.num_programs(0))
    def _():
        for cp in head_copies(n + 1, (n + 1) % 2):
            cp.start()

    slot = n % 2
    for cp in head_copies(n, slot):
        cp.wait()
    head_kv = lambda h: (kbuf[slot, h].astype(BF16), vbuf[slot, h].astype(BF16))
    o_ref[...] = _cross_attend(qx_ref[...], head_kv)


def _sample_cross(qx, mem_k, mem_v, t_new):
    nb, n_mem, n_heads, hd = mem_k.shape
    tok = pl.BlockSpec((t_new, n_heads * hd), lambda n: (n, 0))
    hbm = pl.BlockSpec(memory_space=pl.ANY)
    return pl.pallas_call(
        _sample_cross_kernel,
        grid=(nb,),
        in_specs=[tok, hbm, hbm],
        out_specs=tok,
        out_shape=jax.ShapeDtypeStruct(qx.shape, BF16),
        scratch_shapes=[pltpu.VMEM((2, n_heads, n_mem, hd), F32), pltpu.VMEM((2, n_heads, n_mem, hd), F32),
                        pltpu.SemaphoreType.DMA((2, 2, n_heads))],
        compiler_params=_params(1),
        name="sample_cross",
    )(qx, mem_k, mem_v)


def _swiglu_final(x2, gf_ref, wg_ref, wu_ref, wd_ref, gfin_ref, ff_chunk):
    h = _rms(x2, gf_ref[...]).astype(BF16)
    acc = x2
    for lo in range(0, wg_ref.shape[1], ff_chunk):
        gate = _dot(h, wg_ref[:, lo:lo + ff_chunk])
        up = _dot(h, wu_ref[:, lo:lo + ff_chunk])
        acc = acc + _dot((gate * _sigmoid(gate) * up).astype(BF16), wd_ref[lo:lo + ff_chunk, :])
    return _rms(acc, gfin_ref[...])


def _ffn_kernel(x_ref, gf_ref, wg_ref, wu_ref, wd_ref, gfin_ref, o_ref, *, ff_chunk):
    o_ref[...] = _swiglu_final(x_ref[...], gf_ref, wg_ref, wu_ref, wd_ref, gfin_ref, ff_chunk)


def _sample_tail_kernel(x_ref, xo_ref, wxo_ref, gf_ref, wg_ref, wu_ref, wd_ref, gfin_ref, o_ref, *, ff_chunk):
    x2 = x_ref[...] + _dot(xo_ref[...], wxo_ref[...])
    o_ref[...] = _swiglu_final(x2, gf_ref, wg_ref, wu_ref, wd_ref, gfin_ref, ff_chunk)


FF_CHUNK = 256


def _resident(a):
    return pl.BlockSpec(a.shape, lambda *_: (0,) * a.ndim, pipeline_mode=pl.Buffered(1))


def _ffn(x, gf, wg, wu, wd, gfin, tm):
    n, d = x.shape
    row = lambda i: (i, 0)
    full = _resident
    return pl.pallas_call(
        functools.partial(_ffn_kernel, ff_chunk=FF_CHUNK),
        grid=(n // tm,),
        in_specs=[pl.BlockSpec((tm, d), row), full(gf), full(wg), full(wu), full(wd), full(gfin)],
        out_specs=pl.BlockSpec((tm, d), row),
        out_shape=jax.ShapeDtypeStruct((n, d), F32),
        compiler_params=_params(1),
        name="swiglu_final_norm",
    )(x, gf, wg, wu, wd, gfin)


def _sample_tail(x1, xo, w_xo, gf, wg, wu, wd, gfin):
    return pl.pallas_call(
        functools.partial(_sample_tail_kernel, ff_chunk=FF_CHUNK),
        out_shape=jax.ShapeDtypeStruct(x1.shape, F32),
        compiler_params=pltpu.CompilerParams(vmem_limit_bytes=VMEM_LIMIT),
        name="sample_cross_out_swiglu",
    )(x1, xo, w_xo, gf, wg, wu, wd, gfin)


def _mem_kv_kernel(m_ref, g_ref, wk_ref, wv_ref, k_ref, v_ref, kb_ref, vb_ref):
    h = _rms(m_ref[...], g_ref[...]).astype(BF16)
    k = _dot(h, wk_ref[...])
    v = _dot(h, wv_ref[...])
    k_ref[...] = k
    v_ref[...] = v
    kb_ref[...] = k.astype(BF16)
    vb_ref[...] = v.astype(BF16)


def _mem_kv(mem, g, w_k, w_v, tm):
    n, d = mem.shape
    row = lambda i: (i, 0)
    full = lambda a: pl.BlockSpec(a.shape, lambda i: (0, 0))
    dk = w_k.shape[1]
    return pl.pallas_call(
        _mem_kv_kernel,
        grid=(n // tm,),
        in_specs=[pl.BlockSpec((tm, d), row), full(g), full(w_k), full(w_v)],
        out_specs=[pl.BlockSpec((tm, dk), row)] * 4,
        out_shape=[jax.ShapeDtypeStruct((n, dk), F32)] * 2 + [jax.ShapeDtypeStruct((n, dk), BF16)] * 2,
        compiler_params=_params(1),
        name="mem_kv",
    )(mem, g, w_k, w_v)


ROW_TILE = 512
FFN_ROW_TILE = 1024
ATT_CHUNK = STEPS * DILATED_BRANCHES[-1][1]


def kernel(x_prompt, x_sample, mem_prompt, cache_win_k, cache_win_v, cache_conv, cache_mem_k, cache_mem_v,
           rpb_table, norm_mix_g, w_in, conv_w, conv_b, conv_ln_g, conv_ln_b, w_out, norm_x_g, norm_mem_g,
           w_xq, w_xk, w_xv, w_xo, norm_ffn_g, w_ffn_gate, w_ffn_up, w_ffn_down, norm_final_g):
    depth = w_in.shape[0]
    assert depth == 1, "single-layer stack"
    batch, seq, d = x_prompt.shape
    nb, t_new, _ = x_sample.shape
    buf_len = cache_win_k.shape[2]
    keep_p = min(MAX_DISTANCE, seq)
    n_mem = mem_prompt.shape[1]
    conv_hist = CONV_K - 1
    assert seq % ATT_CHUNK == 0 and keep_p % ATT_CHUNK == 0 and buf_len == MAX_DISTANCE

    row = lambda a: a.reshape(1, -1)
    bf = lambda a: a.astype(BF16)
    l = 0
    w_in_b, w_out_b = bf(w_in[l]), bf(w_out[l])
    w_xq_b, w_xk_b, w_xv_b, w_xo_b = bf(w_xq[l]), bf(w_xk[l]), bf(w_xv[l]), bf(w_xo[l])
    w_g_b, w_u_b, w_d_b = bf(w_ffn_gate[l]), bf(w_ffn_up[l]), bf(w_ffn_down[l])
    g_mix, g_x, g_mem, g_ffn, g_fin = (row(norm_mix_g[l]), row(norm_x_g[l]), row(norm_mem_g[l]),
                                       row(norm_ffn_g[l]), row(norm_final_g))
    cv_w, cv_b, ln_g, ln_b = conv_w[l], row(conv_b[l]), row(conv_ln_g[l]), row(conv_ln_b[l])

    bias_p = _bias_tables(rpb_table, _prompt_bucket_index(), masked_cols=STEPS)
    bias_s = _bias_tables(rpb_table, _sample_bucket_index(buf_len, t_new))[0]

    xp = x_prompt.reshape(batch * seq, d)
    wkv_t_b = bf(jnp.transpose(w_in[l][:, ATT_WIDTH:3 * ATT_WIDTH]))
    q, k, v, u, p_wk_t, p_wv_t = _in_proj_prompt(xp, g_mix, w_in_b, wkv_t_b, seq, keep_p, ROW_TILE)
    att = _dilated_attention_prompt(q, k, v, bias_p, seq // ATT_CHUNK)
    mk, mv, mk_b, mv_b = _mem_kv(mem_prompt.reshape(batch * n_mem, d), g_mem, w_xk_b, w_xv_b, n_mem)
    x2 = _prompt_post(xp, att, u, cv_w, cv_b, ln_g, ln_b, w_out_b, g_x, w_xq_b, mk_b, mv_b, w_xo_b,
                      seq, FFN_ROW_TILE)
    y_prompt = _ffn(x2, g_ffn, w_g_b, w_u_b, w_d_b, g_fin, FFN_ROW_TILE).reshape(batch, seq, d)

    tok_major = lambda a: jnp.transpose(a.reshape(a.shape[0], N_ATT_HEADS, HEAD_DIM, -1), (0, 3, 1, 2))[None]
    p_wk, p_wv = tok_major(p_wk_t), tok_major(p_wv_t)
    u3 = u.reshape(batch, seq, -1)
    p_conv = u3[:, seq - conv_hist:][None]
    xh = lambda a, n: a.reshape(1, n, n_mem, X_HEADS, -1)
    p_mk, p_mv = xh(mk, batch), xh(mv, batch)

    xs = x_sample.reshape(nb * t_new, d)
    qs, us, ks_t, vs_t = _in_proj_sample(xs, g_mix, w_in_b, wkv_t_b)
    chan_major = lambda a: jnp.transpose(a, (0, 2, 3, 1)).reshape(nb, ATT_WIDTH, -1)
    s_wk_t, s_wv_t, att_s = _dilated_attention_sample(
        qs, ks_t, vs_t, chan_major(cache_win_k[l]), chan_major(cache_win_v[l]), bias_s, t_new)
    u_full = jnp.concatenate([cache_conv[l], us.reshape(nb, t_new, -1)], axis=1)
    x1s, qxs = _sample_mix(xs, att_s, u_full, cv_w, cv_b, ln_g, ln_b, w_out_b, g_x, w_xq_b)
    xo_s = _sample_cross(qxs, cache_mem_k[l], cache_mem_v[l], t_new)
    y_sample = _sample_tail(x1s, xo_s, w_xo_b, g_ffn, w_g_b, w_u_b, w_d_b, g_fin).reshape(nb, t_new, d)

    return (y_prompt, y_sample, p_wk, p_wv, p_conv, p_mk, p_mv,
            tok_major(s_wk_t), tok_major(s_wv_t), u_full[:, t_new:][None])
```

```python
import functools
import math

import numpy as np
import jax
import jax.numpy as jnp
from jax import lax
from jax.experimental import pallas as pl
from jax.experimental.pallas import tpu as pltpu

F32 = jnp.float32
BF16 = jnp.bfloat16

HEAD_DIM = 64
N_ATT_HEADS = 12
ATT_WIDTH = N_ATT_HEADS * HEAD_DIM
PAIR_W = 2 * HEAD_DIM
N_PAIRS = N_ATT_HEADS // 2
CONV_K = 31
CONV_HALO = 32
F32_SUBLANES = 8
DILATED_BRANCHES = ((128, 1), (512, 4), (2048, 16))
STEPS = 128
N_CLS = DILATED_BRANCHES[-1][1]
GATHER_STRIDE = 4
NEW_LANES = 128
N_BUCKETS = 32
MAX_EXACT = N_BUCKETS // 2
MAX_DISTANCE = 2048
X_HEADS = 4
EPS = 1e-6
LOG2E = math.log2(math.e)
ATT_SCALE = HEAD_DIM ** -0.5 * LOG2E
MASKED = -1e30

V7X_VMEM_BYTES = 64 * 1024 * 1024
VMEM_LIMIT = V7X_VMEM_BYTES * 3 // 4


def _params(n_grid_dims):
    return pltpu.CompilerParams(dimension_semantics=("arbitrary",) * n_grid_dims,
                                vmem_limit_bytes=VMEM_LIMIT)


def _rms(x, g):
    return x * lax.rsqrt(jnp.mean(x * x, axis=-1, keepdims=True) + EPS) * g


def _sigmoid(x):
    return 1.0 / (1.0 + jnp.exp(-x))


def _dot(a, b):
    return jnp.dot(a, b, preferred_element_type=F32)


def _dot_nt(a, b):
    return lax.dot_general(a, b, (((1,), (1,)), ((), ())), preferred_element_type=F32)


def _t5_bucket_np(dist):
    n = dist.astype(np.int32)
    nf = np.maximum(n, MAX_EXACT).astype(np.float32)
    large = MAX_EXACT + (np.log(nf / np.float32(MAX_EXACT)) / np.float32(math.log(MAX_DISTANCE / MAX_EXACT))
                         * np.float32(N_BUCKETS - MAX_EXACT)).astype(np.int32)
    large = np.minimum(large, N_BUCKETS - 1)
    return np.where(n < MAX_EXACT, n, large)


def _block_order(dil):
    groups = N_CLS // dil
    per = STEPS // groups
    i = np.arange(STEPS)
    return (i % per) * groups + i // per


def _prompt_bucket_index():
    out = []
    for _, dil in DILATED_BRANCHES:
        n = _block_order(dil)
        kj = np.concatenate([n, STEPS + n])[None, :]
        sub = STEPS + n[:, None] - kj
        band = (sub >= 0) & (sub <= STEPS)
        out.append(np.where(band, _t5_bucket_np(dil * np.maximum(sub, 0)), -1))
    return np.stack(out).astype(np.int32)


def _sample_bucket_index(buf_len, t_new):
    col = np.arange(buf_len + NEW_LANES)
    key = np.where(col < buf_len, (col + t_new) % buf_len, col - NEW_LANES + t_new)
    keep = (col < buf_len) | (col >= buf_len + NEW_LANES - t_new)
    i = np.arange(t_new)[:, None]
    dist = buf_len + i - key[None, :]
    out = []
    for window, dil in DILATED_BRANCHES:
        ok = keep[None, :] & (dist >= 0) & (dist % dil == 0) & (dist // dil <= window // dil)
        out.append(np.where(ok, _t5_bucket_np(np.maximum(dist, 0)), -1))
    return np.stack(out).astype(np.int32)


def _bias_kernel(tab_ref, idx_ref, o_ref, *, masked_cols):
    pair = pl.program_id(1)
    idx = idx_ref[0]
    rows = idx.shape[0]
    for half in range(2):
        head = 2 * pair + half
        acc = jnp.full(idx.shape, MASKED, F32)
        for b in range(N_BUCKETS):
            acc = jnp.where(idx == b, tab_ref[b, head] * LOG2E, acc)
        o_ref[0, 0, 0, half * rows:(half + 1) * rows, :] = acc
        if masked_cols:
            col = lax.broadcasted_iota(jnp.int32, idx.shape, 1)
            o_ref[1, 0, 0, half * rows:(half + 1) * rows, :] = jnp.where(col < masked_cols, MASKED, acc)


def _bias_tables(table, idx, masked_cols=0):
    n_var, rows, cols = idx.shape
    n_out = 2 if masked_cols else 1
    return pl.pallas_call(
        functools.partial(_bias_kernel, masked_cols=masked_cols),
        grid=(n_var, N_PAIRS),
        in_specs=[pl.BlockSpec(memory_space=pltpu.SMEM),
                  pl.BlockSpec((1, rows, cols), lambda v, p: (v, 0, 0))],
        out_specs=pl.BlockSpec((n_out, 1, 1, 2 * rows, cols), lambda v, p: (0, v, p, 0, 0)),
        out_shape=jax.ShapeDtypeStruct((n_out, n_var, N_PAIRS, 2 * rows, cols), F32),
        compiler_params=_params(2),
        name="bias_tables",
    )(table, jnp.asarray(idx))


def _in_proj_sample_kernel(x_ref, g_ref, w_ref, wkv_t_ref, q_ref, u_ref, kt_ref, vt_ref):
    h = _rms(x_ref[...], g_ref[...]).astype(BF16)
    aw = q_ref.shape[1]
    cw = u_ref.shape[1]
    q_ref[...] = _dot(h, w_ref[:, 0:aw]) * ATT_SCALE
    a = _dot(h, w_ref[:, 3 * aw:3 * aw + cw])
    gate = _dot(h, w_ref[:, 3 * aw + cw:3 * aw + 2 * cw])
    u_ref[...] = a * _sigmoid(gate)
    kt_ref[...] = _dot_nt(wkv_t_ref[:aw, :], h)
    vt_ref[...] = _dot_nt(wkv_t_ref[aw:, :], h)


def _in_proj_sample(x, g, w_bf16, wkv_t_bf16):
    n, d = x.shape
    aw = ATT_WIDTH
    cw = (w_bf16.shape[1] - 3 * aw) // 2
    new_shape = jax.ShapeDtypeStruct((aw, n), F32)
    return pl.pallas_call(
        _in_proj_sample_kernel,
        out_shape=[jax.ShapeDtypeStruct((n, aw), F32), jax.ShapeDtypeStruct((n, cw), F32), new_shape, new_shape],
        compiler_params=pltpu.CompilerParams(vmem_limit_bytes=VMEM_LIMIT),
        name="in_proj_sample",
    )(x, g, w_bf16, wkv_t_bf16)


def _in_proj_prompt_kernel(x_ref, g_ref, w_ref, wkv_t_ref, q_ref, k_ref, v_ref, u_ref, kt_ref, vt_ref, xs, xs2, *,
                           tiles_per_seq, tail_first):
    tm, d = x_ref.shape
    per = tm // N_CLS
    aw = q_ref.shape[3]
    cw = u_ref.shape[1]
    g = g_ref[...]
    x = x_ref[...]
    hn = _rms(x, g).astype(BF16)
    gate = _dot(hn, w_ref[:, 3 * aw + cw:3 * aw + 2 * cw])
    u_ref[...] = _dot(hn, w_ref[:, 3 * aw:3 * aw + cw]) * _sigmoid(gate)

    quarter = tm // GATHER_STRIDE
    for c in range(d // PAIR_W):
        xs[c] = x[:, c * PAIR_W:(c + 1) * PAIR_W]
    for c in range(d // PAIR_W):
        for res in range(GATHER_STRIDE):
            xs2[c, res * quarter:(res + 1) * quarter, :] = xs[c, pl.ds(res, quarter, stride=GATHER_STRIDE), :]

    def class_rows(c, r):
        start = (r % GATHER_STRIDE) * quarter + r // GATHER_STRIDE
        return xs2[c, pl.ds(start, per, stride=GATHER_STRIDE), :]

    x_cls = jnp.concatenate(
        [jnp.concatenate([class_rows(c, r) for r in range(N_CLS)], axis=0) for c in range(d // PAIR_W)], axis=1)
    h = _rms(x_cls, g).astype(BF16)
    q_ref[0] = (_dot(h, w_ref[:, 0:aw]) * ATT_SCALE).reshape(N_CLS, per, aw)
    k_ref[0] = _dot(h, w_ref[:, aw:2 * aw]).reshape(N_CLS, per, aw)
    v_ref[0] = _dot(h, w_ref[:, 2 * aw:3 * aw]).reshape(N_CLS, per, aw)

    @pl.when(pl.program_id(0) % tiles_per_seq >= tail_first)
    def _():
        kt_ref[0] = _dot_nt(wkv_t_ref[:aw, :], hn)
        vt_ref[0] = _dot_nt(wkv_t_ref[aw:, :], hn)


def _in_proj_prompt(x, g, w_bf16, wkv_t_bf16, seq, keep, tm):
    n, d = x.shape
    aw = ATT_WIDTH
    cw = (w_bf16.shape[1] - 3 * aw) // 2
    chunk = N_CLS * STEPS
    tiles_per_chunk = chunk // tm
    tiles_per_seq = seq // tm
    tail_first = (seq - keep) // tm
    row = lambda i: (i, 0)
    fixed = lambda i: (0, 0)
    cls = pl.BlockSpec((1, N_CLS, tm // N_CLS, aw), lambda i: (i // tiles_per_chunk, 0, i % tiles_per_chunk, 0))
    tail = pl.BlockSpec((1, aw, tm),
                        lambda i: (i // tiles_per_seq, 0, jnp.maximum(i % tiles_per_seq - tail_first, 0)))
    cls_shape = jax.ShapeDtypeStruct((n // chunk, N_CLS, STEPS, aw), F32)
    tail_shape = jax.ShapeDtypeStruct((n // seq, aw, keep), F32)
    return pl.pallas_call(
        functools.partial(_in_proj_prompt_kernel, tiles_per_seq=tiles_per_seq, tail_first=tail_first),
        grid=(n // tm,),
        in_specs=[pl.BlockSpec((tm, d), row), pl.BlockSpec((1, d), fixed),
                  pl.BlockSpec(w_bf16.shape, fixed), pl.BlockSpec(wkv_t_bf16.shape, fixed)],
        out_specs=[cls, cls, cls, pl.BlockSpec((tm, cw), row), tail, tail],
        out_shape=[cls_shape, cls_shape, cls_shape, jax.ShapeDtypeStruct((n, cw), F32), tail_shape, tail_shape],
        scratch_shapes=[pltpu.VMEM((d // PAIR_W, tm, PAIR_W), F32)] * 2,
        compiler_params=_params(1),
        name="in_proj_prompt",
    )(x, g, w_bf16, wkv_t_bf16)


def _stack_heads(x, first):
    zero = jnp.zeros_like(x)
    return jnp.concatenate([jnp.where(first, x, zero), jnp.where(first, zero, x)], axis=0)


def _att_kernel(q_ref, kc_ref, kp_ref, vc_ref, vp_ref, bias_ref, bias0_ref, o_ref, num_s, m_s, l_s, nat_s):
    first = lax.broadcasted_iota(jnp.int32, (STEPS, PAIR_W), 1) < HEAD_DIM
    ones = jnp.ones((2 * STEPS, PAIR_W), BF16)

    def pieces(dil, cls, blk):
        groups = N_CLS // dil
        per = STEPS // groups
        return [(cls + dil * j, pl.ds(blk * per, per)) for j in range(groups)], per

    def load(ref, dil, cls, blk):
        idx, _ = pieces(dil, cls, blk)
        return jnp.concatenate([ref[0, r, rows, :] for r, rows in idx], axis=0)

    def store(ref, g, dil, cls, blk, val):
        idx, per = pieces(dil, cls, blk)
        for j, (r, rows) in enumerate(idx):
            ref[g, r, rows, :] = val[j * per:(j + 1) * per]

    def attend(g, dil, cls, blk, k_prev, v_prev, bias):
        q_st = _stack_heads(load(q_ref, dil, cls, blk), first).astype(BF16)
        k_own = load(kc_ref, dil, cls, blk).astype(BF16)
        v_own = load(vc_ref, dil, cls, blk).astype(BF16)
        kcat = jnp.concatenate([k_prev, k_own], axis=0)
        vcat = jnp.concatenate([v_prev, v_own], axis=0)
        s = _dot_nt(q_st, kcat) + bias
        m = jnp.max(s, axis=-1, keepdims=True)
        p = jnp.exp2(s - m).astype(BF16)
        o = _dot(p, jnp.concatenate([vcat, ones], axis=1))
        store(num_s, g, dil, cls, blk, jnp.where(first, o[:STEPS, :PAIR_W], o[STEPS:, :PAIR_W]))
        store(l_s, g, dil, cls, blk, jnp.where(first, o[:STEPS, PAIR_W:], o[STEPS:, PAIR_W:]))
        store(m_s, g, dil, cls, blk, jnp.where(first, jnp.broadcast_to(m[:STEPS], (STEPS, PAIR_W)),
                                               jnp.broadcast_to(m[STEPS:], (STEPS, PAIR_W))))
        return k_own, v_own

    for g, (_, dil) in enumerate(DILATED_BRANCHES):
        n_blk = N_CLS // dil
        for cls in range(dil):
            kv = attend(g, dil, cls, 0, load(kp_ref, dil, cls, n_blk - 1).astype(BF16),
                        load(vp_ref, dil, cls, n_blk - 1).astype(BF16), bias0_ref[0, g, 0])
            for blk in range(1, n_blk):
                kv = attend(g, dil, cls, blk, *kv, bias_ref[g, 0])

    quarter = N_CLS * STEPS // GATHER_STRIDE
    for r in range(N_CLS):
        m_all = jnp.maximum(jnp.maximum(m_s[0, r], m_s[1, r]), m_s[2, r])
        num = jnp.zeros(m_all.shape, F32)
        den = jnp.zeros(m_all.shape, F32)
        for g in range(len(DILATED_BRANCHES)):
            w = jnp.exp2(m_s[g, r] - m_all)
            num = num + w * num_s[g, r]
            den = den + w * l_s[g, r]
        start = (r % GATHER_STRIDE) * quarter + r // GATHER_STRIDE
        nat_s[pl.ds(start, STEPS, stride=GATHER_STRIDE), :] = num / den
    for res in range(GATHER_STRIDE):
        o_ref[pl.ds(res, quarter, stride=GATHER_STRIDE), :] = nat_s[res * quarter:(res + 1) * quarter, :]


def _dilated_attention_prompt(q, k, v, bias, chunks_per_seq):
    n_chunks = q.shape[0]
    n_br = len(DILATED_BRANCHES)
    cur = lambda b, c, p: (b * chunks_per_seq + c, 0, 0, p)
    prev = lambda b, c, p: (b * chunks_per_seq + jnp.maximum(c - 1, 0), 0, 0, p)
    blk = pl.BlockSpec((1, N_CLS, STEPS, PAIR_W), cur)
    blk_prev = pl.BlockSpec((1, N_CLS, STEPS, PAIR_W), prev)
    return pl.pallas_call(
        _att_kernel,
        grid=(n_chunks // chunks_per_seq, chunks_per_seq, N_PAIRS),
        in_specs=[blk, blk, blk_prev, blk, blk_prev,
                  pl.BlockSpec((n_br, 1, 2 * STEPS, 2 * STEPS), lambda b, c, p: (0, p, 0, 0)),
                  pl.BlockSpec((1, n_br, 1, 2 * STEPS, 2 * STEPS),
                               lambda b, c, p: (jnp.where(c == 0, 1, 0), 0, p, 0, 0))],
        out_specs=pl.BlockSpec((N_CLS * STEPS, PAIR_W), lambda b, c, p: (b * chunks_per_seq + c, p)),
        out_shape=jax.ShapeDtypeStruct((n_chunks * N_CLS * STEPS, ATT_WIDTH), F32),
        scratch_shapes=[pltpu.VMEM((n_br, N_CLS, STEPS, PAIR_W), F32)] * 3 + [pltpu.VMEM((N_CLS * STEPS, PAIR_W), F32)],
        compiler_params=_params(3),
        name="dilated_attention_prompt",
    )(q, k, k, v, v, bias[0], bias)


def _sample_att_kernel(q_ref, kn_ref, vn_ref, ck_ref, cv_ref, bias_ref, ok_ref, ov_ref, o_ref):
    buf_len = ck_ref.shape[2]
    t_new = q_ref.shape[0]
    tail = buf_len - NEW_LANES
    is_new = lax.broadcasted_iota(jnp.int32, (PAIR_W, NEW_LANES), 1) >= NEW_LANES - t_new
    first = lax.broadcasted_iota(jnp.int32, (t_new, PAIR_W), 1) < HEAD_DIM
    col0 = (pl.program_id(0) % (NEW_LANES // t_new)) * t_new
    to_end = (NEW_LANES - t_new - col0) % NEW_LANES

    for j in range(ck_ref.shape[1] // PAIR_W):
        ch = slice(j * PAIR_W, (j + 1) * PAIR_W)

        def shift_in(c_ref, n_ref, o_ref):
            rot = pltpu.roll(c_ref[0, ch, :], buf_len - t_new, axis=1)
            new = jnp.where(is_new, pltpu.roll(n_ref[ch, :], to_end, axis=1), 0.0)
            o_ref[0, ch, :tail] = rot[:, :tail]
            o_ref[0, ch, tail:] = jnp.where(is_new, new, rot[:, tail:])
            return rot.astype(BF16), new.astype(BF16)

        rot_k, new_k = shift_in(ck_ref, kn_ref, ok_ref)
        rot_v, new_v = shift_in(cv_ref, vn_ref, ov_ref)

        q_st = _stack_heads(q_ref[:, ch], first).astype(BF16)
        s = jnp.concatenate([_dot(q_st, rot_k), _dot(q_st, new_k)], axis=1)
        sg = [s + bias_ref[g, j] for g in range(len(DILATED_BRANCHES))]
        m = functools.reduce(jnp.maximum, [jnp.max(x, axis=-1, keepdims=True) for x in sg])
        p = functools.reduce(jnp.add, [jnp.exp2(x - m) for x in sg])
        den = jnp.sum(p, axis=-1, keepdims=True)
        pb = p.astype(BF16)
        o = (_dot_nt(pb[:, :buf_len], rot_v) + _dot_nt(pb[:, buf_len:], new_v)) / den
        o_ref[:, ch] = jnp.where(first, o[:t_new], o[t_new:]).astype(o_ref.dtype)


SAMPLE_PAIRS_PER_STEP = 3


def _dilated_attention_sample(q, k_new_t, v_new_t, cache_k_t, cache_v_t, bias, t_new):
    nb, _, buf_len = cache_k_t.shape
    n_br = len(DILATED_BRANCHES)
    pps = SAMPLE_PAIRS_PER_STEP
    width = pps * PAIR_W
    tok = pl.BlockSpec((t_new, width), lambda n, p: (n, p))
    new = pl.BlockSpec((width, NEW_LANES), lambda n, p: (p, n // (NEW_LANES // t_new)))
    buf = pl.BlockSpec((1, width, buf_len), lambda n, p: (n, p, 0))
    return pl.pallas_call(
        _sample_att_kernel,
        grid=(nb, N_PAIRS // pps),
        in_specs=[tok, new, new, buf, buf,
                  pl.BlockSpec((n_br, pps, 2 * t_new, buf_len + NEW_LANES), lambda n, p: (0, p, 0, 0))],
        out_specs=[buf, buf, tok],
        out_shape=[jax.ShapeDtypeStruct(cache_k_t.shape, F32), jax.ShapeDtypeStruct(cache_v_t.shape, F32),
                   jax.ShapeDtypeStruct((nb * t_new, ATT_WIDTH), BF16)],
        compiler_params=_params(2),
        name="dilated_attention_sample",
    )(q, k_new_t, v_new_t, cache_k_t, cache_v_t, bias)


def _conv_tail(y, cb_ref, lg_ref, lb_ref):
    y = y + cb_ref[...]
    mu = jnp.mean(y, axis=-1, keepdims=True)
    yc = y - mu
    var = jnp.mean(yc * yc, axis=-1, keepdims=True)
    yn = yc * lax.rsqrt(var + EPS) * lg_ref[...] + lb_ref[...]
    return yn * _sigmoid(yn)


def _mix_out(x, att, c, wo_ref):
    aw = att.shape[1]
    return x + _dot(att.astype(BF16), wo_ref[:aw, :]) + _dot(c.astype(BF16), wo_ref[aw:, :])


def _cross_attend(qx, head_kv):
    hd = qx.shape[1] // X_HEADS
    outs = []
    for h in range(X_HEADS):
        mk, mv = head_kv(h)
        s = _dot_nt(qx[:, h * hd:(h + 1) * hd].astype(BF16), mk)
        p = jnp.exp2(s - jnp.max(s, axis=-1, keepdims=True))
        den = jnp.sum(p, axis=-1, keepdims=True)
        outs.append((_dot(p.astype(BF16), mv) / den).astype(BF16))
    return jnp.concatenate(outs, axis=1)


def _prompt_post_kernel(x_ref, att_ref, u_ref, uh_ref, cw_ref, cb_ref, lg_ref, lb_ref, wo_ref,
                        gx_ref, wq_ref, mk_ref, mv_ref, wxo_ref, o_ref, ubuf, *, tiles_per_seq):
    tm = x_ref.shape[0]
    seq_start = (pl.program_id(0) % tiles_per_seq) == 0
    ubuf[:CONV_HALO, :] = jnp.where(seq_start, 0.0, uh_ref[...])
    ubuf[CONV_HALO:, :] = u_ref[...]
    hist = ubuf[...]
    n_rows = hist.shape[0]
    first_tap = CONV_HALO - (CONV_K - 1)
    y = jnp.zeros(u_ref.shape, F32)
    for s in range(F32_SUBLANES):
        shifted = hist if s == 0 else pltpu.roll(hist, n_rows - s, axis=0)
        for j in range(CONV_K):
            if (first_tap + j) % F32_SUBLANES == s:
                lo = first_tap + j - s
                y = y + cw_ref[j:j + 1, :] * shifted[lo:lo + tm]
    c = _conv_tail(y, cb_ref, lg_ref, lb_ref)
    x1 = _mix_out(x_ref[...], att_ref[...], c, wo_ref)
    hd = wq_ref.shape[1] // X_HEADS
    qx = _dot(_rms(x1, gx_ref[...]).astype(BF16), wq_ref[...]) * (hd ** -0.5 * LOG2E)
    head_kv = lambda h: (mk_ref[:, h * hd:(h + 1) * hd], mv_ref[:, h * hd:(h + 1) * hd])
    o_ref[...] = x1 + _dot(_cross_attend(qx, head_kv), wxo_ref[...])


def _prompt_post(x, att, u, conv_w, conv_b, ln_g, ln_b, w_out, gx, w_xq, mk, mv, w_xo, seq, tm):
    n, d = x.shape
    cw = u.shape[1]
    n_mem = mk.shape[0] // (n // seq)
    tiles_per_seq = seq // tm
    row = lambda i: (i, 0)
    fixed = lambda i: (0, 0)
    halo = lambda i: (jnp.maximum(i * (tm // CONV_HALO) - 1, 0), 0)
    per_seq = lambda i: (i // tiles_per_seq, 0)
    full = _resident
    return pl.pallas_call(
        functools.partial(_prompt_post_kernel, tiles_per_seq=tiles_per_seq),
        grid=(n // tm,),
        in_specs=[pl.BlockSpec((tm, d), row), pl.BlockSpec((tm, ATT_WIDTH), row),
                  pl.BlockSpec((tm, cw), row), pl.BlockSpec((CONV_HALO, cw), halo),
                  full(conv_w), full(conv_b), full(ln_g), full(ln_b), full(w_out),
                  full(gx), full(w_xq),
                  pl.BlockSpec((n_mem, d), per_seq), pl.BlockSpec((n_mem, d), per_seq), full(w_xo)],
        out_specs=pl.BlockSpec((tm, d), row),
        out_shape=jax.ShapeDtypeStruct((n, d), F32),
        scratch_shapes=[pltpu.VMEM((CONV_HALO + tm, cw), F32)],
        compiler_params=_params(1),
        name="prompt_mix_cross",
    )(x, att, u, u, conv_w, conv_b, ln_g, ln_b, w_out, gx, w_xq, mk, mv, w_xo)


def _sample_mix_kernel(x_ref, att_ref, uf_ref, cw_ref, cb_ref, lg_ref, lb_ref, wo_ref, gx_ref, wq_ref,
                       x1_ref, qx_ref):
    nb, t_full, cw = uf_ref.shape
    t_new = t_full - (CONV_K - 1)
    y = jnp.zeros((nb, t_new, cw), F32)
    for j in range(CONV_K):
        y = y + cw_ref[j:j + 1, :] * uf_ref[:, j:j + t_new, :]
    c = _conv_tail(y.reshape(nb * t_new, cw), cb_ref, lg_ref, lb_ref)
    x1 = _mix_out(x_ref[...], att_ref[...], c, wo_ref)
    x1_ref[...] = x1
    hd = wq_ref.shape[1] // X_HEADS
    qx_ref[...] = _dot(_rms(x1, gx_ref[...]).astype(BF16), wq_ref[...]) * (hd ** -0.5 * LOG2E)


def _sample_mix(x, att, u_full, conv_w, conv_b, ln_g, ln_b, w_out, gx, w_xq):
    n, d = x.shape
    return pl.pallas_call(
        _sample_mix_kernel,
        out_shape=[jax.ShapeDtypeStruct((n, d), F32), jax.ShapeDtypeStruct((n, w_xq.shape[1]), F32)],
        compiler_params=pltpu.CompilerParams(vmem_limit_bytes=VMEM_LIMIT),
        name="sample_mix",
    )(x, att, u_full, conv_w, conv_b, ln_g, ln_b, w_out, gx, w_xq)


def _sample_cross_kernel(qx_ref, mk_ref, mv_ref, o_ref, *, n_mem):
    n_seq = mk_ref.shape[0]
    t_new = qx_ref.shape[0] // n_seq
    groups = mk_ref.shape[1] // (n_mem * X_HEADS)
    pitch = groups * X_HEADS

    def head_rows(ref, i, h):
        return jnp.concatenate([ref[i, pl.ds(j * X_HEADS + h, n_mem, stride=pitch), :] for j in range(groups)],
                               axis=1).astype(BF16)

    for i in range(n_seq):
        tok = slice(i * t_new, (i + 1) * t_new)
        o_ref[tok, :] = _cross_attend(qx_ref[tok, :],
                                      lambda h, i=i: (head_rows(mk_ref, i, h), head_rows(mv_ref, i, h)))


SAMPLE_CROSS_SEQS_PER_STEP = 4


def _sample_cross(qx, mem_k, mem_v, t_new):
    nb, n_mem, n_heads, hd = mem_k.shape
    sps = SAMPLE_CROSS_SEQS_PER_STEP
    rows = lambda a: jnp.transpose(a.reshape(nb, n_mem, n_heads, hd // PAIR_W, PAIR_W),
                                   (0, 1, 3, 2, 4)).reshape(nb, -1, PAIR_W)
    tok = pl.BlockSpec((sps * t_new, n_heads * hd), lambda n: (n, 0))
    mem = pl.BlockSpec((sps, n_mem * n_heads * hd // PAIR_W, PAIR_W), lambda n: (n, 0, 0))
    return pl.pallas_call(
        functools.partial(_sample_cross_kernel, n_mem=n_mem),
        grid=(nb // sps,),
        in_specs=[tok, mem, mem],
        out_specs=tok,
        out_shape=jax.ShapeDtypeStruct(qx.shape, BF16),
        compiler_params=_params(1),
        name="sample_cross",
    )(qx, rows(mem_k), rows(mem_v))


def _swiglu_final(x2, gf_ref, wg_ref, wu_ref, wd_ref, gfin_ref, ff_chunk):
    h = _rms(x2, gf_ref[...]).astype(BF16)
    acc = x2
    for lo in range(0, wg_ref.shape[1], ff_chunk):
        gate = _dot(h, wg_ref[:, lo:lo + ff_chunk])
        up = _dot(h, wu_ref[:, lo:lo + ff_chunk])
        acc = acc + _dot((gate * _sigmoid(gate) * up).astype(BF16), wd_ref[lo:lo + ff_chunk, :])
    return _rms(acc, gfin_ref[...])


def _ffn_kernel(x_ref, gf_ref, wg_ref, wu_ref, wd_ref, gfin_ref, o_ref, *, ff_chunk):
    o_ref[...] = _swiglu_final(x_ref[...], gf_ref, wg_ref, wu_ref, wd_ref, gfin_ref, ff_chunk)


def _sample_tail_kernel(x_ref, xo_ref, wxo_ref, gf_ref, wg_ref, wu_ref, wd_ref, gfin_ref, o_ref, *, ff_chunk):
    x2 = x_ref[...] + _dot(xo_ref[...], wxo_ref[...])
    o_ref[...] = _swiglu_final(x2, gf_ref, wg_ref, wu_ref, wd_ref, gfin_ref, ff_chunk)


FF_CHUNK = 256


def _resident(a):
    return pl.BlockSpec(a.shape, lambda *_: (0,) * a.ndim, pipeline_mode=pl.Buffered(1))


def _ffn(x, gf, wg, wu, wd, gfin, tm):
    n, d = x.shape
    row = lambda i: (i, 0)
    full = _resident
    return pl.pallas_call(
        functools.partial(_ffn_kernel, ff_chunk=FF_CHUNK),
        grid=(n // tm,),
        in_specs=[pl.BlockSpec((tm, d), row), full(gf), full(wg), full(wu), full(wd), full(gfin)],
        out_specs=pl.BlockSpec((tm, d), row),
        out_shape=jax.ShapeDtypeStruct((n, d), F32),
        compiler_params=_params(1),
        name="swiglu_final_norm",
    )(x, gf, wg, wu, wd, gfin)


def _sample_tail(x1, xo, w_xo, gf, wg, wu, wd, gfin):
    return pl.pallas_call(
        functools.partial(_sample_tail_kernel, ff_chunk=FF_CHUNK),
        out_shape=jax.ShapeDtypeStruct(x1.shape, F32),
        compiler_params=pltpu.CompilerParams(vmem_limit_bytes=VMEM_LIMIT),
        name="sample_cross_out_swiglu",
    )(x1, xo, w_xo, gf, wg, wu, wd, gfin)


def _mem_kv_kernel(m_ref, g_ref, wk_ref, wv_ref, k_ref, v_ref, kb_ref, vb_ref):
    n_mem = m_ref.shape[0]
    hd = wk_ref.shape[1] // X_HEADS
    groups = hd // PAIR_W
    h = _rms(m_ref[...], g_ref[...]).astype(BF16)
    for w_ref, o_ref, ob_ref in ((wk_ref, k_ref, kb_ref), (wv_ref, v_ref, vb_ref)):
        val = _dot(h, w_ref[...])
        ob_ref[...] = val.astype(BF16)
        for head in range(X_HEADS):
            for j in range(groups):
                lanes = slice(head * hd + j * PAIR_W, head * hd + (j + 1) * PAIR_W)
                o_ref[0, pl.ds(j * X_HEADS + head, n_mem, stride=groups * X_HEADS), :] = val[:, lanes]


def _mem_kv(mem, g, w_k, w_v, n_mem):
    n, d = mem.shape
    row = lambda i: (i, 0)
    full = lambda a: pl.BlockSpec(a.shape, lambda i: (0, 0))
    dk = w_k.shape[1]
    rows_spec = pl.BlockSpec((1, n_mem * dk // PAIR_W, PAIR_W), lambda i: (i, 0, 0))
    rows_shape = jax.ShapeDtypeStruct((n // n_mem, n_mem * dk // PAIR_W, PAIR_W), F32)
    return pl.pallas_call(
        _mem_kv_kernel,
        grid=(n // n_mem,),
        in_specs=[pl.BlockSpec((n_mem, d), row), full(g), full(w_k), full(w_v)],
        out_specs=[rows_spec, rows_spec, pl.BlockSpec((n_mem, dk), row), pl.BlockSpec((n_mem, dk), row)],
        out_shape=[rows_shape, rows_shape] + [jax.ShapeDtypeStruct((n, dk), BF16)] * 2,
        compiler_params=_params(1),
        name="mem_kv",
    )(mem, g, w_k, w_v)


ROW_TILE = 512
FFN_ROW_TILE = 1024
ATT_CHUNK = STEPS * DILATED_BRANCHES[-1][1]


def kernel(x_prompt, x_sample, mem_prompt, cache_win_k, cache_win_v, cache_conv, cache_mem_k, cache_mem_v,
           rpb_table, norm_mix_g, w_in, conv_w, conv_b, conv_ln_g, conv_ln_b, w_out, norm_x_g, norm_mem_g,
           w_xq, w_xk, w_xv, w_xo, norm_ffn_g, w_ffn_gate, w_ffn_up, w_ffn_down, norm_final_g):
    depth = w_in.shape[0]
    assert depth == 1, "single-layer stack"
    batch, seq, d = x_prompt.shape
    nb, t_new, _ = x_sample.shape
    buf_len = cache_win_k.shape[2]
    keep_p = min(MAX_DISTANCE, seq)
    n_mem = mem_prompt.shape[1]
    conv_hist = CONV_K - 1
    assert seq % ATT_CHUNK == 0 and keep_p % ATT_CHUNK == 0 and buf_len == MAX_DISTANCE

    row = lambda a: a.reshape(1, -1)
    bf = lambda a: a.astype(BF16)
    l = 0
    w_in_b, w_out_b = bf(w_in[l]), bf(w_out[l])
    w_xq_b, w_xk_b, w_xv_b, w_xo_b = bf(w_xq[l]), bf(w_xk[l]), bf(w_xv[l]), bf(w_xo[l])
    w_g_b, w_u_b, w_d_b = bf(w_ffn_gate[l]), bf(w_ffn_up[l]), bf(w_ffn_down[l])
    g_mix, g_x, g_mem, g_ffn, g_fin = (row(norm_mix_g[l]), row(norm_x_g[l]), row(norm_mem_g[l]),
                                       row(norm_ffn_g[l]), row(norm_final_g))
    cv_w, cv_b, ln_g, ln_b = conv_w[l], row(conv_b[l]), row(conv_ln_g[l]), row(conv_ln_b[l])

    bias_p = _bias_tables(rpb_table, _prompt_bucket_index(), masked_cols=STEPS)
    bias_s = _bias_tables(rpb_table, _sample_bucket_index(buf_len, t_new))[0]

    xp = x_prompt.reshape(batch * seq, d)
    wkv_t_b = bf(jnp.transpose(w_in[l][:, ATT_WIDTH:3 * ATT_WIDTH]))
    q, k, v, u, p_wk_t, p_wv_t = _in_proj_prompt(xp, g_mix, w_in_b, wkv_t_b, seq, keep_p, ROW_TILE)
    att = _dilated_attention_prompt(q, k, v, bias_p, seq // ATT_CHUNK)
    mk, mv, mk_b, mv_b = _mem_kv(mem_prompt.reshape(batch * n_mem, d), g_mem, w_xk_b, w_xv_b, n_mem)
    x2 = _prompt_post(xp, att, u, cv_w, cv_b, ln_g, ln_b, w_out_b, g_x, w_xq_b, mk_b, mv_b, w_xo_b,
                      seq, FFN_ROW_TILE)
    y_prompt = _ffn(x2, g_ffn, w_g_b, w_u_b, w_d_b, g_fin, FFN_ROW_TILE).reshape(batch, seq, d)

    tok_major = lambda a: jnp.transpose(a.reshape(a.shape[0], N_ATT_HEADS, HEAD_DIM, -1), (0, 3, 1, 2))[None]
    p_wk, p_wv = tok_major(p_wk_t), tok_major(p_wv_t)
    u3 = u.reshape(batch, seq, -1)
    p_conv = u3[:, seq - conv_hist:][None]
    xh = lambda a: jnp.transpose(a.reshape(batch, n_mem, -1, X_HEADS, PAIR_W), (0, 1, 3, 2, 4)).reshape(
        1, batch, n_mem, X_HEADS, -1)
    p_mk, p_mv = xh(mk), xh(mv)

    xs = x_sample.reshape(nb * t_new, d)
    qs, us, ks_t, vs_t = _in_proj_sample(xs, g_mix, w_in_b, wkv_t_b)
    chan_major = lambda a: jnp.transpose(a, (0, 2, 3, 1)).reshape(nb, ATT_WIDTH, -1)
    s_wk_t, s_wv_t, att_s = _dilated_attention_sample(
        qs, ks_t, vs_t, chan_major(cache_win_k[l]), chan_major(cache_win_v[l]), bias_s, t_new)
    u_full = jnp.concatenate([cache_conv[l], us.reshape(nb, t_new, -1)], axis=1)
    x1s, qxs = _sample_mix(xs, att_s, u_full, cv_w, cv_b, ln_g, ln_b, w_out_b, g_x, w_xq_b)
    xo_s = _sample_cross(qxs, cache_mem_k[l], cache_mem_v[l], t_new)
    y_sample = _sample_tail(x1s, xo_s, w_xo_b, g_ffn, w_g_b, w_u_b, w_d_b, g_fin).reshape(nb, t_new, d)

    return (y_prompt, y_sample, p_wk, p_wv, p_conv, p_mk, p_mv,
            tok_major(s_wk_t), tok_major(s_wv_t), u_full[:, t_new:][None])
```

```python
import functools
import math

import numpy as np
import jax
import jax.numpy as jnp
from jax import lax
from jax.experimental import pallas as pl
from jax.experimental.pallas import tpu as pltpu

F32 = jnp.float32
BF16 = jnp.bfloat16

HEAD_DIM = 64
N_ATT_HEADS = 12
ATT_WIDTH = N_ATT_HEADS * HEAD_DIM
PAIR_W = 2 * HEAD_DIM
N_PAIRS = N_ATT_HEADS // 2
CONV_K = 31
CONV_HALO = 32
F32_SUBLANES = 8
DILATED_BRANCHES = ((128, 1), (512, 4), (2048, 16))
STEPS = 128
N_CLS = DILATED_BRANCHES[-1][1]
GATHER_STRIDE = 4
NEW_LANES = 128
N_BUCKETS = 32
MAX_EXACT = N_BUCKETS // 2
MAX_DISTANCE = 2048
X_HEADS = 4
EPS = 1e-6
LOG2E = math.log2(math.e)
ATT_SCALE = HEAD_DIM ** -0.5 * LOG2E
MASKED = -1e30

V7X_VMEM_BYTES = 64 * 1024 * 1024
VMEM_LIMIT = V7X_VMEM_BYTES * 3 // 4


def _params(n_grid_dims):
    return pltpu.CompilerParams(dimension_semantics=("arbitrary",) * n_grid_dims,
                                vmem_limit_bytes=VMEM_LIMIT)


def _rms(x, g):
    return x * lax.rsqrt(jnp.mean(x * x, axis=-1, keepdims=True) + EPS) * g


def _sigmoid(x):
    return 1.0 / (1.0 + jnp.exp(-x))


def _dot(a, b):
    return jnp.dot(a, b, preferred_element_type=F32)


def _dot_nt(a, b):
    return lax.dot_general(a, b, (((1,), (1,)), ((), ())), preferred_element_type=F32)


def _t5_bucket_np(dist):
    n = dist.astype(np.int32)
    nf = np.maximum(n, MAX_EXACT).astype(np.float32)
    large = MAX_EXACT + (np.log(nf / np.float32(MAX_EXACT)) / np.float32(math.log(MAX_DISTANCE / MAX_EXACT))
                         * np.float32(N_BUCKETS - MAX_EXACT)).astype(np.int32)
    large = np.minimum(large, N_BUCKETS - 1)
    return np.where(n < MAX_EXACT, n, large)


def _block_order(dil):
    groups = N_CLS // dil
    per = STEPS // groups
    i = np.arange(STEPS)
    return (i % per) * groups + i // per


def _prompt_bucket_index():
    out = []
    for _, dil in DILATED_BRANCHES:
        n = _block_order(dil)
        kj = np.concatenate([n, STEPS + n])[None, :]
        sub = STEPS + n[:, None] - kj
        band = (sub >= 0) & (sub <= STEPS)
        out.append(np.where(band, _t5_bucket_np(dil * np.maximum(sub, 0)), -1))
    return np.stack(out).astype(np.int32)


def _sample_bucket_index(buf_len, t_new):
    col = np.arange(buf_len + NEW_LANES)
    key = np.where(col < buf_len, (col + t_new) % buf_len, col - NEW_LANES + t_new)
    keep = (col < buf_len) | (col >= buf_len + NEW_LANES - t_new)
    i = np.arange(t_new)[:, None]
    dist = buf_len + i - key[None, :]
    out = []
    for window, dil in DILATED_BRANCHES:
        ok = keep[None, :] & (dist >= 0) & (dist % dil == 0) & (dist // dil <= window // dil)
        out.append(np.where(ok, _t5_bucket_np(np.maximum(dist, 0)), -1))
    return np.stack(out).astype(np.int32)


def _bias_kernel(tab_ref, idx_ref, o_ref, *, masked_cols, buckets):
    pair = pl.program_id(0)
    for g, present in enumerate(buckets):
        idx = idx_ref[g]
        rows = idx.shape[0]
        for half in range(2):
            head = 2 * pair + half
            acc = jnp.full(idx.shape, MASKED, F32)
            for b in present:
                acc = jnp.where(idx == b, tab_ref[b, head] * LOG2E, acc)
            o_ref[0, g, 0, half * rows:(half + 1) * rows, :] = acc
            if masked_cols:
                col = lax.broadcasted_iota(jnp.int32, idx.shape, 1)
                o_ref[1, g, 0, half * rows:(half + 1) * rows, :] = jnp.where(col < masked_cols, MASKED, acc)


def _bias_tables(table, idx, masked_cols=0):
    n_var, rows, cols = idx.shape
    n_out = 2 if masked_cols else 1
    buckets = tuple(tuple(int(b) for b in np.unique(idx[g]) if b >= 0) for g in range(n_var))
    return pl.pallas_call(
        functools.partial(_bias_kernel, masked_cols=masked_cols, buckets=buckets),
        grid=(N_PAIRS,),
        in_specs=[pl.BlockSpec(memory_space=pltpu.SMEM),
                  pl.BlockSpec((n_var, rows, cols), lambda p: (0, 0, 0))],
        out_specs=pl.BlockSpec((n_out, n_var, 1, 2 * rows, cols), lambda p: (0, 0, p, 0, 0)),
        out_shape=jax.ShapeDtypeStruct((n_out, n_var, N_PAIRS, 2 * rows, cols), F32),
        compiler_params=_params(1),
        name="bias_tables",
    )(table, jnp.asarray(idx))


def _in_proj_sample_kernel(x_ref, g_ref, w_ref, q_ref, u_ref, kt_ref, vt_ref):
    h = _rms(x_ref[...], g_ref[...]).astype(BF16)
    aw = q_ref.shape[1]
    cw = u_ref.shape[1]
    q_ref[...] = _dot(h, w_ref[:, 0:aw]) * ATT_SCALE
    a = _dot(h, w_ref[:, 3 * aw:3 * aw + cw])
    gate = _dot(h, w_ref[:, 3 * aw + cw:3 * aw + 2 * cw])
    u_ref[...] = a * _sigmoid(gate)
    kt_ref[...] = _dot(h, w_ref[:, aw:2 * aw]).T
    vt_ref[...] = _dot(h, w_ref[:, 2 * aw:3 * aw]).T


def _in_proj_sample(x, g, w_bf16):
    n, d = x.shape
    aw = ATT_WIDTH
    cw = (w_bf16.shape[1] - 3 * aw) // 2
    new_shape = jax.ShapeDtypeStruct((aw, n), F32)
    return pl.pallas_call(
        _in_proj_sample_kernel,
        out_shape=[jax.ShapeDtypeStruct((n, aw), F32), jax.ShapeDtypeStruct((n, cw), F32), new_shape, new_shape],
        compiler_params=pltpu.CompilerParams(vmem_limit_bytes=VMEM_LIMIT),
        name="in_proj_sample",
    )(x, g, w_bf16)


def _in_proj_prompt_kernel(x_ref, g_ref, w_ref, q_ref, k_ref, v_ref, u_ref, kt_ref, vt_ref, xs, xs2, *,
                           tiles_per_seq, tail_first):
    tm, d = x_ref.shape
    per = tm // N_CLS
    aw = q_ref.shape[3]
    cw = u_ref.shape[1]
    g = g_ref[...]
    x = x_ref[...]
    hn = _rms(x, g).astype(BF16)
    gate = _dot(hn, w_ref[:, 3 * aw + cw:3 * aw + 2 * cw])
    u_ref[...] = _dot(hn, w_ref[:, 3 * aw:3 * aw + cw]) * _sigmoid(gate)

    quarter = tm // GATHER_STRIDE
    for c in range(d // PAIR_W):
        xs[c] = x[:, c * PAIR_W:(c + 1) * PAIR_W]
    for c in range(d // PAIR_W):
        for res in range(GATHER_STRIDE):
            xs2[c, res * quarter:(res + 1) * quarter, :] = xs[c, pl.ds(res, quarter, stride=GATHER_STRIDE), :]

    def class_rows(c, r):
        start = (r % GATHER_STRIDE) * quarter + r // GATHER_STRIDE
        return xs2[c, pl.ds(start, per, stride=GATHER_STRIDE), :]

    x_cls = jnp.concatenate(
        [jnp.concatenate([class_rows(c, r) for r in range(N_CLS)], axis=0) for c in range(d // PAIR_W)], axis=1)
    h = _rms(x_cls, g).astype(BF16)
    q_ref[0] = (_dot(h, w_ref[:, 0:aw]) * ATT_SCALE).reshape(N_CLS, per, aw)
    k_ref[0] = _dot(h, w_ref[:, aw:2 * aw]).reshape(N_CLS, per, aw)
    v_ref[0] = _dot(h, w_ref[:, 2 * aw:3 * aw]).reshape(N_CLS, per, aw)

    @pl.when(pl.program_id(0) % tiles_per_seq >= tail_first)
    def _():
        kt_ref[0] = _dot(hn, w_ref[:, aw:2 * aw]).T
        vt_ref[0] = _dot(hn, w_ref[:, 2 * aw:3 * aw]).T


def _in_proj_prompt(x, g, w_bf16, seq, keep, tm):
    n, d = x.shape
    aw = ATT_WIDTH
    cw = (w_bf16.shape[1] - 3 * aw) // 2
    chunk = N_CLS * STEPS
    tiles_per_chunk = chunk // tm
    tiles_per_seq = seq // tm
    tail_first = (seq - keep) // tm
    row = lambda i: (i, 0)
    fixed = lambda i: (0, 0)
    cls = pl.BlockSpec((1, N_CLS, tm // N_CLS, aw), lambda i: (i // tiles_per_chunk, 0, i % tiles_per_chunk, 0))
    tail = pl.BlockSpec((1, aw, tm),
                        lambda i: (i // tiles_per_seq, 0, jnp.maximum(i % tiles_per_seq - tail_first, 0)))
    cls_shape = jax.ShapeDtypeStruct((n // chunk, N_CLS, STEPS, aw), F32)
    tail_shape = jax.ShapeDtypeStruct((n // seq, aw, keep), F32)
    return pl.pallas_call(
        functools.partial(_in_proj_prompt_kernel, tiles_per_seq=tiles_per_seq, tail_first=tail_first),
        grid=(n // tm,),
        in_specs=[pl.BlockSpec((tm, d), row), pl.BlockSpec((1, d), fixed),
                  pl.BlockSpec(w_bf16.shape, fixed)],
        out_specs=[cls, cls, cls, pl.BlockSpec((tm, cw), row), tail, tail],
        out_shape=[cls_shape, cls_shape, cls_shape, jax.ShapeDtypeStruct((n, cw), F32), tail_shape, tail_shape],
        scratch_shapes=[pltpu.VMEM((d // PAIR_W, tm, PAIR_W), F32)] * 2,
        compiler_params=_params(1),
        name="in_proj_prompt",
    )(x, g, w_bf16)


def _stack_heads(x, first):
    zero = jnp.zeros_like(x)
    return jnp.concatenate([jnp.where(first, x, zero), jnp.where(first, zero, x)], axis=0)


def _att_kernel(q_ref, kc_ref, kp_ref, vc_ref, vp_ref, bias_ref, bias0_ref, o_ref, num_s, m_s, l_s, nat_s):
    first = lax.broadcasted_iota(jnp.int32, (STEPS, PAIR_W), 1) < HEAD_DIM
    ones = jnp.ones((2 * STEPS, PAIR_W), BF16)

    def pieces(dil, cls, blk):
        groups = N_CLS // dil
        per = STEPS // groups
        return [(cls + dil * j, pl.ds(blk * per, per)) for j in range(groups)], per

    def load(ref, dil, cls, blk):
        idx, _ = pieces(dil, cls, blk)
        return jnp.concatenate([ref[0, r, rows, :] for r, rows in idx], axis=0)

    def store(ref, g, dil, cls, blk, val):
        idx, per = pieces(dil, cls, blk)
        for j, (r, rows) in enumerate(idx):
            ref[g, r, rows, :] = val[j * per:(j + 1) * per]

    def attend(g, dil, cls, blk, k_prev, v_prev, bias):
        q_st = _stack_heads(load(q_ref, dil, cls, blk), first).astype(BF16)
        k_own = load(kc_ref, dil, cls, blk).astype(BF16)
        v_own = load(vc_ref, dil, cls, blk).astype(BF16)
        kcat = jnp.concatenate([k_prev, k_own], axis=0)
        vcat = jnp.concatenate([v_prev, v_own], axis=0)
        s = _dot_nt(q_st, kcat) + bias
        m = jnp.max(s, axis=-1, keepdims=True)
        p = jnp.exp2(s - m).astype(BF16)
        o = _dot(p, jnp.concatenate([vcat, ones], axis=1))
        store(num_s, g, dil, cls, blk, jnp.where(first, o[:STEPS, :PAIR_W], o[STEPS:, :PAIR_W]))
        store(l_s, g, dil, cls, blk, jnp.where(first, o[:STEPS, PAIR_W:], o[STEPS:, PAIR_W:]))
        store(m_s, g, dil, cls, blk, jnp.where(first, jnp.broadcast_to(m[:STEPS], (STEPS, PAIR_W)),
                                               jnp.broadcast_to(m[STEPS:], (STEPS, PAIR_W))))
        return k_own, v_own

    for g, (_, dil) in enumerate(DILATED_BRANCHES):
        n_blk = N_CLS // dil
        for cls in range(dil):
            kv = attend(g, dil, cls, 0, load(kp_ref, dil, cls, n_blk - 1).astype(BF16),
                        load(vp_ref, dil, cls, n_blk - 1).astype(BF16), bias0_ref[0, g, 0])
            for blk in range(1, n_blk):
                kv = attend(g, dil, cls, blk, *kv, bias_ref[g, 0])

    quarter = N_CLS * STEPS // GATHER_STRIDE
    for r in range(N_CLS):
        m_all = jnp.maximum(jnp.maximum(m_s[0, r], m_s[1, r]), m_s[2, r])
        num = jnp.zeros(m_all.shape, F32)
        den = jnp.zeros(m_all.shape, F32)
        for g in range(len(DILATED_BRANCHES)):
            w = jnp.exp2(m_s[g, r] - m_all)
            num = num + w * num_s[g, r]
            den = den + w * l_s[g, r]
        start = (r % GATHER_STRIDE) * quarter + r // GATHER_STRIDE
        nat_s[pl.ds(start, STEPS, stride=GATHER_STRIDE), :] = num / den
    for res in range(GATHER_STRIDE):
        o_ref[pl.ds(res, quarter, stride=GATHER_STRIDE), :] = nat_s[res * quarter:(res + 1) * quarter, :]


def _dilated_attention_prompt(q, k, v, bias, chunks_per_seq):
    n_chunks = q.shape[0]
    n_br = len(DILATED_BRANCHES)
    cur = lambda b, c, p: (b * chunks_per_seq + c, 0, 0, p)
    prev = lambda b, c, p: (b * chunks_per_seq + jnp.maximum(c - 1, 0), 0, 0, p)
    blk = pl.BlockSpec((1, N_CLS, STEPS, PAIR_W), cur)
    blk_prev = pl.BlockSpec((1, N_CLS, STEPS, PAIR_W), prev)
    return pl.pallas_call(
        _att_kernel,
        grid=(n_chunks // chunks_per_seq, chunks_per_seq, N_PAIRS),
        in_specs=[blk, blk, blk_prev, blk, blk_prev,
                  pl.BlockSpec((n_br, 1, 2 * STEPS, 2 * STEPS), lambda b, c, p: (0, p, 0, 0)),
                  pl.BlockSpec((1, n_br, 1, 2 * STEPS, 2 * STEPS),
                               lambda b, c, p: (jnp.where(c == 0, 1, 0), 0, p, 0, 0))],
        out_specs=pl.BlockSpec((N_CLS * STEPS, PAIR_W), lambda b, c, p: (b * chunks_per_seq + c, p)),
        out_shape=jax.ShapeDtypeStruct((n_chunks * N_CLS * STEPS, ATT_WIDTH), F32),
        scratch_shapes=[pltpu.VMEM((n_br, N_CLS, STEPS, PAIR_W), F32)] * 3 + [pltpu.VMEM((N_CLS * STEPS, PAIR_W), F32)],
        compiler_params=_params(3),
        name="dilated_attention_prompt",
    )(q, k, k, v, v, bias[0], bias)


def _sample_att_kernel(q_ref, kn_ref, vn_ref, ck_ref, cv_ref, bias_ref, ok_ref, ov_ref, o_ref):
    buf_len = ck_ref.shape[2]
    t_new = q_ref.shape[0]
    tail = buf_len - NEW_LANES
    is_new = lax.broadcasted_iota(jnp.int32, (PAIR_W, NEW_LANES), 1) >= NEW_LANES - t_new
    first = lax.broadcasted_iota(jnp.int32, (t_new, PAIR_W), 1) < HEAD_DIM
    col0 = (pl.program_id(0) % (NEW_LANES // t_new)) * t_new
    to_end = (NEW_LANES - t_new - col0) % NEW_LANES

    for j in range(ck_ref.shape[1] // PAIR_W):
        ch = slice(j * PAIR_W, (j + 1) * PAIR_W)

        def shift_in(c_ref, n_ref, o_ref):
            rot = pltpu.roll(c_ref[0, ch, :], buf_len - t_new, axis=1)
            new = jnp.where(is_new, pltpu.roll(n_ref[ch, :], to_end, axis=1), 0.0)
            o_ref[0, ch, :tail] = rot[:, :tail]
            o_ref[0, ch, tail:] = jnp.where(is_new, new, rot[:, tail:])
            return rot.astype(BF16), new.astype(BF16)

        rot_k, new_k = shift_in(ck_ref, kn_ref, ok_ref)
        rot_v, new_v = shift_in(cv_ref, vn_ref, ov_ref)

        q_st = _stack_heads(q_ref[:, ch], first).astype(BF16)
        s = jnp.concatenate([_dot(q_st, rot_k), _dot(q_st, new_k)], axis=1)
        sg = [s + bias_ref[g, j] for g in range(len(DILATED_BRANCHES))]
        m = functools.reduce(jnp.maximum, [jnp.max(x, axis=-1, keepdims=True) for x in sg])
        p = functools.reduce(jnp.add, [jnp.exp2(x - m) for x in sg])
        den = jnp.sum(p, axis=-1, keepdims=True)
        pb = p.astype(BF16)
        o = (_dot_nt(pb[:, :buf_len], rot_v) + _dot_nt(pb[:, buf_len:], new_v)) / den
        o_ref[:, ch] = jnp.where(first, o[:t_new], o[t_new:]).astype(o_ref.dtype)


SAMPLE_PAIRS_PER_STEP = 3


def _dilated_attention_sample(q, k_new_t, v_new_t, cache_k_t, cache_v_t, bias, t_new):
    nb, _, buf_len = cache_k_t.shape
    n_br = len(DILATED_BRANCHES)
    pps = SAMPLE_PAIRS_PER_STEP
    width = pps * PAIR_W
    tok = pl.BlockSpec((t_new, width), lambda n, p: (n, p))
    new = pl.BlockSpec((width, NEW_LANES), lambda n, p: (p, n // (NEW_LANES // t_new)))
    buf = pl.BlockSpec((1, width, buf_len), lambda n, p: (n, p, 0))
    return pl.pallas_call(
        _sample_att_kernel,
        grid=(nb, N_PAIRS // pps),
        in_specs=[tok, new, new, buf, buf,
                  pl.BlockSpec((n_br, pps, 2 * t_new, buf_len + NEW_LANES), lambda n, p: (0, p, 0, 0))],
        out_specs=[buf, buf, tok],
        out_shape=[jax.ShapeDtypeStruct(cache_k_t.shape, F32), jax.ShapeDtypeStruct(cache_v_t.shape, F32),
                   jax.ShapeDtypeStruct((nb * t_new, ATT_WIDTH), BF16)],
        compiler_params=_params(2),
        name="dilated_attention_sample",
    )(q, k_new_t, v_new_t, cache_k_t, cache_v_t, bias)


def _conv_tail(y, cb_ref, lg_ref, lb_ref):
    y = y + cb_ref[...]
    mu = jnp.mean(y, axis=-1, keepdims=True)
    yc = y - mu
    var = jnp.mean(yc * yc, axis=-1, keepdims=True)
    yn = yc * lax.rsqrt(var + EPS) * lg_ref[...] + lb_ref[...]
    return yn * _sigmoid(yn)


def _mix_out(x, att, c, wo_ref):
    aw = att.shape[1]
    return x + _dot(att.astype(BF16), wo_ref[:aw, :]) + _dot(c.astype(BF16), wo_ref[aw:, :])


def _cross_attend(qx, head_kv):
    hd = qx.shape[1] // X_HEADS
    outs = []
    for h in range(X_HEADS):
        mk, mv = head_kv(h)
        s = _dot_nt(qx[:, h * hd:(h + 1) * hd].astype(BF16), mk)
        p = jnp.exp2(s - jnp.max(s, axis=-1, keepdims=True))
        den = jnp.sum(p, axis=-1, keepdims=True)
        outs.append((_dot(p.astype(BF16), mv) / den).astype(BF16))
    return jnp.concatenate(outs, axis=1)


def _prompt_post_kernel(x_ref, att_ref, u_ref, uh_ref, cw_ref, cb_ref, lg_ref, lb_ref, wo_ref,
                        gx_ref, wq_ref, mk_ref, mv_ref, wxo_ref, o_ref, ubuf, *, tiles_per_seq):
    tm = x_ref.shape[0]
    seq_start = (pl.program_id(0) % tiles_per_seq) == 0
    ubuf[:CONV_HALO, :] = jnp.where(seq_start, 0.0, uh_ref[...])
    ubuf[CONV_HALO:, :] = u_ref[...]
    hist = ubuf[...]
    n_rows = hist.shape[0]
    first_tap = CONV_HALO - (CONV_K - 1)
    y = jnp.zeros(u_ref.shape, F32)
    for s in range(F32_SUBLANES):
        shifted = hist if s == 0 else pltpu.roll(hist, n_rows - s, axis=0)
        for j in range(CONV_K):
            if (first_tap + j) % F32_SUBLANES == s:
                lo = first_tap + j - s
                y = y + cw_ref[j:j + 1, :] * shifted[lo:lo + tm]
    c = _conv_tail(y, cb_ref, lg_ref, lb_ref)
    x1 = _mix_out(x_ref[...], att_ref[...], c, wo_ref)
    hd = wq_ref.shape[1] // X_HEADS
    qx = _dot(_rms(x1, gx_ref[...]).astype(BF16), wq_ref[...]) * (hd ** -0.5 * LOG2E)
    head_kv = lambda h: (mk_ref[:, h * hd:(h + 1) * hd], mv_ref[:, h * hd:(h + 1) * hd])
    o_ref[...] = x1 + _dot(_cross_attend(qx, head_kv), wxo_ref[...])


def _prompt_post(x, att, u, conv_w, conv_b, ln_g, ln_b, w_out, gx, w_xq, mk, mv, w_xo, seq, tm):
    n, d = x.shape
    cw = u.shape[1]
    n_mem = mk.shape[0] // (n // seq)
    tiles_per_seq = seq // tm
    row = lambda i: (i, 0)
    fixed = lambda i: (0, 0)
    halo = lambda i: (jnp.maximum(i * (tm // CONV_HALO) - 1, 0), 0)
    per_seq = lambda i: (i // tiles_per_seq, 0)
    full = _resident
    return pl.pallas_call(
        functools.partial(_prompt_post_kernel, tiles_per_seq=tiles_per_seq),
        grid=(n // tm,),
        in_specs=[pl.BlockSpec((tm, d), row), pl.BlockSpec((tm, ATT_WIDTH), row),
                  pl.BlockSpec((tm, cw), row), pl.BlockSpec((CONV_HALO, cw), halo),
                  full(conv_w), full(conv_b), full(ln_g), full(ln_b), full(w_out),
                  full(gx), full(w_xq),
                  pl.BlockSpec((n_mem, d), per_seq), pl.BlockSpec((n_mem, d), per_seq), full(w_xo)],
        out_specs=pl.BlockSpec((tm, d), row),
        out_shape=jax.ShapeDtypeStruct((n, d), F32),
        scratch_shapes=[pltpu.VMEM((CONV_HALO + tm, cw), F32)],
        compiler_params=_params(1),
        name="prompt_mix_cross",
    )(x, att, u, u, conv_w, conv_b, ln_g, ln_b, w_out, gx, w_xq, mk, mv, w_xo)


def _sample_mix_kernel(x_ref, att_ref, uf_ref, cw_ref, cb_ref, lg_ref, lb_ref, wo_ref, gx_ref, wq_ref,
                       x1_ref, qx_ref):
    nb, t_full, cw = uf_ref.shape
    t_new = t_full - (CONV_K - 1)
    y = jnp.zeros((nb, t_new, cw), F32)
    for j in range(CONV_K):
        y = y + cw_ref[j:j + 1, :] * uf_ref[:, j:j + t_new, :]
    c = _conv_tail(y.reshape(nb * t_new, cw), cb_ref, lg_ref, lb_ref)
    x1 = _mix_out(x_ref[...], att_ref[...], c, wo_ref)
    x1_ref[...] = x1
    hd = wq_ref.shape[1] // X_HEADS
    qx_ref[...] = _dot(_rms(x1, gx_ref[...]).astype(BF16), wq_ref[...]) * (hd ** -0.5 * LOG2E)


def _sample_mix(x, att, u_full, conv_w, conv_b, ln_g, ln_b, w_out, gx, w_xq):
    n, d = x.shape
    return pl.pallas_call(
        _sample_mix_kernel,
        out_shape=[jax.ShapeDtypeStruct((n, d), F32), jax.ShapeDtypeStruct((n, w_xq.shape[1]), F32)],
        compiler_params=pltpu.CompilerParams(vmem_limit_bytes=VMEM_LIMIT),
        name="sample_mix",
    )(x, att, u_full, conv_w, conv_b, ln_g, ln_b, w_out, gx, w_xq)


def _sample_cross_kernel(qx_ref, mk_ref, mv_ref, o_ref, *, n_mem):
    n_seq = mk_ref.shape[0]
    t_new = qx_ref.shape[0] // n_seq
    groups = mk_ref.shape[1] // (n_mem * X_HEADS)
    pitch = groups * X_HEADS

    def head_rows(ref, i, h):
        return jnp.concatenate([ref[i, pl.ds(j * X_HEADS + h, n_mem, stride=pitch), :] for j in range(groups)],
                               axis=1).astype(BF16)

    for i in range(n_seq):
        tok = slice(i * t_new, (i + 1) * t_new)
        o_ref[tok, :] = _cross_attend(qx_ref[tok, :],
                                      lambda h, i=i: (head_rows(mk_ref, i, h), head_rows(mv_ref, i, h)))


SAMPLE_CROSS_SEQS_PER_STEP = 4


def _sample_cross(qx, mem_k, mem_v, t_new):
    nb, n_mem, n_heads, hd = mem_k.shape
    sps = SAMPLE_CROSS_SEQS_PER_STEP
    rows = lambda a: jnp.transpose(a.reshape(nb, n_mem, n_heads, hd // PAIR_W, PAIR_W),
                                   (0, 1, 3, 2, 4)).reshape(nb, -1, PAIR_W)
    tok = pl.BlockSpec((sps * t_new, n_heads * hd), lambda n: (n, 0))
    mem = pl.BlockSpec((sps, n_mem * n_heads * hd // PAIR_W, PAIR_W), lambda n: (n, 0, 0))
    return pl.pallas_call(
        functools.partial(_sample_cross_kernel, n_mem=n_mem),
        grid=(nb // sps,),
        in_specs=[tok, mem, mem],
        out_specs=tok,
        out_shape=jax.ShapeDtypeStruct(qx.shape, BF16),
        compiler_params=_params(1),
        name="sample_cross",
    )(qx, rows(mem_k), rows(mem_v))


def _swiglu_final(x2, gf_ref, wg_ref, wu_ref, wd_ref, gfin_ref, ff_chunk):
    h = _rms(x2, gf_ref[...]).astype(BF16)
    acc = x2
    for lo in range(0, wg_ref.shape[1], ff_chunk):
        gate = _dot(h, wg_ref[:, lo:lo + ff_chunk])
        up = _dot(h, wu_ref[:, lo:lo + ff_chunk])
        acc = acc + _dot((gate * _sigmoid(gate) * up).astype(BF16), wd_ref[lo:lo + ff_chunk, :])
    return _rms(acc, gfin_ref[...])


def _ffn_kernel(x_ref, gf_ref, wg_ref, wu_ref, wd_ref, gfin_ref, o_ref, *, ff_chunk):
    o_ref[...] = _swiglu_final(x_ref[...], gf_ref, wg_ref, wu_ref, wd_ref, gfin_ref, ff_chunk)


def _sample_tail_kernel(x_ref, xo_ref, wxo_ref, gf_ref, wg_ref, wu_ref, wd_ref, gfin_ref, o_ref, *, ff_chunk):
    x2 = x_ref[...] + _dot(xo_ref[...], wxo_ref[...])
    o_ref[...] = _swiglu_final(x2, gf_ref, wg_ref, wu_ref, wd_ref, gfin_ref, ff_chunk)


FF_CHUNK = 256


def _resident(a):
    return pl.BlockSpec(a.shape, lambda *_: (0,) * a.ndim, pipeline_mode=pl.Buffered(1))


def _ffn(x, gf, wg, wu, wd, gfin, tm):
    n, d = x.shape
    row = lambda i: (i, 0)
    full = _resident
    return pl.pallas_call(
        functools.partial(_ffn_kernel, ff_chunk=FF_CHUNK),
        grid=(n // tm,),
        in_specs=[pl.BlockSpec((tm, d), row), full(gf), full(wg), full(wu), full(wd), full(gfin)],
        out_specs=pl.BlockSpec((tm, d), row),
        out_shape=jax.ShapeDtypeStruct((n, d), F32),
        compiler_params=_params(1),
        name="swiglu_final_norm",
    )(x, gf, wg, wu, wd, gfin)


def _sample_tail(x1, xo, w_xo, gf, wg, wu, wd, gfin):
    return pl.pallas_call(
        functools.partial(_sample_tail_kernel, ff_chunk=FF_CHUNK),
        out_shape=jax.ShapeDtypeStruct(x1.shape, F32),
        compiler_params=pltpu.CompilerParams(vmem_limit_bytes=VMEM_LIMIT),
        name="sample_cross_out_swiglu",
    )(x1, xo, w_xo, gf, wg, wu, wd, gfin)


def _mem_kv_kernel(m_ref, g_ref, wk_ref, wv_ref, k_ref, v_ref, kb_ref, vb_ref):
    n_mem = m_ref.shape[0]
    hd = wk_ref.shape[1] // X_HEADS
    groups = hd // PAIR_W
    h = _rms(m_ref[...], g_ref[...]).astype(BF16)
    for w_ref, o_ref, ob_ref in ((wk_ref, k_ref, kb_ref), (wv_ref, v_ref, vb_ref)):
        val = _dot(h, w_ref[...])
        ob_ref[...] = val.astype(BF16)
        for head in range(X_HEADS):
            for j in range(groups):
                lanes = slice(head * hd + j * PAIR_W, head * hd + (j + 1) * PAIR_W)
                o_ref[0, pl.ds(j * X_HEADS + head, n_mem, stride=groups * X_HEADS), :] = val[:, lanes]


def _mem_kv(mem, g, w_k, w_v, n_mem):
    n, d = mem.shape
    row = lambda i: (i, 0)
    full = lambda a: pl.BlockSpec(a.shape, lambda i: (0, 0))
    dk = w_k.shape[1]
    rows_spec = pl.BlockSpec((1, n_mem * dk // PAIR_W, PAIR_W), lambda i: (i, 0, 0))
    rows_shape = jax.ShapeDtypeStruct((n // n_mem, n_mem * dk // PAIR_W, PAIR_W), F32)
    return pl.pallas_call(
        _mem_kv_kernel,
        grid=(n // n_mem,),
        in_specs=[pl.BlockSpec((n_mem, d), row), full(g), full(w_k), full(w_v)],
        out_specs=[rows_spec, rows_spec, pl.BlockSpec((n_mem, dk), row), pl.BlockSpec((n_mem, dk), row)],
        out_shape=[rows_shape, rows_shape] + [jax.ShapeDtypeStruct((n, dk), BF16)] * 2,
        compiler_params=_params(1),
        name="mem_kv",
    )(mem, g, w_k, w_v)


ROW_TILE = 512
FFN_ROW_TILE = 1024
ATT_CHUNK = STEPS * DILATED_BRANCHES[-1][1]


def kernel(x_prompt, x_sample, mem_prompt, cache_win_k, cache_win_v, cache_conv, cache_mem_k, cache_mem_v,
           rpb_table, norm_mix_g, w_in, conv_w, conv_b, conv_ln_g, conv_ln_b, w_out, norm_x_g, norm_mem_g,
           w_xq, w_xk, w_xv, w_xo, norm_ffn_g, w_ffn_gate, w_ffn_up, w_ffn_down, norm_final_g):
    depth = w_in.shape[0]
    assert depth == 1, "single-layer stack"
    batch, seq, d = x_prompt.shape
    nb, t_new, _ = x_sample.shape
    buf_len = cache_win_k.shape[2]
    keep_p = min(MAX_DISTANCE, seq)
    n_mem = mem_prompt.shape[1]
    conv_hist = CONV_K - 1
    assert seq % ATT_CHUNK == 0 and keep_p % ATT_CHUNK == 0 and buf_len == MAX_DISTANCE

    row = lambda a: a.reshape(1, -1)
    bf = lambda a: a.astype(BF16)
    l = 0
    w_in_b, w_out_b = bf(w_in[l]), bf(w_out[l])
    w_xq_b, w_xk_b, w_xv_b, w_xo_b = bf(w_xq[l]), bf(w_xk[l]), bf(w_xv[l]), bf(w_xo[l])
    w_g_b, w_u_b, w_d_b = bf(w_ffn_gate[l]), bf(w_ffn_up[l]), bf(w_ffn_down[l])
    g_mix, g_x, g_mem, g_ffn, g_fin = (row(norm_mix_g[l]), row(norm_x_g[l]), row(norm_mem_g[l]),
                                       row(norm_ffn_g[l]), row(norm_final_g))
    cv_w, cv_b, ln_g, ln_b = conv_w[l], row(conv_b[l]), row(conv_ln_g[l]), row(conv_ln_b[l])

    bias_p = _bias_tables(rpb_table, _prompt_bucket_index(), masked_cols=STEPS)
    bias_s = _bias_tables(rpb_table, _sample_bucket_index(buf_len, t_new))[0]

    xp = x_prompt.reshape(batch * seq, d)
    q, k, v, u, p_wk_t, p_wv_t = _in_proj_prompt(xp, g_mix, w_in_b, seq, keep_p, ROW_TILE)
    att = _dilated_attention_prompt(q, k, v, bias_p, seq // ATT_CHUNK)
    mk, mv, mk_b, mv_b = _mem_kv(mem_prompt.reshape(batch * n_mem, d), g_mem, w_xk_b, w_xv_b, n_mem)
    x2 = _prompt_post(xp, att, u, cv_w, cv_b, ln_g, ln_b, w_out_b, g_x, w_xq_b, mk_b, mv_b, w_xo_b,
                      seq, FFN_ROW_TILE)
    y_prompt = _ffn(x2, g_ffn, w_g_b, w_u_b, w_d_b, g_fin, FFN_ROW_TILE).reshape(batch, seq, d)

    tok_major = lambda a: jnp.transpose(a.reshape(a.shape[0], N_ATT_HEADS, HEAD_DIM, -1), (0, 3, 1, 2))[None]
    p_wk, p_wv = tok_major(p_wk_t), tok_major(p_wv_t)
    u3 = u.reshape(batch, seq, -1)
    p_conv = u3[:, seq - conv_hist:][None]
    xh = lambda a: jnp.transpose(a.reshape(batch, n_mem, -1, X_HEADS, PAIR_W), (0, 1, 3, 2, 4)).reshape(
        1, batch, n_mem, X_HEADS, -1)
    p_mk, p_mv = xh(mk), xh(mv)

    xs = x_sample.reshape(nb * t_new, d)
    qs, us, ks_t, vs_t = _in_proj_sample(xs, g_mix, w_in_b)
    chan_major = lambda a: jnp.transpose(a, (0, 2, 3, 1)).reshape(nb, ATT_WIDTH, -1)
    s_wk_t, s_wv_t, att_s = _dilated_attention_sample(
        qs, ks_t, vs_t, chan_major(cache_win_k[l]), chan_major(cache_win_v[l]), bias_s, t_new)
    u_full = jnp.concatenate([cache_conv[l], us.reshape(nb, t_new, -1)], axis=1)
    x1s, qxs = _sample_mix(xs, att_s, u_full, cv_w, cv_b, ln_g, ln_b, w_out_b, g_x, w_xq_b)
    xo_s = _sample_cross(qxs, cache_mem_k[l], cache_mem_v[l], t_new)
    y_sample = _sample_tail(x1s, xo_s, w_xo_b, g_ffn, w_g_b, w_u_b, w_d_b, g_fin).reshape(nb, t_new, d)

    return (y_prompt, y_sample, p_wk, p_wv, p_conv, p_mk, p_mv,
            tok_major(s_wk_t), tok_major(s_wv_t), u_full[:, t_new:][None])
```

```python
import functools
import math

import numpy as np
import jax
import jax.numpy as jnp
from jax import lax
from jax.experimental import pallas as pl
from jax.experimental.pallas import tpu as pltpu

F32 = jnp.float32
BF16 = jnp.bfloat16

HEAD_DIM = 64
N_ATT_HEADS = 12
ATT_WIDTH = N_ATT_HEADS * HEAD_DIM
PAIR_W = 2 * HEAD_DIM
N_PAIRS = N_ATT_HEADS // 2
CONV_K = 31
CONV_HALO = 32
F32_SUBLANES = 8
DILATED_BRANCHES = ((128, 1), (512, 4), (2048, 16))
STEPS = 128
N_CLS = DILATED_BRANCHES[-1][1]
GATHER_STRIDE = 4
NEW_LANES = 128
N_BUCKETS = 32
MAX_EXACT = N_BUCKETS // 2
MAX_DISTANCE = 2048
X_HEADS = 4
EPS = 1e-6
LOG2E = math.log2(math.e)
ATT_SCALE = HEAD_DIM ** -0.5 * LOG2E
MASKED = -1e30

V7X_VMEM_BYTES = 64 * 1024 * 1024
VMEM_LIMIT = V7X_VMEM_BYTES * 3 // 4


def _params(n_grid_dims):
    return pltpu.CompilerParams(dimension_semantics=("arbitrary",) * n_grid_dims,
                                vmem_limit_bytes=VMEM_LIMIT)


def _rms(x, g):
    return x * lax.rsqrt(jnp.mean(x * x, axis=-1, keepdims=True) + EPS) * g


def _sigmoid(x):
    return 1.0 / (1.0 + jnp.exp(-x))


def _dot(a, b):
    return jnp.dot(a, b, preferred_element_type=F32)


def _dot_nt(a, b):
    return lax.dot_general(a, b, (((1,), (1,)), ((), ())), preferred_element_type=F32)


def _t5_bucket_np(dist):
    n = dist.astype(np.int32)
    nf = np.maximum(n, MAX_EXACT).astype(np.float32)
    large = MAX_EXACT + (np.log(nf / np.float32(MAX_EXACT)) / np.float32(math.log(MAX_DISTANCE / MAX_EXACT))
                         * np.float32(N_BUCKETS - MAX_EXACT)).astype(np.int32)
    large = np.minimum(large, N_BUCKETS - 1)
    return np.where(n < MAX_EXACT, n, large)


def _block_order(dil):
    groups = N_CLS // dil
    per = STEPS // groups
    i = np.arange(STEPS)
    return (i % per) * groups + i // per


def _prompt_bucket_index():
    out = []
    for _, dil in DILATED_BRANCHES:
        n = _block_order(dil)
        kj = np.concatenate([n, STEPS + n])[None, :]
        sub = STEPS + n[:, None] - kj
        band = (sub >= 0) & (sub <= STEPS)
        out.append(np.where(band, _t5_bucket_np(dil * np.maximum(sub, 0)), -1))
    return np.stack(out).astype(np.int32)


def _sample_bucket_index(buf_len, t_new):
    col = np.arange(buf_len + NEW_LANES)
    key = np.where(col < buf_len, (col + t_new) % buf_len, col - NEW_LANES + t_new)
    keep = (col < buf_len) | (col >= buf_len + NEW_LANES - t_new)
    i = np.arange(t_new)[:, None]
    dist = buf_len + i - key[None, :]
    out = []
    for window, dil in DILATED_BRANCHES:
        ok = keep[None, :] & (dist >= 0) & (dist % dil == 0) & (dist // dil <= window // dil)
        out.append(np.where(ok, _t5_bucket_np(np.maximum(dist, 0)), -1))
    return np.stack(out).astype(np.int32)


def _bias_kernel(tab_ref, idx_ref, o_ref, *, masked_cols, buckets):
    pair = pl.program_id(0)
    for g, present in enumerate(buckets):
        idx = idx_ref[g]
        rows = idx.shape[0]
        for half in range(2):
            head = 2 * pair + half
            acc = jnp.full(idx.shape, MASKED, F32)
            for b in present:
                acc = jnp.where(idx == b, tab_ref[b, head] * LOG2E, acc)
            o_ref[0, g, 0, half * rows:(half + 1) * rows, :] = acc
            if masked_cols:
                col = lax.broadcasted_iota(jnp.int32, idx.shape, 1)
                o_ref[1, g, 0, half * rows:(half + 1) * rows, :] = jnp.where(col < masked_cols, MASKED, acc)


def _bias_tables(table, idx, masked_cols=0):
    n_var, rows, cols = idx.shape
    n_out = 2 if masked_cols else 1
    buckets = tuple(tuple(int(b) for b in np.unique(idx[g]) if b >= 0) for g in range(n_var))
    return pl.pallas_call(
        functools.partial(_bias_kernel, masked_cols=masked_cols, buckets=buckets),
        grid=(N_PAIRS,),
        in_specs=[pl.BlockSpec(memory_space=pltpu.SMEM),
                  pl.BlockSpec((n_var, rows, cols), lambda p: (0, 0, 0))],
        out_specs=pl.BlockSpec((n_out, n_var, 1, 2 * rows, cols), lambda p: (0, 0, p, 0, 0)),
        out_shape=jax.ShapeDtypeStruct((n_out, n_var, N_PAIRS, 2 * rows, cols), F32),
        compiler_params=_params(1),
        name="bias_tables",
    )(table, jnp.asarray(idx))


def _in_proj_sample_kernel(x_ref, g_ref, w_ref, q_ref, u_ref, kt_ref, vt_ref):
    h = _rms(x_ref[...], g_ref[...]).astype(BF16)
    aw = q_ref.shape[1]
    cw = u_ref.shape[1]
    q_ref[...] = _dot(h, w_ref[:, 0:aw]) * ATT_SCALE
    a = _dot(h, w_ref[:, 3 * aw:3 * aw + cw])
    gate = _dot(h, w_ref[:, 3 * aw + cw:3 * aw + 2 * cw])
    u_ref[...] = a * _sigmoid(gate)
    kt_ref[...] = _dot(h, w_ref[:, aw:2 * aw]).T
    vt_ref[...] = _dot(h, w_ref[:, 2 * aw:3 * aw]).T


def _in_proj_sample(x, g, w_bf16):
    n, d = x.shape
    aw = ATT_WIDTH
    cw = (w_bf16.shape[1] - 3 * aw) // 2
    new_shape = jax.ShapeDtypeStruct((aw, n), F32)
    return pl.pallas_call(
        _in_proj_sample_kernel,
        out_shape=[jax.ShapeDtypeStruct((n, aw), F32), jax.ShapeDtypeStruct((n, cw), F32), new_shape, new_shape],
        compiler_params=pltpu.CompilerParams(vmem_limit_bytes=VMEM_LIMIT),
        name="in_proj_sample",
    )(x, g, w_bf16)


def _in_proj_prompt_kernel(x_ref, g_ref, w_ref, q_ref, k_ref, v_ref, u_ref, kt_ref, vt_ref, xs, xs2, *,
                           tiles_per_seq, tail_first):
    tm, d = x_ref.shape
    per = tm // N_CLS
    aw = q_ref.shape[3]
    cw = u_ref.shape[1]
    g = g_ref[...]
    x = x_ref[...]
    hn = _rms(x, g).astype(BF16)
    gate = _dot(hn, w_ref[:, 3 * aw + cw:3 * aw + 2 * cw])
    u_ref[...] = _dot(hn, w_ref[:, 3 * aw:3 * aw + cw]) * _sigmoid(gate)

    quarter = tm // GATHER_STRIDE
    for c in range(d // PAIR_W):
        xs[c] = x[:, c * PAIR_W:(c + 1) * PAIR_W]
    for c in range(d // PAIR_W):
        for res in range(GATHER_STRIDE):
            xs2[c, res * quarter:(res + 1) * quarter, :] = xs[c, pl.ds(res, quarter, stride=GATHER_STRIDE), :]

    def class_rows(c, r):
        start = (r % GATHER_STRIDE) * quarter + r // GATHER_STRIDE
        return xs2[c, pl.ds(start, per, stride=GATHER_STRIDE), :]

    x_cls = jnp.concatenate(
        [jnp.concatenate([class_rows(c, r) for r in range(N_CLS)], axis=0) for c in range(d // PAIR_W)], axis=1)
    h = _rms(x_cls, g).astype(BF16)
    q_ref[0] = (_dot(h, w_ref[:, 0:aw]) * ATT_SCALE).reshape(N_CLS, per, aw)
    k_ref[0] = _dot(h, w_ref[:, aw:2 * aw]).reshape(N_CLS, per, aw)
    v_ref[0] = _dot(h, w_ref[:, 2 * aw:3 * aw]).reshape(N_CLS, per, aw)

    @pl.when(pl.program_id(0) % tiles_per_seq >= tail_first)
    def _():
        kt_ref[0] = _dot(hn, w_ref[:, aw:2 * aw]).T
        vt_ref[0] = _dot(hn, w_ref[:, 2 * aw:3 * aw]).T


def _in_proj_prompt(x, g, w_bf16, seq, keep, tm):
    n, d = x.shape
    aw = ATT_WIDTH
    cw = (w_bf16.shape[1] - 3 * aw) // 2
    chunk = N_CLS * STEPS
    tiles_per_chunk = chunk // tm
    tiles_per_seq = seq // tm
    tail_first = (seq - keep) // tm
    row = lambda i: (i, 0)
    fixed = lambda i: (0, 0)
    cls = pl.BlockSpec((1, N_CLS, tm // N_CLS, aw), lambda i: (i // tiles_per_chunk, 0, i % tiles_per_chunk, 0))
    tail = pl.BlockSpec((1, aw, tm),
                        lambda i: (i // tiles_per_seq, 0, jnp.maximum(i % tiles_per_seq - tail_first, 0)))
    cls_shape = jax.ShapeDtypeStruct((n // chunk, N_CLS, STEPS, aw), F32)
    tail_shape = jax.ShapeDtypeStruct((n // seq, aw, keep), F32)
    return pl.pallas_call(
        functools.partial(_in_proj_prompt_kernel, tiles_per_seq=tiles_per_seq, tail_first=tail_first),
        grid=(n // tm,),
        in_specs=[pl.BlockSpec((tm, d), row), pl.BlockSpec((1, d), fixed),
                  pl.BlockSpec(w_bf16.shape, fixed)],
        out_specs=[cls, cls, cls, pl.BlockSpec((tm, cw), row), tail, tail],
        out_shape=[cls_shape, cls_shape, cls_shape, jax.ShapeDtypeStruct((n, cw), F32), tail_shape, tail_shape],
        scratch_shapes=[pltpu.VMEM((d // PAIR_W, tm, PAIR_W), F32)] * 2,
        compiler_params=_params(1),
        name="in_proj_prompt",
    )(x, g, w_bf16)


def _stack_heads(x, first):
    zero = jnp.zeros_like(x)
    return jnp.concatenate([jnp.where(first, x, zero), jnp.where(first, zero, x)], axis=0)


def _att_kernel(q_ref, kc_ref, kp_ref, vc_ref, vp_ref, bias_ref, bias0_ref, o_ref, num_s, m_s, l_s, nat_s):
    first = lax.broadcasted_iota(jnp.int32, (STEPS, PAIR_W), 1) < HEAD_DIM
    ones = jnp.ones((2 * STEPS, PAIR_W), BF16)

    def pieces(dil, cls, blk):
        groups = N_CLS // dil
        per = STEPS // groups
        return [(cls + dil * j, pl.ds(blk * per, per)) for j in range(groups)], per

    def load(ref, dil, cls, blk):
        idx, _ = pieces(dil, cls, blk)
        return jnp.concatenate([ref[0, r, rows, :] for r, rows in idx], axis=0)

    def store(ref, g, dil, cls, blk, val):
        idx, per = pieces(dil, cls, blk)
        for j, (r, rows) in enumerate(idx):
            ref[g, r, rows, :] = val[j * per:(j + 1) * per]

    def attend(g, dil, cls, blk, k_prev, v_prev, bias):
        q_st = _stack_heads(load(q_ref, dil, cls, blk), first).astype(BF16)
        k_own = load(kc_ref, dil, cls, blk).astype(BF16)
        v_own = load(vc_ref, dil, cls, blk).astype(BF16)
        kcat = jnp.concatenate([k_prev, k_own], axis=0)
        vcat = jnp.concatenate([v_prev, v_own], axis=0)
        s = _dot_nt(q_st, kcat) + bias
        m = jnp.max(s, axis=-1, keepdims=True)
        p = jnp.exp2(s - m).astype(BF16)
        o = _dot(p, jnp.concatenate([vcat, ones], axis=1))
        store(num_s, g, dil, cls, blk, jnp.where(first, o[:STEPS, :PAIR_W], o[STEPS:, :PAIR_W]))
        store(l_s, g, dil, cls, blk, jnp.where(first, o[:STEPS, PAIR_W:], o[STEPS:, PAIR_W:]))
        store(m_s, g, dil, cls, blk, jnp.where(first, jnp.broadcast_to(m[:STEPS], (STEPS, PAIR_W)),
                                               jnp.broadcast_to(m[STEPS:], (STEPS, PAIR_W))))
        return k_own, v_own

    carried = {}
    for i in range(N_CLS):
        for g, (_, dil) in reversed(list(enumerate(DILATED_BRANCHES))):
            n_blk = N_CLS // dil
            cls, blk = i % dil, i // dil
            if blk == 0:
                prev = (load(kp_ref, dil, cls, n_blk - 1).astype(BF16), load(vp_ref, dil, cls, n_blk - 1).astype(BF16))
                bias = bias0_ref[0, g, 0]
            else:
                prev, bias = carried[g, cls], bias_ref[g, 0]
            carried[g, cls] = attend(g, dil, cls, blk, *prev, bias)

    quarter = N_CLS * STEPS // GATHER_STRIDE
    for r in range(N_CLS):
        m_all = jnp.maximum(jnp.maximum(m_s[0, r], m_s[1, r]), m_s[2, r])
        num = jnp.zeros(m_all.shape, F32)
        den = jnp.zeros(m_all.shape, F32)
        for g in range(len(DILATED_BRANCHES)):
            w = jnp.exp2(m_s[g, r] - m_all)
            num = num + w * num_s[g, r]
            den = den + w * l_s[g, r]
        start = (r % GATHER_STRIDE) * quarter + r // GATHER_STRIDE
        nat_s[pl.ds(start, STEPS, stride=GATHER_STRIDE), :] = num / den
    for res in range(GATHER_STRIDE):
        o_ref[pl.ds(res, quarter, stride=GATHER_STRIDE), :] = nat_s[res * quarter:(res + 1) * quarter, :]


def _dilated_attention_prompt(q, k, v, bias, chunks_per_seq):
    n_chunks = q.shape[0]
    n_br = len(DILATED_BRANCHES)
    cur = lambda b, c, p: (b * chunks_per_seq + c, 0, 0, p)
    prev = lambda b, c, p: (b * chunks_per_seq + jnp.maximum(c - 1, 0), 0, 0, p)
    blk = pl.BlockSpec((1, N_CLS, STEPS, PAIR_W), cur)
    blk_prev = pl.BlockSpec((1, N_CLS, STEPS, PAIR_W), prev)
    return pl.pallas_call(
        _att_kernel,
        grid=(n_chunks // chunks_per_seq, chunks_per_seq, N_PAIRS),
        in_specs=[blk, blk, blk_prev, blk, blk_prev,
                  pl.BlockSpec((n_br, 1, 2 * STEPS, 2 * STEPS), lambda b, c, p: (0, p, 0, 0)),
                  pl.BlockSpec((1, n_br, 1, 2 * STEPS, 2 * STEPS),
                               lambda b, c, p: (jnp.where(c == 0, 1, 0), 0, p, 0, 0))],
        out_specs=pl.BlockSpec((N_CLS * STEPS, PAIR_W), lambda b, c, p: (b * chunks_per_seq + c, p)),
        out_shape=jax.ShapeDtypeStruct((n_chunks * N_CLS * STEPS, ATT_WIDTH), F32),
        scratch_shapes=[pltpu.VMEM((n_br, N_CLS, STEPS, PAIR_W), F32)] * 3 + [pltpu.VMEM((N_CLS * STEPS, PAIR_W), F32)],
        compiler_params=_params(3),
        name="dilated_attention_prompt",
    )(q, k, k, v, v, bias[0], bias)


def _sample_att_kernel(q_ref, kn_ref, vn_ref, ck_ref, cv_ref, bias_ref, ok_ref, ov_ref, o_ref):
    buf_len = ck_ref.shape[2]
    t_new = q_ref.shape[0]
    tail = buf_len - NEW_LANES
    is_new = lax.broadcasted_iota(jnp.int32, (PAIR_W, NEW_LANES), 1) >= NEW_LANES - t_new
    first = lax.broadcasted_iota(jnp.int32, (t_new, PAIR_W), 1) < HEAD_DIM
    col0 = (pl.program_id(0) % (NEW_LANES // t_new)) * t_new
    to_end = (NEW_LANES - t_new - col0) % NEW_LANES

    for j in range(ck_ref.shape[1] // PAIR_W):
        ch = slice(j * PAIR_W, (j + 1) * PAIR_W)

        def shift_in(c_ref, n_ref, o_ref):
            rot = pltpu.roll(c_ref[0, ch, :], buf_len - t_new, axis=1)
            new = jnp.where(is_new, pltpu.roll(n_ref[ch, :], to_end, axis=1), 0.0)
            o_ref[0, ch, :tail] = rot[:, :tail]
            o_ref[0, ch, tail:] = jnp.where(is_new, new, rot[:, tail:])
            return rot.astype(BF16), new.astype(BF16)

        rot_k, new_k = shift_in(ck_ref, kn_ref, ok_ref)
        rot_v, new_v = shift_in(cv_ref, vn_ref, ov_ref)

        q_st = _stack_heads(q_ref[:, ch], first).astype(BF16)
        s = jnp.concatenate([_dot(q_st, rot_k), _dot(q_st, new_k)], axis=1)
        sg = [s + bias_ref[g, j] for g in range(len(DILATED_BRANCHES))]
        m = functools.reduce(jnp.maximum, [jnp.max(x, axis=-1, keepdims=True) for x in sg])
        p = functools.reduce(jnp.add, [jnp.exp2(x - m) for x in sg])
        den = jnp.sum(p, axis=-1, keepdims=True)
        pb = p.astype(BF16)
        o = (_dot_nt(pb[:, :buf_len], rot_v) + _dot_nt(pb[:, buf_len:], new_v)) / den
        o_ref[:, ch] = jnp.where(first, o[:t_new], o[t_new:]).astype(o_ref.dtype)


SAMPLE_PAIRS_PER_STEP = 3


def _dilated_attention_sample(q, k_new_t, v_new_t, cache_k_t, cache_v_t, bias, t_new):
    nb, _, buf_len = cache_k_t.shape
    n_br = len(DILATED_BRANCHES)
    pps = SAMPLE_PAIRS_PER_STEP
    width = pps * PAIR_W
    tok = pl.BlockSpec((t_new, width), lambda n, p: (n, p))
    new = pl.BlockSpec((width, NEW_LANES), lambda n, p: (p, n // (NEW_LANES // t_new)))
    buf = pl.BlockSpec((1, width, buf_len), lambda n, p: (n, p, 0))
    return pl.pallas_call(
        _sample_att_kernel,
        grid=(nb, N_PAIRS // pps),
        in_specs=[tok, new, new, buf, buf,
                  pl.BlockSpec((n_br, pps, 2 * t_new, buf_len + NEW_LANES), lambda n, p: (0, p, 0, 0))],
        out_specs=[buf, buf, tok],
        out_shape=[jax.ShapeDtypeStruct(cache_k_t.shape, F32), jax.ShapeDtypeStruct(cache_v_t.shape, F32),
                   jax.ShapeDtypeStruct((nb * t_new, ATT_WIDTH), BF16)],
        compiler_params=_params(2),
        name="dilated_attention_sample",
    )(q, k_new_t, v_new_t, cache_k_t, cache_v_t, bias)


def _conv_tail(y, cb_ref, lg_ref, lb_ref):
    y = y + cb_ref[...]
    mu = jnp.mean(y, axis=-1, keepdims=True)
    yc = y - mu
    var = jnp.mean(yc * yc, axis=-1, keepdims=True)
    yn = yc * lax.rsqrt(var + EPS) * lg_ref[...] + lb_ref[...]
    return yn * _sigmoid(yn)


def _mix_out(x, att, c, wo_ref):
    aw = att.shape[1]
    return x + _dot(att.astype(BF16), wo_ref[:aw, :]) + _dot(c.astype(BF16), wo_ref[aw:, :])


def _cross_attend(qx, head_kv):
    hd = qx.shape[1] // X_HEADS
    outs = []
    for h in range(X_HEADS):
        mk, mv = head_kv(h)
        s = _dot_nt(qx[:, h * hd:(h + 1) * hd].astype(BF16), mk)
        p = jnp.exp2(s - jnp.max(s, axis=-1, keepdims=True))
        den = jnp.sum(p, axis=-1, keepdims=True)
        outs.append((_dot(p.astype(BF16), mv) / den).astype(BF16))
    return jnp.concatenate(outs, axis=1)


def _prompt_post_kernel(x_ref, att_ref, u_ref, uh_ref, cw_ref, cb_ref, lg_ref, lb_ref, wo_ref,
                        gx_ref, wq_ref, mk_ref, mv_ref, wxo_ref, o_ref, ubuf, *, tiles_per_seq):
    tm = x_ref.shape[0]
    seq_start = (pl.program_id(0) % tiles_per_seq) == 0
    ubuf[:CONV_HALO, :] = jnp.where(seq_start, 0.0, uh_ref[...])
    ubuf[CONV_HALO:, :] = u_ref[...]
    hist = ubuf[...]
    n_rows = hist.shape[0]
    first_tap = CONV_HALO - (CONV_K - 1)
    y = jnp.zeros(u_ref.shape, F32)
    for s in range(F32_SUBLANES):
        shifted = hist if s == 0 else pltpu.roll(hist, n_rows - s, axis=0)
        for j in range(CONV_K):
            if (first_tap + j) % F32_SUBLANES == s:
                lo = first_tap + j - s
                y = y + cw_ref[j:j + 1, :] * shifted[lo:lo + tm]
    c = _conv_tail(y, cb_ref, lg_ref, lb_ref)
    x1 = _mix_out(x_ref[...], att_ref[...], c, wo_ref)
    hd = wq_ref.shape[1] // X_HEADS
    qx = _dot(_rms(x1, gx_ref[...]).astype(BF16), wq_ref[...]) * (hd ** -0.5 * LOG2E)
    head_kv = lambda h: (mk_ref[:, h * hd:(h + 1) * hd], mv_ref[:, h * hd:(h + 1) * hd])
    o_ref[...] = x1 + _dot(_cross_attend(qx, head_kv), wxo_ref[...])


def _prompt_post(x, att, u, conv_w, conv_b, ln_g, ln_b, w_out, gx, w_xq, mk, mv, w_xo, seq, tm):
    n, d = x.shape
    cw = u.shape[1]
    n_mem = mk.shape[0] // (n // seq)
    tiles_per_seq = seq // tm
    row = lambda i: (i, 0)
    fixed = lambda i: (0, 0)
    halo = lambda i: (jnp.maximum(i * (tm // CONV_HALO) - 1, 0), 0)
    per_seq = lambda i: (i // tiles_per_seq, 0)
    full = _resident
    return pl.pallas_call(
        functools.partial(_prompt_post_kernel, tiles_per_seq=tiles_per_seq),
        grid=(n // tm,),
        in_specs=[pl.BlockSpec((tm, d), row), pl.BlockSpec((tm, ATT_WIDTH), row),
                  pl.BlockSpec((tm, cw), row), pl.BlockSpec((CONV_HALO, cw), halo),
                  full(conv_w), full(conv_b), full(ln_g), full(ln_b), full(w_out),
                  full(gx), full(w_xq),
                  pl.BlockSpec((n_mem, d), per_seq), pl.BlockSpec((n_mem, d), per_seq), full(w_xo)],
        out_specs=pl.BlockSpec((tm, d), row),
        out_shape=jax.ShapeDtypeStruct((n, d), F32),
        scratch_shapes=[pltpu.VMEM((CONV_HALO + tm, cw), F32)],
        compiler_params=_params(1),
        name="prompt_mix_cross",
    )(x, att, u, u, conv_w, conv_b, ln_g, ln_b, w_out, gx, w_xq, mk, mv, w_xo)


def _sample_mix_kernel(x_ref, att_ref, uf_ref, cw_ref, cb_ref, lg_ref, lb_ref, wo_ref, gx_ref, wq_ref,
                       x1_ref, qx_ref):
    nb, t_full, cw = uf_ref.shape
    t_new = t_full - (CONV_K - 1)
    y = jnp.zeros((nb, t_new, cw), F32)
    for j in range(CONV_K):
        y = y + cw_ref[j:j + 1, :] * uf_ref[:, j:j + t_new, :]
    c = _conv_tail(y.reshape(nb * t_new, cw), cb_ref, lg_ref, lb_ref)
    x1 = _mix_out(x_ref[...], att_ref[...], c, wo_ref)
    x1_ref[...] = x1
    hd = wq_ref.shape[1] // X_HEADS
    qx_ref[...] = _dot(_rms(x1, gx_ref[...]).astype(BF16), wq_ref[...]) * (hd ** -0.5 * LOG2E)


def _sample_mix(x, att, u_full, conv_w, conv_b, ln_g, ln_b, w_out, gx, w_xq):
    n, d = x.shape
    return pl.pallas_call(
        _sample_mix_kernel,
        out_shape=[jax.ShapeDtypeStruct((n, d), F32), jax.ShapeDtypeStruct((n, w_xq.shape[1]), F32)],
        compiler_params=pltpu.CompilerParams(vmem_limit_bytes=VMEM_LIMIT),
        name="sample_mix",
    )(x, att, u_full, conv_w, conv_b, ln_g, ln_b, w_out, gx, w_xq)


def _sample_cross_kernel(qx_ref, mk_ref, mv_ref, o_ref, *, n_mem):
    n_seq = mk_ref.shape[0]
    t_new = qx_ref.shape[0] // n_seq
    groups = mk_ref.shape[1] // (n_mem * X_HEADS)
    pitch = groups * X_HEADS

    def head_rows(ref, i, h):
        return jnp.concatenate([ref[i, pl.ds(j * X_HEADS + h, n_mem, stride=pitch), :] for j in range(groups)],
                               axis=1).astype(BF16)

    for i in range(n_seq):
        tok = slice(i * t_new, (i + 1) * t_new)
        o_ref[tok, :] = _cross_attend(qx_ref[tok, :],
                                      lambda h, i=i: (head_rows(mk_ref, i, h), head_rows(mv_ref, i, h)))


SAMPLE_CROSS_SEQS_PER_STEP = 4


def _sample_cross(qx, mem_k, mem_v, t_new):
    nb, n_mem, n_heads, hd = mem_k.shape
    sps = SAMPLE_CROSS_SEQS_PER_STEP
    rows = lambda a: jnp.transpose(a.reshape(nb, n_mem, n_heads, hd // PAIR_W, PAIR_W),
                                   (0, 1, 3, 2, 4)).reshape(nb, -1, PAIR_W)
    tok = pl.BlockSpec((sps * t_new, n_heads * hd), lambda n: (n, 0))
    mem = pl.BlockSpec((sps, n_mem * n_heads * hd // PAIR_W, PAIR_W), lambda n: (n, 0, 0))
    return pl.pallas_call(
        functools.partial(_sample_cross_kernel, n_mem=n_mem),
        grid=(nb // sps,),
        in_specs=[tok, mem, mem],
        out_specs=tok,
        out_shape=jax.ShapeDtypeStruct(qx.shape, BF16),
        compiler_params=_params(1),
        name="sample_cross",
    )(qx, rows(mem_k), rows(mem_v))


def _swiglu_final(x2, gf_ref, wg_ref, wu_ref, wd_ref, gfin_ref, ff_chunk):
    h = _rms(x2, gf_ref[...]).astype(BF16)
    acc = x2
    for lo in range(0, wg_ref.shape[1], ff_chunk):
        gate = _dot(h, wg_ref[:, lo:lo + ff_chunk])
        up = _dot(h, wu_ref[:, lo:lo + ff_chunk])
        acc = acc + _dot((gate * _sigmoid(gate) * up).astype(BF16), wd_ref[lo:lo + ff_chunk, :])
    return _rms(acc, gfin_ref[...])


def _ffn_kernel(x_ref, gf_ref, wg_ref, wu_ref, wd_ref, gfin_ref, o_ref, *, ff_chunk):
    o_ref[...] = _swiglu_final(x_ref[...], gf_ref, wg_ref, wu_ref, wd_ref, gfin_ref, ff_chunk)


def _sample_tail_kernel(x_ref, xo_ref, wxo_ref, gf_ref, wg_ref, wu_ref, wd_ref, gfin_ref, o_ref, *, ff_chunk):
    x2 = x_ref[...] + _dot(xo_ref[...], wxo_ref[...])
    o_ref[...] = _swiglu_final(x2, gf_ref, wg_ref, wu_ref, wd_ref, gfin_ref, ff_chunk)


FF_CHUNK = 256


def _resident(a):
    return pl.BlockSpec(a.shape, lambda *_: (0,) * a.ndim, pipeline_mode=pl.Buffered(1))


def _ffn(x, gf, wg, wu, wd, gfin, tm):
    n, d = x.shape
    row = lambda i: (i, 0)
    full = _resident
    return pl.pallas_call(
        functools.partial(_ffn_kernel, ff_chunk=FF_CHUNK),
        grid=(n // tm,),
        in_specs=[pl.BlockSpec((tm, d), row), full(gf), full(wg), full(wu), full(wd), full(gfin)],
        out_specs=pl.BlockSpec((tm, d), row),
        out_shape=jax.ShapeDtypeStruct((n, d), F32),
        compiler_params=_params(1),
        name="swiglu_final_norm",
    )(x, gf, wg, wu, wd, gfin)


def _sample_tail(x1, xo, w_xo, gf, wg, wu, wd, gfin):
    return pl.pallas_call(
        functools.partial(_sample_tail_kernel, ff_chunk=FF_CHUNK),
        out_shape=jax.ShapeDtypeStruct(x1.shape, F32),
        compiler_params=pltpu.CompilerParams(vmem_limit_bytes=VMEM_LIMIT),
        name="sample_cross_out_swiglu",
    )(x1, xo, w_xo, gf, wg, wu, wd, gfin)


def _mem_kv_kernel(m_ref, g_ref, wk_ref, wv_ref, k_ref, v_ref, kb_ref, vb_ref):
    n_mem = m_ref.shape[0]
    hd = wk_ref.shape[1] // X_HEADS
    groups = hd // PAIR_W
    h = _rms(m_ref[...], g_ref[...]).astype(BF16)
    for w_ref, o_ref, ob_ref in ((wk_ref, k_ref, kb_ref), (wv_ref, v_ref, vb_ref)):
        val = _dot(h, w_ref[...])
        ob_ref[...] = val.astype(BF16)
        for head in range(X_HEADS):
            for j in range(groups):
                lanes = slice(head * hd + j * PAIR_W, head * hd + (j + 1) * PAIR_W)
                o_ref[0, pl.ds(j * X_HEADS + head, n_mem, stride=groups * X_HEADS), :] = val[:, lanes]


def _mem_kv(mem, g, w_k, w_v, n_mem):
    n, d = mem.shape
    row = lambda i: (i, 0)
    full = lambda a: pl.BlockSpec(a.shape, lambda i: (0, 0))
    dk = w_k.shape[1]
    rows_spec = pl.BlockSpec((1, n_mem * dk // PAIR_W, PAIR_W), lambda i: (i, 0, 0))
    rows_shape = jax.ShapeDtypeStruct((n // n_mem, n_mem * dk // PAIR_W, PAIR_W), F32)
    return pl.pallas_call(
        _mem_kv_kernel,
        grid=(n // n_mem,),
        in_specs=[pl.BlockSpec((n_mem, d), row), full(g), full(w_k), full(w_v)],
        out_specs=[rows_spec, rows_spec, pl.BlockSpec((n_mem, dk), row), pl.BlockSpec((n_mem, dk), row)],
        out_shape=[rows_shape, rows_shape] + [jax.ShapeDtypeStruct((n, dk), BF16)] * 2,
        compiler_params=_params(1),
        name="mem_kv",
    )(mem, g, w_k, w_v)


ROW_TILE = 512
FFN_ROW_TILE = 1024
ATT_CHUNK = STEPS * DILATED_BRANCHES[-1][1]


def kernel(x_prompt, x_sample, mem_prompt, cache_win_k, cache_win_v, cache_conv, cache_mem_k, cache_mem_v,
           rpb_table, norm_mix_g, w_in, conv_w, conv_b, conv_ln_g, conv_ln_b, w_out, norm_x_g, norm_mem_g,
           w_xq, w_xk, w_xv, w_xo, norm_ffn_g, w_ffn_gate, w_ffn_up, w_ffn_down, norm_final_g):
    depth = w_in.shape[0]
    assert depth == 1, "single-layer stack"
    batch, seq, d = x_prompt.shape
    nb, t_new, _ = x_sample.shape
    buf_len = cache_win_k.shape[2]
    keep_p = min(MAX_DISTANCE, seq)
    n_mem = mem_prompt.shape[1]
    conv_hist = CONV_K - 1
    assert seq % ATT_CHUNK == 0 and keep_p % ATT_CHUNK == 0 and buf_len == MAX_DISTANCE

    row = lambda a: a.reshape(1, -1)
    bf = lambda a: a.astype(BF16)
    l = 0
    w_in_b, w_out_b = bf(w_in[l]), bf(w_out[l])
    w_xq_b, w_xk_b, w_xv_b, w_xo_b = bf(w_xq[l]), bf(w_xk[l]), bf(w_xv[l]), bf(w_xo[l])
    w_g_b, w_u_b, w_d_b = bf(w_ffn_gate[l]), bf(w_ffn_up[l]), bf(w_ffn_down[l])
    g_mix, g_x, g_mem, g_ffn, g_fin = (row(norm_mix_g[l]), row(norm_x_g[l]), row(norm_mem_g[l]),
                                       row(norm_ffn_g[l]), row(norm_final_g))
    cv_w, cv_b, ln_g, ln_b = conv_w[l], row(conv_b[l]), row(conv_ln_g[l]), row(conv_ln_b[l])

    bias_p = _bias_tables(rpb_table, _prompt_bucket_index(), masked_cols=STEPS)
    bias_s = _bias_tables(rpb_table, _sample_bucket_index(buf_len, t_new))[0]

    xp = x_prompt.reshape(batch * seq, d)
    q, k, v, u, p_wk_t, p_wv_t = _in_proj_prompt(xp, g_mix, w_in_b, seq, keep_p, ROW_TILE)
    att = _dilated_attention_prompt(q, k, v, bias_p, seq // ATT_CHUNK)
    mk, mv, mk_b, mv_b = _mem_kv(mem_prompt.reshape(batch * n_mem, d), g_mem, w_xk_b, w_xv_b, n_mem)
    x2 = _prompt_post(xp, att, u, cv_w, cv_b, ln_g, ln_b, w_out_b, g_x, w_xq_b, mk_b, mv_b, w_xo_b,
                      seq, FFN_ROW_TILE)
    y_prompt = _ffn(x2, g_ffn, w_g_b, w_u_b, w_d_b, g_fin, FFN_ROW_TILE).reshape(batch, seq, d)

    tok_major = lambda a: jnp.transpose(a.reshape(a.shape[0], N_ATT_HEADS, HEAD_DIM, -1), (0, 3, 1, 2))[None]
    p_wk, p_wv = tok_major(p_wk_t), tok_major(p_wv_t)
    u3 = u.reshape(batch, seq, -1)
    p_conv = u3[:, seq - conv_hist:][None]
    xh = lambda a: jnp.transpose(a.reshape(batch, n_mem, -1, X_HEADS, PAIR_W), (0, 1, 3, 2, 4)).reshape(
        1, batch, n_mem, X_HEADS, -1)
    p_mk, p_mv = xh(mk), xh(mv)

    xs = x_sample.reshape(nb * t_new, d)
    qs, us, ks_t, vs_t = _in_proj_sample(xs, g_mix, w_in_b)
    chan_major = lambda a: jnp.transpose(a, (0, 2, 3, 1)).reshape(nb, ATT_WIDTH, -1)
    s_wk_t, s_wv_t, att_s = _dilated_attention_sample(
        qs, ks_t, vs_t, chan_major(cache_win_k[l]), chan_major(cache_win_v[l]), bias_s, t_new)
    u_full = jnp.concatenate([cache_conv[l], us.reshape(nb, t_new, -1)], axis=1)
    x1s, qxs = _sample_mix(xs, att_s, u_full, cv_w, cv_b, ln_g, ln_b, w_out_b, g_x, w_xq_b)
    xo_s = _sample_cross(qxs, cache_mem_k[l], cache_mem_v[l], t_new)
    y_sample = _sample_tail(x1s, xo_s, w_xo_b, g_ffn, w_g_b, w_u_b, w_d_b, g_fin).reshape(nb, t_new, d)

    return (y_prompt, y_sample, p_wk, p_wv, p_conv, p_mk, p_mv,
            tok_major(s_wk_t), tok_major(s_wv_t), u_full[:, t_new:][None])
```

```python
import functools
import math

import numpy as np
import jax
import jax.numpy as jnp
from jax import lax
from jax.experimental import pallas as pl
from jax.experimental.pallas import tpu as pltpu

F32 = jnp.float32
BF16 = jnp.bfloat16

HEAD_DIM = 64
N_ATT_HEADS = 12
ATT_WIDTH = N_ATT_HEADS * HEAD_DIM
PAIR_W = 2 * HEAD_DIM
N_PAIRS = N_ATT_HEADS // 2
CONV_K = 31
CONV_HALO = 32
F32_SUBLANES = 8
DILATED_BRANCHES = ((128, 1), (512, 4), (2048, 16))
STEPS = 128
N_CLS = DILATED_BRANCHES[-1][1]
GATHER_STRIDE = 4
NEW_LANES = 128
N_BUCKETS = 32
MAX_EXACT = N_BUCKETS // 2
MAX_DISTANCE = 2048
X_HEADS = 4
EPS = 1e-6
LOG2E = math.log2(math.e)
ATT_SCALE = HEAD_DIM ** -0.5 * LOG2E
MASKED = -1e30

V7X_VMEM_BYTES = 64 * 1024 * 1024
VMEM_LIMIT = V7X_VMEM_BYTES * 3 // 4


def _params(n_grid_dims):
    return pltpu.CompilerParams(dimension_semantics=("arbitrary",) * n_grid_dims,
                                vmem_limit_bytes=VMEM_LIMIT)


def _rms(x, g):
    return x * lax.rsqrt(jnp.mean(x * x, axis=-1, keepdims=True) + EPS) * g


def _sigmoid(x):
    return 1.0 / (1.0 + jnp.exp(-x))


def _dot(a, b):
    return jnp.dot(a, b, preferred_element_type=F32)


def _dot_nt(a, b):
    return lax.dot_general(a, b, (((1,), (1,)), ((), ())), preferred_element_type=F32)


def _t5_bucket_np(dist):
    n = dist.astype(np.int32)
    nf = np.maximum(n, MAX_EXACT).astype(np.float32)
    large = MAX_EXACT + (np.log(nf / np.float32(MAX_EXACT)) / np.float32(math.log(MAX_DISTANCE / MAX_EXACT))
                         * np.float32(N_BUCKETS - MAX_EXACT)).astype(np.int32)
    large = np.minimum(large, N_BUCKETS - 1)
    return np.where(n < MAX_EXACT, n, large)


def _block_order(dil):
    groups = N_CLS // dil
    per = STEPS // groups
    i = np.arange(STEPS)
    return (i % per) * groups + i // per


def _prompt_bucket_index():
    out = []
    for _, dil in DILATED_BRANCHES:
        n = _block_order(dil)
        kj = np.concatenate([n, STEPS + n])[None, :]
        sub = STEPS + n[:, None] - kj
        band = (sub >= 0) & (sub <= STEPS)
        out.append(np.where(band, _t5_bucket_np(dil * np.maximum(sub, 0)), -1))
    return np.stack(out).astype(np.int32)


def _sample_bucket_index(buf_len, t_new):
    col = np.arange(buf_len + NEW_LANES)
    key = np.where(col < buf_len, (col + t_new) % buf_len, col - NEW_LANES + t_new)
    keep = (col < buf_len) | (col >= buf_len + NEW_LANES - t_new)
    i = np.arange(t_new)[:, None]
    dist = buf_len + i - key[None, :]
    out = []
    for window, dil in DILATED_BRANCHES:
        ok = keep[None, :] & (dist >= 0) & (dist % dil == 0) & (dist // dil <= window // dil)
        out.append(np.where(ok, _t5_bucket_np(np.maximum(dist, 0)), -1))
    return np.stack(out).astype(np.int32)


def _bias_kernel(tab_ref, idx_ref, o_ref, *, masked_cols, buckets):
    pair = pl.program_id(0)
    for g, present in enumerate(buckets):
        idx = idx_ref[g]
        rows = idx.shape[0]
        for half in range(2):
            head = 2 * pair + half
            acc = jnp.full(idx.shape, MASKED, F32)
            for b in present:
                acc = jnp.where(idx == b, tab_ref[b, head] * LOG2E, acc)
            o_ref[0, g, 0, half * rows:(half + 1) * rows, :] = acc
            if masked_cols:
                col = lax.broadcasted_iota(jnp.int32, idx.shape, 1)
                o_ref[1, g, 0, half * rows:(half + 1) * rows, :] = jnp.where(col < masked_cols, MASKED, acc)


def _bias_tables(table, idx, masked_cols=0):
    n_var, rows, cols = idx.shape
    n_out = 2 if masked_cols else 1
    buckets = tuple(tuple(int(b) for b in np.unique(idx[g]) if b >= 0) for g in range(n_var))
    return pl.pallas_call(
        functools.partial(_bias_kernel, masked_cols=masked_cols, buckets=buckets),
        grid=(N_PAIRS,),
        in_specs=[pl.BlockSpec(memory_space=pltpu.SMEM),
                  pl.BlockSpec((n_var, rows, cols), lambda p: (0, 0, 0))],
        out_specs=pl.BlockSpec((n_out, n_var, 1, 2 * rows, cols), lambda p: (0, 0, p, 0, 0)),
        out_shape=jax.ShapeDtypeStruct((n_out, n_var, N_PAIRS, 2 * rows, cols), F32),
        compiler_params=_params(1),
        name="bias_tables",
    )(table, jnp.asarray(idx))


def _in_proj_sample_kernel(x_ref, g_ref, w_ref, q_ref, u_ref, kt_ref, vt_ref):
    h = _rms(x_ref[...], g_ref[...]).astype(BF16)
    aw = q_ref.shape[1]
    cw = u_ref.shape[1]
    q_ref[...] = _dot(h, w_ref[:, 0:aw]) * ATT_SCALE
    a = _dot(h, w_ref[:, 3 * aw:3 * aw + cw])
    gate = _dot(h, w_ref[:, 3 * aw + cw:3 * aw + 2 * cw])
    u_ref[...] = a * _sigmoid(gate)
    kt_ref[...] = _dot(h, w_ref[:, aw:2 * aw]).T
    vt_ref[...] = _dot(h, w_ref[:, 2 * aw:3 * aw]).T


def _in_proj_sample(x, g, w_bf16):
    n, d = x.shape
    aw = ATT_WIDTH
    cw = (w_bf16.shape[1] - 3 * aw) // 2
    new_shape = jax.ShapeDtypeStruct((aw, n), F32)
    return pl.pallas_call(
        _in_proj_sample_kernel,
        out_shape=[jax.ShapeDtypeStruct((n, aw), F32), jax.ShapeDtypeStruct((n, cw), F32), new_shape, new_shape],
        compiler_params=pltpu.CompilerParams(vmem_limit_bytes=VMEM_LIMIT),
        name="in_proj_sample",
    )(x, g, w_bf16)


def _in_proj_prompt_kernel(x_ref, g_ref, w_ref, q_ref, k_ref, v_ref, u_ref, kt_ref, vt_ref, xs, xs2, *,
                           tiles_per_seq, tail_first):
    tm, d = x_ref.shape
    per = tm // N_CLS
    aw = q_ref.shape[3]
    cw = u_ref.shape[1]
    xn = _rms(x_ref[...], g_ref[...])
    hn = xn.astype(BF16)
    gate = _dot(hn, w_ref[:, 3 * aw + cw:3 * aw + 2 * cw])
    u_ref[...] = _dot(hn, w_ref[:, 3 * aw:3 * aw + cw]) * _sigmoid(gate)

    quarter = tm // GATHER_STRIDE
    for c in range(d // PAIR_W):
        xs[c] = xn[:, c * PAIR_W:(c + 1) * PAIR_W]
    for c in range(d // PAIR_W):
        for res in range(GATHER_STRIDE):
            xs2[c, res * quarter:(res + 1) * quarter, :] = xs[c, pl.ds(res, quarter, stride=GATHER_STRIDE), :]

    def class_rows(c, r):
        start = (r % GATHER_STRIDE) * quarter + r // GATHER_STRIDE
        return xs2[c, pl.ds(start, per, stride=GATHER_STRIDE), :]

    h = jnp.concatenate(
        [jnp.concatenate([class_rows(c, r) for r in range(N_CLS)], axis=0) for c in range(d // PAIR_W)],
        axis=1).astype(BF16)
    q_ref[0] = (_dot(h, w_ref[:, 0:aw]) * ATT_SCALE).reshape(N_CLS, per, aw)
    k_ref[0] = _dot(h, w_ref[:, aw:2 * aw]).reshape(N_CLS, per, aw)
    v_ref[0] = _dot(h, w_ref[:, 2 * aw:3 * aw]).reshape(N_CLS, per, aw)

    @pl.when(pl.program_id(0) % tiles_per_seq >= tail_first)
    def _():
        kt_ref[0] = _dot(hn, w_ref[:, aw:2 * aw]).T
        vt_ref[0] = _dot(hn, w_ref[:, 2 * aw:3 * aw]).T


def _in_proj_prompt(x, g, w_bf16, seq, keep, tm):
    n, d = x.shape
    aw = ATT_WIDTH
    cw = (w_bf16.shape[1] - 3 * aw) // 2
    chunk = N_CLS * STEPS
    tiles_per_chunk = chunk // tm
    tiles_per_seq = seq // tm
    tail_first = (seq - keep) // tm
    row = lambda i: (i, 0)
    fixed = lambda i: (0, 0)
    cls = pl.BlockSpec((1, N_CLS, tm // N_CLS, aw), lambda i: (i // tiles_per_chunk, 0, i % tiles_per_chunk, 0))
    tail = pl.BlockSpec((1, aw, tm),
                        lambda i: (i // tiles_per_seq, 0, jnp.maximum(i % tiles_per_seq - tail_first, 0)))
    cls_shape = jax.ShapeDtypeStruct((n // chunk, N_CLS, STEPS, aw), F32)
    tail_shape = jax.ShapeDtypeStruct((n // seq, aw, keep), F32)
    return pl.pallas_call(
        functools.partial(_in_proj_prompt_kernel, tiles_per_seq=tiles_per_seq, tail_first=tail_first),
        grid=(n // tm,),
        in_specs=[pl.BlockSpec((tm, d), row), pl.BlockSpec((1, d), fixed),
                  pl.BlockSpec(w_bf16.shape, fixed)],
        out_specs=[cls, cls, cls, pl.BlockSpec((tm, cw), row), tail, tail],
        out_shape=[cls_shape, cls_shape, cls_shape, jax.ShapeDtypeStruct((n, cw), F32), tail_shape, tail_shape],
        scratch_shapes=[pltpu.VMEM((d // PAIR_W, tm, PAIR_W), F32)] * 2,
        compiler_params=_params(1),
        name="in_proj_prompt",
    )(x, g, w_bf16)


def _stack_heads(x, first):
    zero = jnp.zeros_like(x)
    return jnp.concatenate([jnp.where(first, x, zero), jnp.where(first, zero, x)], axis=0)


def _att_kernel(q_ref, kc_ref, kp_ref, vc_ref, vp_ref, bias_ref, bias0_ref, o_ref, num_s, m_s, l_s, nat_s):
    first = lax.broadcasted_iota(jnp.int32, (STEPS, PAIR_W), 1) < HEAD_DIM
    ones = jnp.ones((2 * STEPS, PAIR_W), BF16)

    def pieces(dil, cls, blk):
        groups = N_CLS // dil
        per = STEPS // groups
        return [(cls + dil * j, pl.ds(blk * per, per)) for j in range(groups)], per

    def load(ref, dil, cls, blk):
        idx, _ = pieces(dil, cls, blk)
        return jnp.concatenate([ref[0, r, rows, :] for r, rows in idx], axis=0)

    def store(ref, g, dil, cls, blk, val):
        idx, per = pieces(dil, cls, blk)
        for j, (r, rows) in enumerate(idx):
            ref[g, r, rows, :] = val[j * per:(j + 1) * per]

    def attend(g, dil, cls, blk, k_prev, v_prev, bias):
        q_st = _stack_heads(load(q_ref, dil, cls, blk), first).astype(BF16)
        k_own = load(kc_ref, dil, cls, blk).astype(BF16)
        v_own = load(vc_ref, dil, cls, blk).astype(BF16)
        kcat = jnp.concatenate([k_prev, k_own], axis=0)
        vcat = jnp.concatenate([v_prev, v_own], axis=0)
        s = _dot_nt(q_st, kcat) + bias
        m = jnp.max(s, axis=-1, keepdims=True)
        p = jnp.exp2(s - m).astype(BF16)
        o = _dot(p, jnp.concatenate([vcat, ones], axis=1))
        store(num_s, g, dil, cls, blk, jnp.where(first, o[:STEPS, :PAIR_W], o[STEPS:, :PAIR_W]))
        store(l_s, g, dil, cls, blk, jnp.where(first, o[:STEPS, PAIR_W:], o[STEPS:, PAIR_W:]))
        store(m_s, g, dil, cls, blk, jnp.where(first, jnp.broadcast_to(m[:STEPS], (STEPS, PAIR_W)),
                                               jnp.broadcast_to(m[STEPS:], (STEPS, PAIR_W))))
        return k_own, v_own

    carried = {}
    for i in range(N_CLS):
        for g, (_, dil) in reversed(list(enumerate(DILATED_BRANCHES))):
            n_blk = N_CLS // dil
            cls, blk = i % dil, i // dil
            if blk == 0:
                prev = (load(kp_ref, dil, cls, n_blk - 1).astype(BF16), load(vp_ref, dil, cls, n_blk - 1).astype(BF16))
                bias = bias0_ref[0, g, 0]
            else:
                prev, bias = carried[g, cls], bias_ref[g, 0]
            carried[g, cls] = attend(g, dil, cls, blk, *prev, bias)

    quarter = N_CLS * STEPS // GATHER_STRIDE
    for r in range(N_CLS):
        m_all = jnp.maximum(jnp.maximum(m_s[0, r], m_s[1, r]), m_s[2, r])
        num = jnp.zeros(m_all.shape, F32)
        den = jnp.zeros(m_all.shape, F32)
        for g in range(len(DILATED_BRANCHES)):
            w = jnp.exp2(m_s[g, r] - m_all)
            num = num + w * num_s[g, r]
            den = den + w * l_s[g, r]
        start = (r % GATHER_STRIDE) * quarter + r // GATHER_STRIDE
        nat_s[pl.ds(start, STEPS, stride=GATHER_STRIDE), :] = num / den
    for res in range(GATHER_STRIDE):
        o_ref[pl.ds(res, quarter, stride=GATHER_STRIDE), :] = nat_s[res * quarter:(res + 1) * quarter, :]


def _dilated_attention_prompt(q, k, v, bias, chunks_per_seq):
    n_chunks = q.shape[0]
    n_br = len(DILATED_BRANCHES)
    cur = lambda b, c, p: (b * chunks_per_seq + c, 0, 0, p)
    prev = lambda b, c, p: (b * chunks_per_seq + jnp.maximum(c - 1, 0), 0, 0, p)
    blk = pl.BlockSpec((1, N_CLS, STEPS, PAIR_W), cur)
    blk_prev = pl.BlockSpec((1, N_CLS, STEPS, PAIR_W), prev)
    return pl.pallas_call(
        _att_kernel,
        grid=(n_chunks // chunks_per_seq, chunks_per_seq, N_PAIRS),
        in_specs=[blk, blk, blk_prev, blk, blk_prev,
                  pl.BlockSpec((n_br, 1, 2 * STEPS, 2 * STEPS), lambda b, c, p: (0, p, 0, 0)),
                  pl.BlockSpec((1, n_br, 1, 2 * STEPS, 2 * STEPS),
                               lambda b, c, p: (jnp.where(c == 0, 1, 0), 0, p, 0, 0))],
        out_specs=pl.BlockSpec((N_CLS * STEPS, PAIR_W), lambda b, c, p: (b * chunks_per_seq + c, p)),
        out_shape=jax.ShapeDtypeStruct((n_chunks * N_CLS * STEPS, ATT_WIDTH), F32),
        scratch_shapes=[pltpu.VMEM((n_br, N_CLS, STEPS, PAIR_W), F32)] * 3 + [pltpu.VMEM((N_CLS * STEPS, PAIR_W), F32)],
        compiler_params=_params(3),
        name="dilated_attention_prompt",
    )(q, k, k, v, v, bias[0], bias)


def _sample_att_kernel(q_ref, kn_ref, vn_ref, ck_ref, cv_ref, bias_ref, ok_ref, ov_ref, o_ref):
    buf_len = ck_ref.shape[2]
    t_new = q_ref.shape[0]
    tail = buf_len - NEW_LANES
    is_new = lax.broadcasted_iota(jnp.int32, (PAIR_W, NEW_LANES), 1) >= NEW_LANES - t_new
    first = lax.broadcasted_iota(jnp.int32, (t_new, PAIR_W), 1) < HEAD_DIM
    col0 = (pl.program_id(0) % (NEW_LANES // t_new)) * t_new
    to_end = (NEW_LANES - t_new - col0) % NEW_LANES

    for j in range(ck_ref.shape[1] // PAIR_W):
        ch = slice(j * PAIR_W, (j + 1) * PAIR_W)

        def shift_in(c_ref, n_ref, o_ref):
            rot = pltpu.roll(c_ref[0, ch, :], buf_len - t_new, axis=1)
            new = jnp.where(is_new, pltpu.roll(n_ref[ch, :], to_end, axis=1), 0.0)
            o_ref[0, ch, :tail] = rot[:, :tail]
            o_ref[0, ch, tail:] = jnp.where(is_new, new, rot[:, tail:])
            return rot.astype(BF16), new.astype(BF16)

        rot_k, new_k = shift_in(ck_ref, kn_ref, ok_ref)
        rot_v, new_v = shift_in(cv_ref, vn_ref, ov_ref)

        q_st = _stack_heads(q_ref[:, ch], first).astype(BF16)
        s = jnp.concatenate([_dot(q_st, rot_k), _dot(q_st, new_k)], axis=1)
        sg = [s + bias_ref[g, j] for g in range(len(DILATED_BRANCHES))]
        m = functools.reduce(jnp.maximum, [jnp.max(x, axis=-1, keepdims=True) for x in sg])
        p = functools.reduce(jnp.add, [jnp.exp2(x - m) for x in sg])
        den = jnp.sum(p, axis=-1, keepdims=True)
        pb = p.astype(BF16)
        o = (_dot_nt(pb[:, :buf_len], rot_v) + _dot_nt(pb[:, buf_len:], new_v)) / den
        o_ref[:, ch] = jnp.where(first, o[:t_new], o[t_new:]).astype(o_ref.dtype)


SAMPLE_PAIRS_PER_STEP = 3


def _dilated_attention_sample(q, k_new_t, v_new_t, cache_k_t, cache_v_t, bias, t_new):
    nb, _, buf_len = cache_k_t.shape
    n_br = len(DILATED_BRANCHES)
    pps = SAMPLE_PAIRS_PER_STEP
    width = pps * PAIR_W
    tok = pl.BlockSpec((t_new, width), lambda n, p: (n, p))
    new = pl.BlockSpec((width, NEW_LANES), lambda n, p: (p, n // (NEW_LANES // t_new)))
    buf = pl.BlockSpec((1, width, buf_len), lambda n, p: (n, p, 0))
    return pl.pallas_call(
        _sample_att_kernel,
        grid=(nb, N_PAIRS // pps),
        in_specs=[tok, new, new, buf, buf,
                  pl.BlockSpec((n_br, pps, 2 * t_new, buf_len + NEW_LANES), lambda n, p: (0, p, 0, 0))],
        out_specs=[buf, buf, tok],
        out_shape=[jax.ShapeDtypeStruct(cache_k_t.shape, F32), jax.ShapeDtypeStruct(cache_v_t.shape, F32),
                   jax.ShapeDtypeStruct((nb * t_new, ATT_WIDTH), BF16)],
        compiler_params=_params(2),
        name="dilated_attention_sample",
    )(q, k_new_t, v_new_t, cache_k_t, cache_v_t, bias)


def _conv_tail(y, cb_ref, lg_ref, lb_ref):
    y = y + cb_ref[...]
    mu = jnp.mean(y, axis=-1, keepdims=True)
    yc = y - mu
    var = jnp.mean(yc * yc, axis=-1, keepdims=True)
    yn = yc * lax.rsqrt(var + EPS) * lg_ref[...] + lb_ref[...]
    return yn * _sigmoid(yn)


def _mix_out(x, att, c, wo_ref):
    aw = att.shape[1]
    return x + _dot(att.astype(BF16), wo_ref[:aw, :]) + _dot(c.astype(BF16), wo_ref[aw:, :])


def _cross_attend(qx, head_kv):
    hd = qx.shape[1] // X_HEADS
    outs = []
    for h in range(X_HEADS):
        mk, mv = head_kv(h)
        s = _dot_nt(qx[:, h * hd:(h + 1) * hd].astype(BF16), mk)
        p = jnp.exp2(s - jnp.max(s, axis=-1, keepdims=True))
        den = jnp.sum(p, axis=-1, keepdims=True)
        outs.append((_dot(p.astype(BF16), mv) / den).astype(BF16))
    return jnp.concatenate(outs, axis=1)


def _prompt_post_kernel(x_ref, att_ref, u_ref, uh_ref, cw_ref, cb_ref, lg_ref, lb_ref, wo_ref,
                        gx_ref, wq_ref, mk_ref, mv_ref, wxo_ref, o_ref, ubuf, *, tiles_per_seq):
    tm = x_ref.shape[0]
    seq_start = (pl.program_id(0) % tiles_per_seq) == 0
    ubuf[:CONV_HALO, :] = jnp.where(seq_start, 0.0, uh_ref[...])
    ubuf[CONV_HALO:, :] = u_ref[...]
    hist = ubuf[...]
    n_rows = hist.shape[0]
    first_tap = CONV_HALO - (CONV_K - 1)
    y = jnp.zeros(u_ref.shape, F32)
    for s in range(F32_SUBLANES):
        shifted = hist if s == 0 else pltpu.roll(hist, n_rows - s, axis=0)
        for j in range(CONV_K):
            if (first_tap + j) % F32_SUBLANES == s:
                lo = first_tap + j - s
                y = y + cw_ref[j:j + 1, :] * shifted[lo:lo + tm]
    c = _conv_tail(y, cb_ref, lg_ref, lb_ref)
    x1 = _mix_out(x_ref[...], att_ref[...], c, wo_ref)
    hd = wq_ref.shape[1] // X_HEADS
    qx = _dot(_rms(x1, gx_ref[...]).astype(BF16), wq_ref[...]) * (hd ** -0.5 * LOG2E)
    head_kv = lambda h: (mk_ref[:, h * hd:(h + 1) * hd], mv_ref[:, h * hd:(h + 1) * hd])
    o_ref[...] = x1 + _dot(_cross_attend(qx, head_kv), wxo_ref[...])


def _prompt_post(x, att, u, conv_w, conv_b, ln_g, ln_b, w_out, gx, w_xq, mk, mv, w_xo, seq, tm):
    n, d = x.shape
    cw = u.shape[1]
    n_mem = mk.shape[0] // (n // seq)
    tiles_per_seq = seq // tm
    row = lambda i: (i, 0)
    fixed = lambda i: (0, 0)
    halo = lambda i: (jnp.maximum(i * (tm // CONV_HALO) - 1, 0), 0)
    per_seq = lambda i: (i // tiles_per_seq, 0)
    full = _resident
    return pl.pallas_call(
        functools.partial(_prompt_post_kernel, tiles_per_seq=tiles_per_seq),
        grid=(n // tm,),
        in_specs=[pl.BlockSpec((tm, d), row), pl.BlockSpec((tm, ATT_WIDTH), row),
                  pl.BlockSpec((tm, cw), row), pl.BlockSpec((CONV_HALO, cw), halo),
                  full(conv_w), full(conv_b), full(ln_g), full(ln_b), full(w_out),
                  full(gx), full(w_xq),
                  pl.BlockSpec((n_mem, d), per_seq), pl.BlockSpec((n_mem, d), per_seq), full(w_xo)],
        out_specs=pl.BlockSpec((tm, d), row),
        out_shape=jax.ShapeDtypeStruct((n, d), F32),
        scratch_shapes=[pltpu.VMEM((CONV_HALO + tm, cw), F32)],
        compiler_params=_params(1),
        name="prompt_mix_cross",
    )(x, att, u, u, conv_w, conv_b, ln_g, ln_b, w_out, gx, w_xq, mk, mv, w_xo)


def _sample_mix_kernel(x_ref, att_ref, uf_ref, cw_ref, cb_ref, lg_ref, lb_ref, wo_ref, gx_ref, wq_ref,
                       x1_ref, qx_ref):
    nb, t_full, cw = uf_ref.shape
    t_new = t_full - (CONV_K - 1)
    y = jnp.zeros((nb, t_new, cw), F32)
    for j in range(CONV_K):
        y = y + cw_ref[j:j + 1, :] * uf_ref[:, j:j + t_new, :]
    c = _conv_tail(y.reshape(nb * t_new, cw), cb_ref, lg_ref, lb_ref)
    x1 = _mix_out(x_ref[...], att_ref[...], c, wo_ref)
    x1_ref[...] = x1
    hd = wq_ref.shape[1] // X_HEADS
    qx_ref[...] = _dot(_rms(x1, gx_ref[...]).astype(BF16), wq_ref[...]) * (hd ** -0.5 * LOG2E)


def _sample_mix(x, att, u_full, conv_w, conv_b, ln_g, ln_b, w_out, gx, w_xq):
    n, d = x.shape
    return pl.pallas_call(
        _sample_mix_kernel,
        out_shape=[jax.ShapeDtypeStruct((n, d), F32), jax.ShapeDtypeStruct((n, w_xq.shape[1]), F32)],
        compiler_params=pltpu.CompilerParams(vmem_limit_bytes=VMEM_LIMIT),
        name="sample_mix",
    )(x, att, u_full, conv_w, conv_b, ln_g, ln_b, w_out, gx, w_xq)


def _sample_cross_kernel(qx_ref, mk_ref, mv_ref, o_ref, *, n_mem):
    n_seq = mk_ref.shape[0]
    t_new = qx_ref.shape[0] // n_seq
    groups = mk_ref.shape[1] // (n_mem * X_HEADS)
    pitch = groups * X_HEADS

    def head_rows(ref, i, h):
        return jnp.concatenate([ref[i, pl.ds(j * X_HEADS + h, n_mem, stride=pitch), :] for j in range(groups)],
                               axis=1).astype(BF16)

    for i in range(n_seq):
        tok = slice(i * t_new, (i + 1) * t_new)
        o_ref[tok, :] = _cross_attend(qx_ref[tok, :],
                                      lambda h, i=i: (head_rows(mk_ref, i, h), head_rows(mv_ref, i, h)))


SAMPLE_CROSS_SEQS_PER_STEP = 4


def _sample_cross(qx, mem_k, mem_v, t_new):
    nb, n_mem, n_heads, hd = mem_k.shape
    sps = SAMPLE_CROSS_SEQS_PER_STEP
    rows = lambda a: jnp.transpose(a.reshape(nb, n_mem, n_heads, hd // PAIR_W, PAIR_W),
                                   (0, 1, 3, 2, 4)).reshape(nb, -1, PAIR_W)
    tok = pl.BlockSpec((sps * t_new, n_heads * hd), lambda n: (n, 0))
    mem = pl.BlockSpec((sps, n_mem * n_heads * hd // PAIR_W, PAIR_W), lambda n: (n, 0, 0))
    return pl.pallas_call(
        functools.partial(_sample_cross_kernel, n_mem=n_mem),
        grid=(nb // sps,),
        in_specs=[tok, mem, mem],
        out_specs=tok,
        out_shape=jax.ShapeDtypeStruct(qx.shape, BF16),
        compiler_params=_params(1),
        name="sample_cross",
    )(qx, rows(mem_k), rows(mem_v))


def _swiglu_final(x2, gf_ref, wg_ref, wu_ref, wd_ref, gfin_ref, ff_chunk):
    h = _rms(x2, gf_ref[...]).astype(BF16)
    acc = x2
    for lo in range(0, wg_ref.shape[1], ff_chunk):
        gate = _dot(h, wg_ref[:, lo:lo + ff_chunk])
        up = _dot(h, wu_ref[:, lo:lo + ff_chunk])
        acc = acc + _dot((gate * _sigmoid(gate) * up).astype(BF16), wd_ref[lo:lo + ff_chunk, :])
    return _rms(acc, gfin_ref[...])


def _ffn_kernel(x_ref, gf_ref, wg_ref, wu_ref, wd_ref, gfin_ref, o_ref, *, ff_chunk):
    o_ref[...] = _swiglu_final(x_ref[...], gf_ref, wg_ref, wu_ref, wd_ref, gfin_ref, ff_chunk)


def _sample_tail_kernel(x_ref, xo_ref, wxo_ref, gf_ref, wg_ref, wu_ref, wd_ref, gfin_ref, o_ref, *, ff_chunk):
    x2 = x_ref[...] + _dot(xo_ref[...], wxo_ref[...])
    o_ref[...] = _swiglu_final(x2, gf_ref, wg_ref, wu_ref, wd_ref, gfin_ref, ff_chunk)


FF_CHUNK = 256


def _resident(a):
    return pl.BlockSpec(a.shape, lambda *_: (0,) * a.ndim, pipeline_mode=pl.Buffered(1))


def _ffn(x, gf, wg, wu, wd, gfin, tm):
    n, d = x.shape
    row = lambda i: (i, 0)
    full = _resident
    return pl.pallas_call(
        functools.partial(_ffn_kernel, ff_chunk=FF_CHUNK),
        grid=(n // tm,),
        in_specs=[pl.BlockSpec((tm, d), row), full(gf), full(wg), full(wu), full(wd), full(gfin)],
        out_specs=pl.BlockSpec((tm, d), row),
        out_shape=jax.ShapeDtypeStruct((n, d), F32),
        compiler_params=_params(1),
        name="swiglu_final_norm",
    )(x, gf, wg, wu, wd, gfin)


def _sample_tail(x1, xo, w_xo, gf, wg, wu, wd, gfin):
    return pl.pallas_call(
        functools.partial(_sample_tail_kernel, ff_chunk=FF_CHUNK),
        out_shape=jax.ShapeDtypeStruct(x1.shape, F32),
        compiler_params=pltpu.CompilerParams(vmem_limit_bytes=VMEM_LIMIT),
        name="sample_cross_out_swiglu",
    )(x1, xo, w_xo, gf, wg, wu, wd, gfin)


def _mem_kv_kernel(m_ref, g_ref, wk_ref, wv_ref, k_ref, v_ref, kb_ref, vb_ref):
    n_mem = m_ref.shape[0]
    hd = wk_ref.shape[1] // X_HEADS
    groups = hd // PAIR_W
    h = _rms(m_ref[...], g_ref[...]).astype(BF16)
    for w_ref, o_ref, ob_ref in ((wk_ref, k_ref, kb_ref), (wv_ref, v_ref, vb_ref)):
        val = _dot(h, w_ref[...])
        ob_ref[...] = val.astype(BF16)
        for head in range(X_HEADS):
            for j in range(groups):
                lanes = slice(head * hd + j * PAIR_W, head * hd + (j + 1) * PAIR_W)
                o_ref[0, pl.ds(j * X_HEADS + head, n_mem, stride=groups * X_HEADS), :] = val[:, lanes]


def _mem_kv(mem, g, w_k, w_v, n_mem):
    n, d = mem.shape
    row = lambda i: (i, 0)
    full = lambda a: pl.BlockSpec(a.shape, lambda i: (0, 0))
    dk = w_k.shape[1]
    rows_spec = pl.BlockSpec((1, n_mem * dk // PAIR_W, PAIR_W), lambda i: (i, 0, 0))
    rows_shape = jax.ShapeDtypeStruct((n // n_mem, n_mem * dk // PAIR_W, PAIR_W), F32)
    return pl.pallas_call(
        _mem_kv_kernel,
        grid=(n // n_mem,),
        in_specs=[pl.BlockSpec((n_mem, d), row), full(g), full(w_k), full(w_v)],
        out_specs=[rows_spec, rows_spec, pl.BlockSpec((n_mem, dk), row), pl.BlockSpec((n_mem, dk), row)],
        out_shape=[rows_shape, rows_shape] + [jax.ShapeDtypeStruct((n, dk), BF16)] * 2,
        compiler_params=_params(1),
        name="mem_kv",
    )(mem, g, w_k, w_v)


ROW_TILE = 512
FFN_ROW_TILE = 1024
ATT_CHUNK = STEPS * DILATED_BRANCHES[-1][1]


def kernel(x_prompt, x_sample, mem_prompt, cache_win_k, cache_win_v, cache_conv, cache_mem_k, cache_mem_v,
           rpb_table, norm_mix_g, w_in, conv_w, conv_b, conv_ln_g, conv_ln_b, w_out, norm_x_g, norm_mem_g,
           w_xq, w_xk, w_xv, w_xo, norm_ffn_g, w_ffn_gate, w_ffn_up, w_ffn_down, norm_final_g):
    depth = w_in.shape[0]
    assert depth == 1, "single-layer stack"
    batch, seq, d = x_prompt.shape
    nb, t_new, _ = x_sample.shape
    buf_len = cache_win_k.shape[2]
    keep_p = min(MAX_DISTANCE, seq)
    n_mem = mem_prompt.shape[1]
    conv_hist = CONV_K - 1
    assert seq % ATT_CHUNK == 0 and keep_p % ATT_CHUNK == 0 and buf_len == MAX_DISTANCE

    row = lambda a: a.reshape(1, -1)
    bf = lambda a: a.astype(BF16)
    l = 0
    w_in_b, w_out_b = bf(w_in[l]), bf(w_out[l])
    w_xq_b, w_xk_b, w_xv_b, w_xo_b = bf(w_xq[l]), bf(w_xk[l]), bf(w_xv[l]), bf(w_xo[l])
    w_g_b, w_u_b, w_d_b = bf(w_ffn_gate[l]), bf(w_ffn_up[l]), bf(w_ffn_down[l])
    g_mix, g_x, g_mem, g_ffn, g_fin = (row(norm_mix_g[l]), row(norm_x_g[l]), row(norm_mem_g[l]),
                                       row(norm_ffn_g[l]), row(norm_final_g))
    cv_w, cv_b, ln_g, ln_b = conv_w[l], row(conv_b[l]), row(conv_ln_g[l]), row(conv_ln_b[l])

    bias_p = _bias_tables(rpb_table, _prompt_bucket_index(), masked_cols=STEPS)
    bias_s = _bias_tables(rpb_table, _sample_bucket_index(buf_len, t_new))[0]

    xp = x_prompt.reshape(batch * seq, d)
    q, k, v, u, p_wk_t, p_wv_t = _in_proj_prompt(xp, g_mix, w_in_b, seq, keep_p, ROW_TILE)
    att = _dilated_attention_prompt(q, k, v, bias_p, seq // ATT_CHUNK)
    mk, mv, mk_b, mv_b = _mem_kv(mem_prompt.reshape(batch * n_mem, d), g_mem, w_xk_b, w_xv_b, n_mem)
    x2 = _prompt_post(xp, att, u, cv_w, cv_b, ln_g, ln_b, w_out_b, g_x, w_xq_b, mk_b, mv_b, w_xo_b,
                      seq, FFN_ROW_TILE)
    y_prompt = _ffn(x2, g_ffn, w_g_b, w_u_b, w_d_b, g_fin, FFN_ROW_TILE).reshape(batch, seq, d)

    tok_major = lambda a: jnp.transpose(a.reshape(a.shape[0], N_ATT_HEADS, HEAD_DIM, -1), (0, 3, 1, 2))[None]
    p_wk, p_wv = tok_major(p_wk_t), tok_major(p_wv_t)
    u3 = u.reshape(batch, seq, -1)
    p_conv = u3[:, seq - conv_hist:][None]
    xh = lambda a: jnp.transpose(a.reshape(batch, n_mem, -1, X_HEADS, PAIR_W), (0, 1, 3, 2, 4)).reshape(
        1, batch, n_mem, X_HEADS, -1)
    p_mk, p_mv = xh(mk), xh(mv)

    xs = x_sample.reshape(nb * t_new, d)
    qs, us, ks_t, vs_t = _in_proj_sample(xs, g_mix, w_in_b)
    chan_major = lambda a: jnp.transpose(a, (0, 2, 3, 1)).reshape(nb, ATT_WIDTH, -1)
    s_wk_t, s_wv_t, att_s = _dilated_attention_sample(
        qs, ks_t, vs_t, chan_major(cache_win_k[l]), chan_major(cache_win_v[l]), bias_s, t_new)
    u_full = jnp.concatenate([cache_conv[l], us.reshape(nb, t_new, -1)], axis=1)
    x1s, qxs = _sample_mix(xs, att_s, u_full, cv_w, cv_b, ln_g, ln_b, w_out_b, g_x, w_xq_b)
    xo_s = _sample_cross(qxs, cache_mem_k[l], cache_mem_v[l], t_new)
    y_sample = _sample_tail(x1s, xo_s, w_xo_b, g_ffn, w_g_b, w_u_b, w_d_b, g_fin).reshape(nb, t_new, d)

    return (y_prompt, y_sample, p_wk, p_wv, p_conv, p_mk, p_mv,
            tok_major(s_wk_t), tok_major(s_wv_t), u_full[:, t_new:][None])
```

```python
import functools
import math

import numpy as np
import jax
import jax.numpy as jnp
from jax import lax
from jax.experimental import pallas as pl
from jax.experimental.pallas import tpu as pltpu

F32 = jnp.float32
BF16 = jnp.bfloat16

HEAD_DIM = 64
N_ATT_HEADS = 12
ATT_WIDTH = N_ATT_HEADS * HEAD_DIM
PAIR_W = 2 * HEAD_DIM
N_PAIRS = N_ATT_HEADS // 2
CONV_K = 31
CONV_HALO = 32
F32_SUBLANES = 8
DILATED_BRANCHES = ((128, 1), (512, 4), (2048, 16))
STEPS = 128
N_CLS = DILATED_BRANCHES[-1][1]
GATHER_STRIDE = 4
NEW_LANES = 128
N_BUCKETS = 32
MAX_EXACT = N_BUCKETS // 2
MAX_DISTANCE = 2048
X_HEADS = 4
EPS = 1e-6
LOG2E = math.log2(math.e)
ATT_SCALE = HEAD_DIM ** -0.5 * LOG2E
MASKED = -1e30

V7X_VMEM_BYTES = 64 * 1024 * 1024
VMEM_LIMIT = V7X_VMEM_BYTES * 3 // 4


def _params(n_grid_dims):
    return pltpu.CompilerParams(dimension_semantics=("arbitrary",) * n_grid_dims,
                                vmem_limit_bytes=VMEM_LIMIT)


def _rms(x, g):
    return x * lax.rsqrt(jnp.mean(x * x, axis=-1, keepdims=True) + EPS) * g


def _sigmoid(x):
    return 1.0 / (1.0 + jnp.exp(-x))


def _dot(a, b):
    return jnp.dot(a, b, preferred_element_type=F32)


def _dot_nt(a, b):
    return lax.dot_general(a, b, (((1,), (1,)), ((), ())), preferred_element_type=F32)


def _t5_bucket_np(dist):
    n = dist.astype(np.int32)
    nf = np.maximum(n, MAX_EXACT).astype(np.float32)
    large = MAX_EXACT + (np.log(nf / np.float32(MAX_EXACT)) / np.float32(math.log(MAX_DISTANCE / MAX_EXACT))
                         * np.float32(N_BUCKETS - MAX_EXACT)).astype(np.int32)
    large = np.minimum(large, N_BUCKETS - 1)
    return np.where(n < MAX_EXACT, n, large)


def _block_order(dil):
    groups = N_CLS // dil
    per = STEPS // groups
    i = np.arange(STEPS)
    return (i % per) * groups + i // per


def _prompt_bucket_index():
    out = []
    for _, dil in DILATED_BRANCHES:
        n = _block_order(dil)
        kj = np.concatenate([n, STEPS + n])[None, :]
        sub = STEPS + n[:, None] - kj
        band = (sub >= 0) & (sub <= STEPS)
        out.append(np.where(band, _t5_bucket_np(dil * np.maximum(sub, 0)), -1))
    return np.stack(out).astype(np.int32)


def _sample_bucket_index(buf_len, t_new):
    col = np.arange(buf_len + NEW_LANES)
    key = np.where(col < buf_len, (col + t_new) % buf_len, col - NEW_LANES + t_new)
    keep = (col < buf_len) | (col >= buf_len + NEW_LANES - t_new)
    i = np.arange(t_new)[:, None]
    dist = buf_len + i - key[None, :]
    out = []
    for window, dil in DILATED_BRANCHES:
        ok = keep[None, :] & (dist >= 0) & (dist % dil == 0) & (dist // dil <= window // dil)
        out.append(np.where(ok, _t5_bucket_np(np.maximum(dist, 0)), -1))
    return np.stack(out).astype(np.int32)


def _bias_kernel(tab_ref, idx_ref, o_ref, *, masked_cols, buckets):
    pair = pl.program_id(0)
    for g, present in enumerate(buckets):
        idx = idx_ref[g]
        rows = idx.shape[0]
        for half in range(2):
            head = 2 * pair + half
            acc = jnp.full(idx.shape, MASKED, F32)
            for b in present:
                acc = jnp.where(idx == b, tab_ref[b, head] * LOG2E, acc)
            o_ref[0, g, 0, half * rows:(half + 1) * rows, :] = acc
            if masked_cols:
                col = lax.broadcasted_iota(jnp.int32, idx.shape, 1)
                o_ref[1, g, 0, half * rows:(half + 1) * rows, :] = jnp.where(col < masked_cols, MASKED, acc)


def _bias_tables(table, idx, masked_cols=0):
    n_var, rows, cols = idx.shape
    n_out = 2 if masked_cols else 1
    buckets = tuple(tuple(int(b) for b in np.unique(idx[g]) if b >= 0) for g in range(n_var))
    return pl.pallas_call(
        functools.partial(_bias_kernel, masked_cols=masked_cols, buckets=buckets),
        grid=(N_PAIRS,),
        in_specs=[pl.BlockSpec(memory_space=pltpu.SMEM),
                  pl.BlockSpec((n_var, rows, cols), lambda p: (0, 0, 0))],
        out_specs=pl.BlockSpec((n_out, n_var, 1, 2 * rows, cols), lambda p: (0, 0, p, 0, 0)),
        out_shape=jax.ShapeDtypeStruct((n_out, n_var, N_PAIRS, 2 * rows, cols), F32),
        compiler_params=_params(1),
        name="bias_tables",
    )(table, jnp.asarray(idx))


def _in_proj_sample_kernel(x_ref, g_ref, w_ref, q_ref, u_ref, kt_ref, vt_ref):
    h = _rms(x_ref[...], g_ref[...]).astype(BF16)
    aw = q_ref.shape[1]
    cw = u_ref.shape[1]
    q_ref[...] = _dot(h, w_ref[:, 0:aw]) * ATT_SCALE
    a = _dot(h, w_ref[:, 3 * aw:3 * aw + cw])
    gate = _dot(h, w_ref[:, 3 * aw + cw:3 * aw + 2 * cw])
    u_ref[...] = a * _sigmoid(gate)
    kt_ref[...] = _dot(h, w_ref[:, aw:2 * aw]).T
    vt_ref[...] = _dot(h, w_ref[:, 2 * aw:3 * aw]).T


def _in_proj_sample(x, g, w_bf16):
    n, d = x.shape
    aw = ATT_WIDTH
    cw = (w_bf16.shape[1] - 3 * aw) // 2
    new_shape = jax.ShapeDtypeStruct((aw, n), F32)
    return pl.pallas_call(
        _in_proj_sample_kernel,
        out_shape=[jax.ShapeDtypeStruct((n, aw), F32), jax.ShapeDtypeStruct((n, cw), F32), new_shape, new_shape],
        compiler_params=pltpu.CompilerParams(vmem_limit_bytes=VMEM_LIMIT),
        name="in_proj_sample",
    )(x, g, w_bf16)


def _in_proj_prompt_kernel(x_ref, g_ref, w_ref, q_ref, k_ref, v_ref, u_ref, kt_ref, vt_ref, xs, xs2, *,
                           tiles_per_seq, tail_first):
    tm, d = x_ref.shape
    per = tm // N_CLS
    aw = q_ref.shape[3]
    cw = u_ref.shape[1]
    xn = _rms(x_ref[...], g_ref[...])
    hn = xn.astype(BF16)
    gate = _dot(hn, w_ref[:, 3 * aw + cw:3 * aw + 2 * cw])
    u_ref[...] = _dot(hn, w_ref[:, 3 * aw:3 * aw + cw]) * _sigmoid(gate)

    quarter = tm // GATHER_STRIDE
    for c in range(d // PAIR_W):
        xs[c] = xn[:, c * PAIR_W:(c + 1) * PAIR_W]
    for c in range(d // PAIR_W):
        for res in range(GATHER_STRIDE):
            xs2[c, res * quarter:(res + 1) * quarter, :] = xs[c, pl.ds(res, quarter, stride=GATHER_STRIDE), :]

    def class_rows(c, r):
        start = (r % GATHER_STRIDE) * quarter + r // GATHER_STRIDE
        return xs2[c, pl.ds(start, per, stride=GATHER_STRIDE), :]

    h = jnp.concatenate(
        [jnp.concatenate([class_rows(c, r) for r in range(N_CLS)], axis=0) for c in range(d // PAIR_W)],
        axis=1).astype(BF16)
    q_ref[0] = (_dot(h, w_ref[:, 0:aw]) * ATT_SCALE).reshape(N_CLS, per, aw)
    k_ref[0] = _dot(h, w_ref[:, aw:2 * aw]).reshape(N_CLS, per, aw)
    v_ref[0] = _dot(h, w_ref[:, 2 * aw:3 * aw]).reshape(N_CLS, per, aw)

    @pl.when(pl.program_id(0) % tiles_per_seq >= tail_first)
    def _():
        kt_ref[0] = _dot(hn, w_ref[:, aw:2 * aw]).T
        vt_ref[0] = _dot(hn, w_ref[:, 2 * aw:3 * aw]).T


def _in_proj_prompt(x, g, w_bf16, seq, keep, tm):
    n, d = x.shape
    aw = ATT_WIDTH
    cw = (w_bf16.shape[1] - 3 * aw) // 2
    chunk = N_CLS * STEPS
    tiles_per_chunk = chunk // tm
    tiles_per_seq = seq // tm
    tail_first = (seq - keep) // tm
    row = lambda i: (i, 0)
    fixed = lambda i: (0, 0)
    cls = pl.BlockSpec((1, N_CLS, tm // N_CLS, aw), lambda i: (i // tiles_per_chunk, 0, i % tiles_per_chunk, 0))
    tail = pl.BlockSpec((1, aw, tm),
                        lambda i: (i // tiles_per_seq, 0, jnp.maximum(i % tiles_per_seq - tail_first, 0)))
    cls_shape = jax.ShapeDtypeStruct((n // chunk, N_CLS, STEPS, aw), F32)
    tail_shape = jax.ShapeDtypeStruct((n // seq, aw, keep), F32)
    return pl.pallas_call(
        functools.partial(_in_proj_prompt_kernel, tiles_per_seq=tiles_per_seq, tail_first=tail_first),
        grid=(n // tm,),
        in_specs=[pl.BlockSpec((tm, d), row), pl.BlockSpec((1, d), fixed),
                  pl.BlockSpec(w_bf16.shape, fixed)],
        out_specs=[cls, cls, cls, pl.BlockSpec((tm, cw), row), tail, tail],
        out_shape=[cls_shape, cls_shape, cls_shape, jax.ShapeDtypeStruct((n, cw), F32), tail_shape, tail_shape],
        scratch_shapes=[pltpu.VMEM((d // PAIR_W, tm, PAIR_W), F32)] * 2,
        compiler_params=_params(1),
        name="in_proj_prompt",
    )(x, g, w_bf16)


def _stack_heads(x, first):
    zero = jnp.zeros_like(x)
    return jnp.concatenate([jnp.where(first, x, zero), jnp.where(first, zero, x)], axis=0)


def _att_kernel(q_ref, kc_ref, kp_ref, vc_ref, vp_ref, bias_ref, bias0_ref, o_ref, num_s, m_s, l_s, nat_s):
    first = lax.broadcasted_iota(jnp.int32, (STEPS, PAIR_W), 1) < HEAD_DIM
    ones = jnp.ones((2 * STEPS, PAIR_W), BF16)

    def pieces(dil, cls, blk):
        groups = N_CLS // dil
        per = STEPS // groups
        return [(cls + dil * j, pl.ds(blk * per, per)) for j in range(groups)], per

    def load(ref, dil, cls, blk):
        idx, _ = pieces(dil, cls, blk)
        return jnp.concatenate([ref[0, r, rows, :] for r, rows in idx], axis=0)

    def store(ref, g, dil, cls, blk, val):
        idx, per = pieces(dil, cls, blk)
        for j, (r, rows) in enumerate(idx):
            ref[g, r, rows, :] = val[j * per:(j + 1) * per]

    def attend(g, dil, cls, blk, k_prev, v_prev, bias):
        q_st = _stack_heads(load(q_ref, dil, cls, blk), first).astype(BF16)
        k_own = load(kc_ref, dil, cls, blk).astype(BF16)
        v_own = load(vc_ref, dil, cls, blk).astype(BF16)
        kcat = jnp.concatenate([k_prev, k_own], axis=0)
        vcat = jnp.concatenate([v_prev, v_own], axis=0)
        s = _dot_nt(q_st, kcat) + bias
        m = jnp.max(s, axis=-1, keepdims=True)
        p = jnp.exp2(s - m).astype(BF16)
        o = _dot(p, jnp.concatenate([vcat, ones], axis=1))
        store(num_s, g, dil, cls, blk, jnp.where(first, o[:STEPS, :PAIR_W], o[STEPS:, :PAIR_W]))
        store(l_s, g, dil, cls, blk, jnp.where(first, o[:STEPS, PAIR_W:], o[STEPS:, PAIR_W:]))
        store(m_s, g, dil, cls, blk, jnp.where(first, jnp.broadcast_to(m[:STEPS], (STEPS, PAIR_W)),
                                               jnp.broadcast_to(m[STEPS:], (STEPS, PAIR_W))))
        return k_own, v_own

    carried = {}
    for i in range(N_CLS):
        for g, (_, dil) in reversed(list(enumerate(DILATED_BRANCHES))):
            n_blk = N_CLS // dil
            cls, blk = i % dil, i // dil
            if blk == 0:
                prev = (load(kp_ref, dil, cls, n_blk - 1).astype(BF16), load(vp_ref, dil, cls, n_blk - 1).astype(BF16))
                bias = bias0_ref[0, g, 0]
            else:
                prev, bias = carried[g, cls], bias_ref[g, 0]
            carried[g, cls] = attend(g, dil, cls, blk, *prev, bias)

    quarter = N_CLS * STEPS // GATHER_STRIDE
    for r in range(N_CLS):
        m_all = jnp.maximum(jnp.maximum(m_s[0, r], m_s[1, r]), m_s[2, r])
        num = jnp.zeros(m_all.shape, F32)
        den = jnp.zeros(m_all.shape, F32)
        for g in range(len(DILATED_BRANCHES)):
            w = jnp.exp2(m_s[g, r] - m_all)
            num = num + w * num_s[g, r]
            den = den + w * l_s[g, r]
        start = (r % GATHER_STRIDE) * quarter + r // GATHER_STRIDE
        nat_s[pl.ds(start, STEPS, stride=GATHER_STRIDE), :] = num / den
    for res in range(GATHER_STRIDE):
        o_ref[pl.ds(res, quarter, stride=GATHER_STRIDE), :] = nat_s[res * quarter:(res + 1) * quarter, :]


def _sample_att_body(seq, q_ref, kn_ref, vn_ref, ck_ref, cv_ref, bias_ref, ok_ref, ov_ref, o_ref):
    buf_len = ck_ref.shape[2]
    t_new = q_ref.shape[0]
    tail = buf_len - NEW_LANES
    is_new = lax.broadcasted_iota(jnp.int32, (PAIR_W, NEW_LANES), 1) >= NEW_LANES - t_new
    first = lax.broadcasted_iota(jnp.int32, (t_new, PAIR_W), 1) < HEAD_DIM
    col0 = (seq % (NEW_LANES // t_new)) * t_new
    to_end = (NEW_LANES - t_new - col0) % NEW_LANES

    for j in range(ck_ref.shape[1] // PAIR_W):
        ch = slice(j * PAIR_W, (j + 1) * PAIR_W)

        def shift_in(c_ref, n_ref, o_ref):
            rot = pltpu.roll(c_ref[0, ch, :], buf_len - t_new, axis=1)
            new = jnp.where(is_new, pltpu.roll(n_ref[ch, :], to_end, axis=1), 0.0)
            o_ref[0, ch, :tail] = rot[:, :tail]
            o_ref[0, ch, tail:] = jnp.where(is_new, new, rot[:, tail:])
            return rot.astype(BF16), new.astype(BF16)

        rot_k, new_k = shift_in(ck_ref, kn_ref, ok_ref)
        rot_v, new_v = shift_in(cv_ref, vn_ref, ov_ref)

        q_st = _stack_heads(q_ref[:, ch], first).astype(BF16)
        s = jnp.concatenate([_dot(q_st, rot_k), _dot(q_st, new_k)], axis=1)
        sg = [s + bias_ref[g, j] for g in range(len(DILATED_BRANCHES))]
        m = functools.reduce(jnp.maximum, [jnp.max(x, axis=-1, keepdims=True) for x in sg])
        p = functools.reduce(jnp.add, [jnp.exp2(x - m) for x in sg])
        den = jnp.sum(p, axis=-1, keepdims=True)
        pb = p.astype(BF16)
        o = (_dot_nt(pb[:, :buf_len], rot_v) + _dot_nt(pb[:, buf_len:], new_v)) / den
        o_ref[:, ch] = jnp.where(first, o[:t_new], o[t_new:]).astype(o_ref.dtype)


def _att_both_kernel(q_ref, kc_ref, kp_ref, vc_ref, vp_ref, bias_ref, bias0_ref,
                     sq_ref, skn_ref, svn_ref, sck_ref, scv_ref, sbias_ref,
                     o_ref, sok_ref, sov_ref, so_ref, num_s, m_s, l_s, nat_s, *, groups_per_seq):
    step = ((pl.program_id(0) * pl.num_programs(1) + pl.program_id(1)) * pl.num_programs(2) + pl.program_id(2))
    _sample_att_body(step // groups_per_seq, sq_ref, skn_ref, svn_ref, sck_ref, scv_ref, sbias_ref,
                     sok_ref, sov_ref, so_ref)
    _att_kernel(q_ref, kc_ref, kp_ref, vc_ref, vp_ref, bias_ref, bias0_ref, o_ref, num_s, m_s, l_s, nat_s)


def _dilated_attention_both(q, k, v, bias, chunks_per_seq, sq, sk_new_t, sv_new_t, cache_k_t, cache_v_t, sbias, t_new):
    n_chunks = q.shape[0]
    n_br = len(DILATED_BRANCHES)
    nb, _, buf_len = cache_k_t.shape
    n_steps = n_chunks * N_PAIRS
    groups_per_seq = n_steps // nb
    assert nb * groups_per_seq == n_steps and N_PAIRS % groups_per_seq == 0, "sample steps must tile the prompt grid"
    width = N_PAIRS // groups_per_seq * PAIR_W

    cur = lambda b, c, p: (b * chunks_per_seq + c, 0, 0, p)
    prev = lambda b, c, p: (b * chunks_per_seq + jnp.maximum(c - 1, 0), 0, 0, p)
    blk = pl.BlockSpec((1, N_CLS, STEPS, PAIR_W), cur)
    blk_prev = pl.BlockSpec((1, N_CLS, STEPS, PAIR_W), prev)

    step = lambda b, c, p: (b * chunks_per_seq + c) * N_PAIRS + p
    s_seq = lambda b, c, p: step(b, c, p) // groups_per_seq
    s_grp = lambda b, c, p: step(b, c, p) % groups_per_seq
    tok = pl.BlockSpec((t_new, width), lambda b, c, p: (s_seq(b, c, p), s_grp(b, c, p)))
    new = pl.BlockSpec((width, NEW_LANES), lambda b, c, p: (s_grp(b, c, p), s_seq(b, c, p) // (NEW_LANES // t_new)))
    buf = pl.BlockSpec((1, width, buf_len), lambda b, c, p: (s_seq(b, c, p), s_grp(b, c, p), 0))
    return pl.pallas_call(
        functools.partial(_att_both_kernel, groups_per_seq=groups_per_seq),
        grid=(n_chunks // chunks_per_seq, chunks_per_seq, N_PAIRS),
        in_specs=[blk, blk, blk_prev, blk, blk_prev,
                  pl.BlockSpec((n_br, 1, 2 * STEPS, 2 * STEPS), lambda b, c, p: (0, p, 0, 0)),
                  pl.BlockSpec((1, n_br, 1, 2 * STEPS, 2 * STEPS),
                               lambda b, c, p: (jnp.where(c == 0, 1, 0), 0, p, 0, 0)),
                  tok, new, new, buf, buf,
                  pl.BlockSpec((n_br, width // PAIR_W, 2 * t_new, buf_len + NEW_LANES),
                               lambda b, c, p: (0, s_grp(b, c, p), 0, 0))],
        out_specs=[pl.BlockSpec((N_CLS * STEPS, PAIR_W), lambda b, c, p: (b * chunks_per_seq + c, p)), buf, buf, tok],
        out_shape=[jax.ShapeDtypeStruct((n_chunks * N_CLS * STEPS, ATT_WIDTH), F32),
                   jax.ShapeDtypeStruct(cache_k_t.shape, F32), jax.ShapeDtypeStruct(cache_v_t.shape, F32),
                   jax.ShapeDtypeStruct((nb * t_new, ATT_WIDTH), BF16)],
        scratch_shapes=[pltpu.VMEM((n_br, N_CLS, STEPS, PAIR_W), F32)] * 3 + [pltpu.VMEM((N_CLS * STEPS, PAIR_W), F32)],
        compiler_params=_params(3),
        name="dilated_attention",
    )(q, k, k, v, v, bias[0], bias, sq, sk_new_t, sv_new_t, cache_k_t, cache_v_t, sbias)


def _conv_tail(y, cb_ref, lg_ref, lb_ref):
    y = y + cb_ref[...]
    mu = jnp.mean(y, axis=-1, keepdims=True)
    yc = y - mu
    var = jnp.mean(yc * yc, axis=-1, keepdims=True)
    yn = yc * lax.rsqrt(var + EPS) * lg_ref[...] + lb_ref[...]
    return yn * _sigmoid(yn)


def _mix_out(x, att, c, wo_ref):
    aw = att.shape[1]
    return x + _dot(att.astype(BF16), wo_ref[:aw, :]) + _dot(c.astype(BF16), wo_ref[aw:, :])


def _cross_attend(qx, head_kv):
    hd = qx.shape[1] // X_HEADS
    outs = []
    for h in range(X_HEADS):
        mk, mv = head_kv(h)
        s = _dot_nt(qx[:, h * hd:(h + 1) * hd].astype(BF16), mk)
        p = jnp.exp2(s - jnp.max(s, axis=-1, keepdims=True))
        den = jnp.sum(p, axis=-1, keepdims=True)
        outs.append((_dot(p.astype(BF16), mv) / den).astype(BF16))
    return jnp.concatenate(outs, axis=1)


def _prompt_post_kernel(x_ref, att_ref, u_ref, uh_ref, cw_ref, cb_ref, lg_ref, lb_ref, wo_ref,
                        gx_ref, wq_ref, mk_ref, mv_ref, wxo_ref, o_ref, ubuf, *, tiles_per_seq):
    tm = x_ref.shape[0]
    seq_start = (pl.program_id(0) % tiles_per_seq) == 0
    ubuf[:CONV_HALO, :] = jnp.where(seq_start, 0.0, uh_ref[...])
    ubuf[CONV_HALO:, :] = u_ref[...]
    hist = ubuf[...]
    n_rows = hist.shape[0]
    first_tap = CONV_HALO - (CONV_K - 1)
    y = jnp.zeros(u_ref.shape, F32)
    for s in range(F32_SUBLANES):
        shifted = hist if s == 0 else pltpu.roll(hist, n_rows - s, axis=0)
        for j in range(CONV_K):
            if (first_tap + j) % F32_SUBLANES == s:
                lo = first_tap + j - s
                y = y + cw_ref[j:j + 1, :] * shifted[lo:lo + tm]
    c = _conv_tail(y, cb_ref, lg_ref, lb_ref)
    x1 = _mix_out(x_ref[...], att_ref[...], c, wo_ref)
    hd = wq_ref.shape[1] // X_HEADS
    qx = _dot(_rms(x1, gx_ref[...]).astype(BF16), wq_ref[...]) * (hd ** -0.5 * LOG2E)
    head_kv = lambda h: (mk_ref[:, h * hd:(h + 1) * hd], mv_ref[:, h * hd:(h + 1) * hd])
    o_ref[...] = x1 + _dot(_cross_attend(qx, head_kv), wxo_ref[...])


def _prompt_post(x, att, u, conv_w, conv_b, ln_g, ln_b, w_out, gx, w_xq, mk, mv, w_xo, seq, tm):
    n, d = x.shape
    cw = u.shape[1]
    n_mem = mk.shape[0] // (n // seq)
    tiles_per_seq = seq // tm
    row = lambda i: (i, 0)
    fixed = lambda i: (0, 0)
    halo = lambda i: (jnp.maximum(i * (tm // CONV_HALO) - 1, 0), 0)
    per_seq = lambda i: (i // tiles_per_seq, 0)
    full = _resident
    return pl.pallas_call(
        functools.partial(_prompt_post_kernel, tiles_per_seq=tiles_per_seq),
        grid=(n // tm,),
        in_specs=[pl.BlockSpec((tm, d), row), pl.BlockSpec((tm, ATT_WIDTH), row),
                  pl.BlockSpec((tm, cw), row), pl.BlockSpec((CONV_HALO, cw), halo),
                  full(conv_w), full(conv_b), full(ln_g), full(ln_b), full(w_out),
                  full(gx), full(w_xq),
                  pl.BlockSpec((n_mem, d), per_seq), pl.BlockSpec((n_mem, d), per_seq), full(w_xo)],
        out_specs=pl.BlockSpec((tm, d), row),
        out_shape=jax.ShapeDtypeStruct((n, d), F32),
        scratch_shapes=[pltpu.VMEM((CONV_HALO + tm, cw), F32)],
        compiler_params=_params(1),
        name="prompt_mix_cross",
    )(x, att, u, u, conv_w, conv_b, ln_g, ln_b, w_out, gx, w_xq, mk, mv, w_xo)


def _sample_mix_kernel(x_ref, att_ref, uf_ref, cw_ref, cb_ref, lg_ref, lb_ref, wo_ref, gx_ref, wq_ref,
                       x1_ref, qx_ref):
    nb, t_full, cw = uf_ref.shape
    t_new = t_full - (CONV_K - 1)
    y = jnp.zeros((nb, t_new, cw), F32)
    for j in range(CONV_K):
        y = y + cw_ref[j:j + 1, :] * uf_ref[:, j:j + t_new, :]
    c = _conv_tail(y.reshape(nb * t_new, cw), cb_ref, lg_ref, lb_ref)
    x1 = _mix_out(x_ref[...], att_ref[...], c, wo_ref)
    x1_ref[...] = x1
    hd = wq_ref.shape[1] // X_HEADS
    qx_ref[...] = _dot(_rms(x1, gx_ref[...]).astype(BF16), wq_ref[...]) * (hd ** -0.5 * LOG2E)


def _sample_mix(x, att, u_full, conv_w, conv_b, ln_g, ln_b, w_out, gx, w_xq):
    n, d = x.shape
    return pl.pallas_call(
        _sample_mix_kernel,
        out_shape=[jax.ShapeDtypeStruct((n, d), F32), jax.ShapeDtypeStruct((n, w_xq.shape[1]), F32)],
        compiler_params=pltpu.CompilerParams(vmem_limit_bytes=VMEM_LIMIT),
        name="sample_mix",
    )(x, att, u_full, conv_w, conv_b, ln_g, ln_b, w_out, gx, w_xq)


def _sample_cross_kernel(qx_ref, mk_ref, mv_ref, o_ref, *, n_mem):
    n_seq = mk_ref.shape[0]
    t_new = qx_ref.shape[0] // n_seq
    groups = mk_ref.shape[1] // (n_mem * X_HEADS)
    pitch = groups * X_HEADS

    def head_rows(ref, i, h):
        return jnp.concatenate([ref[i, pl.ds(j * X_HEADS + h, n_mem, stride=pitch), :] for j in range(groups)],
                               axis=1).astype(BF16)

    for i in range(n_seq):
        tok = slice(i * t_new, (i + 1) * t_new)
        o_ref[tok, :] = _cross_attend(qx_ref[tok, :],
                                      lambda h, i=i: (head_rows(mk_ref, i, h), head_rows(mv_ref, i, h)))


SAMPLE_CROSS_SEQS_PER_STEP = 4


def _sample_cross(qx, mem_k, mem_v, t_new):
    nb, n_mem, n_heads, hd = mem_k.shape
    sps = SAMPLE_CROSS_SEQS_PER_STEP
    rows = lambda a: jnp.transpose(a.reshape(nb, n_mem, n_heads, hd // PAIR_W, PAIR_W),
                                   (0, 1, 3, 2, 4)).reshape(nb, -1, PAIR_W)
    tok = pl.BlockSpec((sps * t_new, n_heads * hd), lambda n: (n, 0))
    mem = pl.BlockSpec((sps, n_mem * n_heads * hd // PAIR_W, PAIR_W), lambda n: (n, 0, 0))
    return pl.pallas_call(
        functools.partial(_sample_cross_kernel, n_mem=n_mem),
        grid=(nb // sps,),
        in_specs=[tok, mem, mem],
        out_specs=tok,
        out_shape=jax.ShapeDtypeStruct(qx.shape, BF16),
        compiler_params=_params(1),
        name="sample_cross",
    )(qx, rows(mem_k), rows(mem_v))


def _swiglu_final(x2, gf_ref, wg_ref, wu_ref, wd_ref, gfin_ref, ff_chunk):
    h = _rms(x2, gf_ref[...]).astype(BF16)
    acc = x2
    for lo in range(0, wg_ref.shape[1], ff_chunk):
        gate = _dot(h, wg_ref[:, lo:lo + ff_chunk])
        up = _dot(h, wu_ref[:, lo:lo + ff_chunk])
        acc = acc + _dot((gate * _sigmoid(gate) * up).astype(BF16), wd_ref[lo:lo + ff_chunk, :])
    return _rms(acc, gfin_ref[...])


def _ffn_kernel(x_ref, gf_ref, wg_ref, wu_ref, wd_ref, gfin_ref, o_ref, *, ff_chunk):
    o_ref[...] = _swiglu_final(x_ref[...], gf_ref, wg_ref, wu_ref, wd_ref, gfin_ref, ff_chunk)


def _sample_tail_kernel(x_ref, xo_ref, wxo_ref, gf_ref, wg_ref, wu_ref, wd_ref, gfin_ref, o_ref, *, ff_chunk):
    x2 = x_ref[...] + _dot(xo_ref[...], wxo_ref[...])
    o_ref[...] = _swiglu_final(x2, gf_ref, wg_ref, wu_ref, wd_ref, gfin_ref, ff_chunk)


FF_CHUNK = 256


def _resident(a):
    return pl.BlockSpec(a.shape, lambda *_: (0,) * a.ndim, pipeline_mode=pl.Buffered(1))


def _ffn(x, gf, wg, wu, wd, gfin, tm):
    n, d = x.shape
    row = lambda i: (i, 0)
    full = _resident
    return pl.pallas_call(
        functools.partial(_ffn_kernel, ff_chunk=FF_CHUNK),
        grid=(n // tm,),
        in_specs=[pl.BlockSpec((tm, d), row), full(gf), full(wg), full(wu), full(wd), full(gfin)],
        out_specs=pl.BlockSpec((tm, d), row),
        out_shape=jax.ShapeDtypeStruct((n, d), F32),
        compiler_params=_params(1),
        name="swiglu_final_norm",
    )(x, gf, wg, wu, wd, gfin)


def _sample_tail(x1, xo, w_xo, gf, wg, wu, wd, gfin):
    return pl.pallas_call(
        functools.partial(_sample_tail_kernel, ff_chunk=FF_CHUNK),
        out_shape=jax.ShapeDtypeStruct(x1.shape, F32),
        compiler_params=pltpu.CompilerParams(vmem_limit_bytes=VMEM_LIMIT),
        name="sample_cross_out_swiglu",
    )(x1, xo, w_xo, gf, wg, wu, wd, gfin)


def _mem_kv_kernel(m_ref, g_ref, wk_ref, wv_ref, k_ref, v_ref, kb_ref, vb_ref):
    n_mem = m_ref.shape[0]
    hd = wk_ref.shape[1] // X_HEADS
    groups = hd // PAIR_W
    h = _rms(m_ref[...], g_ref[...]).astype(BF16)
    for w_ref, o_ref, ob_ref in ((wk_ref, k_ref, kb_ref), (wv_ref, v_ref, vb_ref)):
        val = _dot(h, w_ref[...])
        ob_ref[...] = val.astype(BF16)
        for head in range(X_HEADS):
            for j in range(groups):
                lanes = slice(head * hd + j * PAIR_W, head * hd + (j + 1) * PAIR_W)
                o_ref[0, pl.ds(j * X_HEADS + head, n_mem, stride=groups * X_HEADS), :] = val[:, lanes]


def _mem_kv(mem, g, w_k, w_v, n_mem):
    n, d = mem.shape
    row = lambda i: (i, 0)
    full = lambda a: pl.BlockSpec(a.shape, lambda i: (0, 0))
    dk = w_k.shape[1]
    rows_spec = pl.BlockSpec((1, n_mem * dk // PAIR_W, PAIR_W), lambda i: (i, 0, 0))
    rows_shape = jax.ShapeDtypeStruct((n // n_mem, n_mem * dk // PAIR_W, PAIR_W), F32)
    return pl.pallas_call(
        _mem_kv_kernel,
        grid=(n // n_mem,),
        in_specs=[pl.BlockSpec((n_mem, d), row), full(g), full(w_k), full(w_v)],
        out_specs=[rows_spec, rows_spec, pl.BlockSpec((n_mem, dk), row), pl.BlockSpec((n_mem, dk), row)],
        out_shape=[rows_shape, rows_shape] + [jax.ShapeDtypeStruct((n, dk), BF16)] * 2,
        compiler_params=_params(1),
        name="mem_kv",
    )(mem, g, w_k, w_v)


ROW_TILE = 512
FFN_ROW_TILE = 1024
ATT_CHUNK = STEPS * DILATED_BRANCHES[-1][1]


def kernel(x_prompt, x_sample, mem_prompt, cache_win_k, cache_win_v, cache_conv, cache_mem_k, cache_mem_v,
           rpb_table, norm_mix_g, w_in, conv_w, conv_b, conv_ln_g, conv_ln_b, w_out, norm_x_g, norm_mem_g,
           w_xq, w_xk, w_xv, w_xo, norm_ffn_g, w_ffn_gate, w_ffn_up, w_ffn_down, norm_final_g):
    depth = w_in.shape[0]
    assert depth == 1, "single-layer stack"
    batch, seq, d = x_prompt.shape
    nb, t_new, _ = x_sample.shape
    buf_len = cache_win_k.shape[2]
    keep_p = min(MAX_DISTANCE, seq)
    n_mem = mem_prompt.shape[1]
    conv_hist = CONV_K - 1
    assert seq % ATT_CHUNK == 0 and keep_p % ATT_CHUNK == 0 and buf_len == MAX_DISTANCE

    row = lambda a: a.reshape(1, -1)
    bf = lambda a: a.astype(BF16)
    l = 0
    w_in_b, w_out_b = bf(w_in[l]), bf(w_out[l])
    w_xq_b, w_xk_b, w_xv_b, w_xo_b = bf(w_xq[l]), bf(w_xk[l]), bf(w_xv[l]), bf(w_xo[l])
    w_g_b, w_u_b, w_d_b = bf(w_ffn_gate[l]), bf(w_ffn_up[l]), bf(w_ffn_down[l])
    g_mix, g_x, g_mem, g_ffn, g_fin = (row(norm_mix_g[l]), row(norm_x_g[l]), row(norm_mem_g[l]),
                                       row(norm_ffn_g[l]), row(norm_final_g))
    cv_w, cv_b, ln_g, ln_b = conv_w[l], row(conv_b[l]), row(conv_ln_g[l]), row(conv_ln_b[l])

    bias_p = _bias_tables(rpb_table, _prompt_bucket_index(), masked_cols=STEPS)
    bias_s = _bias_tables(rpb_table, _sample_bucket_index(buf_len, t_new))[0]

    xp = x_prompt.reshape(batch * seq, d)
    xs = x_sample.reshape(nb * t_new, d)
    q, k, v, u, p_wk_t, p_wv_t = _in_proj_prompt(xp, g_mix, w_in_b, seq, keep_p, ROW_TILE)
    qs, us, ks_t, vs_t = _in_proj_sample(xs, g_mix, w_in_b)
    chan_major = lambda a: jnp.transpose(a, (0, 2, 3, 1)).reshape(nb, ATT_WIDTH, -1)
    att, s_wk_t, s_wv_t, att_s = _dilated_attention_both(
        q, k, v, bias_p, seq // ATT_CHUNK,
        qs, ks_t, vs_t, chan_major(cache_win_k[l]), chan_major(cache_win_v[l]), bias_s, t_new)

    mk, mv, mk_b, mv_b = _mem_kv(mem_prompt.reshape(batch * n_mem, d), g_mem, w_xk_b, w_xv_b, n_mem)
    x2 = _prompt_post(xp, att, u, cv_w, cv_b, ln_g, ln_b, w_out_b, g_x, w_xq_b, mk_b, mv_b, w_xo_b,
                      seq, FFN_ROW_TILE)
    y_prompt = _ffn(x2, g_ffn, w_g_b, w_u_b, w_d_b, g_fin, FFN_ROW_TILE).reshape(batch, seq, d)

    tok_major = lambda a: jnp.transpose(a.reshape(a.shape[0], N_ATT_HEADS, HEAD_DIM, -1), (0, 3, 1, 2))[None]
    p_wk, p_wv = tok_major(p_wk_t), tok_major(p_wv_t)
    u3 = u.reshape(batch, seq, -1)
    p_conv = u3[:, seq - conv_hist:][None]
    xh = lambda a: jnp.transpose(a.reshape(batch, n_mem, -1, X_HEADS, PAIR_W), (0, 1, 3, 2, 4)).reshape(
        1, batch, n_mem, X_HEADS, -1)
    p_mk, p_mv = xh(mk), xh(mv)

    u_full = jnp.concatenate([cache_conv[l], us.reshape(nb, t_new, -1)], axis=1)
    x1s, qxs = _sample_mix(xs, att_s, u_full, cv_w, cv_b, ln_g, ln_b, w_out_b, g_x, w_xq_b)
    xo_s = _sample_cross(qxs, cache_mem_k[l], cache_mem_v[l], t_new)
    y_sample = _sample_tail(x1s, xo_s, w_xo_b, g_ffn, w_g_b, w_u_b, w_d_b, g_fin).reshape(nb, t_new, d)

    return (y_prompt, y_sample, p_wk, p_wv, p_conv, p_mk, p_mv,
            tok_major(s_wk_t), tok_major(s_wv_t), u_full[:, t_new:][None])
```

```python
import functools
import math

import numpy as np
import jax
import jax.numpy as jnp
from jax import lax
from jax.experimental import pallas as pl
from jax.experimental.pallas import tpu as pltpu

F32 = jnp.float32
BF16 = jnp.bfloat16

HEAD_DIM = 64
N_ATT_HEADS = 12
ATT_WIDTH = N_ATT_HEADS * HEAD_DIM
PAIR_W = 2 * HEAD_DIM
N_PAIRS = N_ATT_HEADS // 2
CONV_K = 31
CONV_HALO = 32
F32_SUBLANES = 8
DILATED_BRANCHES = ((128, 1), (512, 4), (2048, 16))
STEPS = 128
N_CLS = DILATED_BRANCHES[-1][1]
GATHER_STRIDE = 4
NEW_LANES = 128
N_BUCKETS = 32
MAX_EXACT = N_BUCKETS // 2
MAX_DISTANCE = 2048
X_HEADS = 4
EPS = 1e-6
LOG2E = math.log2(math.e)
ATT_SCALE = HEAD_DIM ** -0.5 * LOG2E
MASKED = -1e30

V7X_VMEM_BYTES = 64 * 1024 * 1024
VMEM_LIMIT = V7X_VMEM_BYTES * 3 // 4


def _params(n_grid_dims):
    return pltpu.CompilerParams(dimension_semantics=("arbitrary",) * n_grid_dims,
                                vmem_limit_bytes=VMEM_LIMIT)


def _rms(x, g):
    return x * lax.rsqrt(jnp.mean(x * x, axis=-1, keepdims=True) + EPS) * g


def _sigmoid(x):
    return 1.0 / (1.0 + jnp.exp(-x))


def _dot(a, b):
    return jnp.dot(a, b, preferred_element_type=F32)


def _dot_nt(a, b):
    return lax.dot_general(a, b, (((1,), (1,)), ((), ())), preferred_element_type=F32)


def _t5_bucket_np(dist):
    n = dist.astype(np.int32)
    nf = np.maximum(n, MAX_EXACT).astype(np.float32)
    large = MAX_EXACT + (np.log(nf / np.float32(MAX_EXACT)) / np.float32(math.log(MAX_DISTANCE / MAX_EXACT))
                         * np.float32(N_BUCKETS - MAX_EXACT)).astype(np.int32)
    large = np.minimum(large, N_BUCKETS - 1)
    return np.where(n < MAX_EXACT, n, large)


def _block_order(dil):
    groups = N_CLS // dil
    per = STEPS // groups
    i = np.arange(STEPS)
    return (i % per) * groups + i // per


def _prompt_bucket_index():
    out = []
    for _, dil in DILATED_BRANCHES:
        n = _block_order(dil)
        kj = np.concatenate([n, STEPS + n])[None, :]
        sub = STEPS + n[:, None] - kj
        band = (sub >= 0) & (sub <= STEPS)
        out.append(np.where(band, _t5_bucket_np(dil * np.maximum(sub, 0)), -1))
    return np.stack(out).astype(np.int32)


def _sample_bucket_index(buf_len, t_new):
    col = np.arange(buf_len + NEW_LANES)
    key = np.where(col < buf_len, (col + t_new) % buf_len, col - NEW_LANES + t_new)
    keep = (col < buf_len) | (col >= buf_len + NEW_LANES - t_new)
    i = np.arange(t_new)[:, None]
    dist = buf_len + i - key[None, :]
    out = []
    for window, dil in DILATED_BRANCHES:
        ok = keep[None, :] & (dist >= 0) & (dist % dil == 0) & (dist // dil <= window // dil)
        out.append(np.where(ok, _t5_bucket_np(np.maximum(dist, 0)), -1))
    return np.stack(out).astype(np.int32)


def _bias_kernel(tab_ref, idx_ref, o_ref, *, masked_cols, buckets):
    pair = pl.program_id(0)
    for g, present in enumerate(buckets):
        idx = idx_ref[g]
        rows = idx.shape[0]
        for half in range(2):
            head = 2 * pair + half
            acc = jnp.full(idx.shape, MASKED, F32)
            for b in present:
                acc = jnp.where(idx == b, tab_ref[b, head] * LOG2E, acc)
            o_ref[0, g, 0, half * rows:(half + 1) * rows, :] = acc
            if masked_cols:
                col = lax.broadcasted_iota(jnp.int32, idx.shape, 1)
                o_ref[1, g, 0, half * rows:(half + 1) * rows, :] = jnp.where(col < masked_cols, MASKED, acc)


def _bias_tables(table, idx, masked_cols=0):
    n_var, rows, cols = idx.shape
    n_out = 2 if masked_cols else 1
    buckets = tuple(tuple(int(b) for b in np.unique(idx[g]) if b >= 0) for g in range(n_var))
    return pl.pallas_call(
        functools.partial(_bias_kernel, masked_cols=masked_cols, buckets=buckets),
        grid=(N_PAIRS,),
        in_specs=[pl.BlockSpec(memory_space=pltpu.SMEM),
                  pl.BlockSpec((n_var, rows, cols), lambda p: (0, 0, 0))],
        out_specs=pl.BlockSpec((n_out, n_var, 1, 2 * rows, cols), lambda p: (0, 0, p, 0, 0)),
        out_shape=jax.ShapeDtypeStruct((n_out, n_var, N_PAIRS, 2 * rows, cols), F32),
        compiler_params=_params(1),
        name="bias_tables",
    )(table, jnp.asarray(idx))


def _in_proj_sample_kernel(x_ref, g_ref, w_ref, q_ref, u_ref, kt_ref, vt_ref):
    h = _rms(x_ref[...], g_ref[...]).astype(BF16)
    aw = q_ref.shape[1]
    cw = u_ref.shape[1]
    q_ref[...] = _dot(h, w_ref[:, 0:aw]) * ATT_SCALE
    a = _dot(h, w_ref[:, 3 * aw:3 * aw + cw])
    gate = _dot(h, w_ref[:, 3 * aw + cw:3 * aw + 2 * cw])
    u_ref[...] = a * _sigmoid(gate)
    kt_ref[...] = _dot(h, w_ref[:, aw:2 * aw]).T
    vt_ref[...] = _dot(h, w_ref[:, 2 * aw:3 * aw]).T


def _in_proj_sample(x, g, w_bf16):
    n, d = x.shape
    aw = ATT_WIDTH
    cw = (w_bf16.shape[1] - 3 * aw) // 2
    new_shape = jax.ShapeDtypeStruct((aw, n), F32)
    return pl.pallas_call(
        _in_proj_sample_kernel,
        out_shape=[jax.ShapeDtypeStruct((n, aw), F32), jax.ShapeDtypeStruct((n, cw), F32), new_shape, new_shape],
        compiler_params=pltpu.CompilerParams(vmem_limit_bytes=VMEM_LIMIT),
        name="in_proj_sample",
    )(x, g, w_bf16)


def _in_proj_prompt_kernel(*refs, tiles_per_seq, tail_first, side_blocks):
    n_side = len(side_blocks)
    x_ref, g_ref, w_ref = refs[:3]
    side_in = refs[3:3 + n_side]
    q_ref, k_ref, v_ref, u_ref, kt_ref, vt_ref = refs[3 + n_side:9 + n_side]
    side_out = refs[9 + n_side:9 + 2 * n_side]
    xs, xs2 = refs[9 + 2 * n_side:]
    for src, dst, n_blk in zip(side_in, side_out, side_blocks):
        @pl.when(pl.program_id(0) < n_blk)
        def _(src=src, dst=dst):
            dst[...] = src[...].astype(dst.dtype)

    tm, d = x_ref.shape
    per = tm // N_CLS
    aw = q_ref.shape[3]
    cw = u_ref.shape[1]
    xn = _rms(x_ref[...], g_ref[...])
    hn = xn.astype(BF16)
    gate = _dot(hn, w_ref[:, 3 * aw + cw:3 * aw + 2 * cw])
    u_ref[...] = _dot(hn, w_ref[:, 3 * aw:3 * aw + cw]) * _sigmoid(gate)

    quarter = tm // GATHER_STRIDE
    for c in range(d // PAIR_W):
        xs[c] = xn[:, c * PAIR_W:(c + 1) * PAIR_W]
    for c in range(d // PAIR_W):
        for res in range(GATHER_STRIDE):
            xs2[c, res * quarter:(res + 1) * quarter, :] = xs[c, pl.ds(res, quarter, stride=GATHER_STRIDE), :]

    def class_rows(c, r):
        start = (r % GATHER_STRIDE) * quarter + r // GATHER_STRIDE
        return xs2[c, pl.ds(start, per, stride=GATHER_STRIDE), :]

    h = jnp.concatenate(
        [jnp.concatenate([class_rows(c, r) for r in range(N_CLS)], axis=0) for c in range(d // PAIR_W)],
        axis=1).astype(BF16)
    q_ref[0] = (_dot(h, w_ref[:, 0:aw]) * ATT_SCALE).reshape(N_CLS, per, aw)
    k_ref[0] = _dot(h, w_ref[:, aw:2 * aw]).reshape(N_CLS, per, aw)
    v_ref[0] = _dot(h, w_ref[:, 2 * aw:3 * aw]).reshape(N_CLS, per, aw)

    @pl.when(pl.program_id(0) % tiles_per_seq >= tail_first)
    def _():
        kt_ref[0] = _dot(hn, w_ref[:, aw:2 * aw]).T
        vt_ref[0] = _dot(hn, w_ref[:, 2 * aw:3 * aw]).T


BF16_SUBLANES = 16


def _side_block_rows(rows, n_steps):
    return next(br for br in range(BF16_SUBLANES, rows + 1, BF16_SUBLANES)
                if rows % br == 0 and rows // br <= n_steps)


def _in_proj_prompt(x, g, w_bf16, side_weights, seq, keep, tm):
    n, d = x.shape
    n_steps = n // tm
    side_rows = [_side_block_rows(w.shape[0], n_steps) for w in side_weights]
    side_blocks = tuple(w.shape[0] // br for w, br in zip(side_weights, side_rows))
    side_specs = [pl.BlockSpec((br, w.shape[1]), lambda i, last=nb - 1: (jnp.minimum(i, last), 0))
                  for w, br, nb in zip(side_weights, side_rows, side_blocks)]
    aw = ATT_WIDTH
    cw = (w_bf16.shape[1] - 3 * aw) // 2
    chunk = N_CLS * STEPS
    tiles_per_chunk = chunk // tm
    tiles_per_seq = seq // tm
    tail_first = (seq - keep) // tm
    row = lambda i: (i, 0)
    fixed = lambda i: (0, 0)
    cls = pl.BlockSpec((1, N_CLS, tm // N_CLS, aw), lambda i: (i // tiles_per_chunk, 0, i % tiles_per_chunk, 0))
    tail = pl.BlockSpec((1, aw, tm),
                        lambda i: (i // tiles_per_seq, 0, jnp.maximum(i % tiles_per_seq - tail_first, 0)))
    cls_shape = jax.ShapeDtypeStruct((n // chunk, N_CLS, STEPS, aw), F32)
    tail_shape = jax.ShapeDtypeStruct((n // seq, aw, keep), F32)
    return pl.pallas_call(
        functools.partial(_in_proj_prompt_kernel, tiles_per_seq=tiles_per_seq, tail_first=tail_first,
                          side_blocks=side_blocks),
        grid=(n_steps,),
        in_specs=[pl.BlockSpec((tm, d), row), pl.BlockSpec((1, d), fixed),
                  pl.BlockSpec(w_bf16.shape, fixed)] + side_specs,
        out_specs=[cls, cls, cls, pl.BlockSpec((tm, cw), row), tail, tail] + side_specs,
        out_shape=[cls_shape, cls_shape, cls_shape, jax.ShapeDtypeStruct((n, cw), F32), tail_shape, tail_shape]
                  + [jax.ShapeDtypeStruct(w.shape, BF16) for w in side_weights],
        scratch_shapes=[pltpu.VMEM((d // PAIR_W, tm, PAIR_W), F32)] * 2,
        compiler_params=_params(1),
        name="in_proj_prompt",
    )(x, g, w_bf16, *side_weights)


def _stack_heads(x, first):
    zero = jnp.zeros_like(x)
    return jnp.concatenate([jnp.where(first, x, zero), jnp.where(first, zero, x)], axis=0)


def _att_kernel(q_ref, kc_ref, kp_ref, vc_ref, vp_ref, bias_ref, bias0_ref, o_ref, num_s, m_s, l_s, nat_s):
    first = lax.broadcasted_iota(jnp.int32, (STEPS, PAIR_W), 1) < HEAD_DIM
    ones = jnp.ones((2 * STEPS, PAIR_W), BF16)

    def pieces(dil, cls, blk):
        groups = N_CLS // dil
        per = STEPS // groups
        return [(cls + dil * j, pl.ds(blk * per, per)) for j in range(groups)], per

    def load(ref, dil, cls, blk):
        idx, _ = pieces(dil, cls, blk)
        return jnp.concatenate([ref[0, r, rows, :] for r, rows in idx], axis=0)

    def store(ref, g, dil, cls, blk, val):
        idx, per = pieces(dil, cls, blk)
        for j, (r, rows) in enumerate(idx):
            ref[g, r, rows, :] = val[j * per:(j + 1) * per]

    def attend(g, dil, cls, blk, k_prev, v_prev, bias):
        q_st = _stack_heads(load(q_ref, dil, cls, blk), first).astype(BF16)
        k_own = load(kc_ref, dil, cls, blk).astype(BF16)
        v_own = load(vc_ref, dil, cls, blk).astype(BF16)
        kcat = jnp.concatenate([k_prev, k_own], axis=0)
        vcat = jnp.concatenate([v_prev, v_own], axis=0)
        s = _dot_nt(q_st, kcat) + bias
        m = jnp.max(s, axis=-1, keepdims=True)
        p = jnp.exp2(s - m).astype(BF16)
        o = _dot(p, jnp.concatenate([vcat, ones], axis=1))
        store(num_s, g, dil, cls, blk, jnp.where(first, o[:STEPS, :PAIR_W], o[STEPS:, :PAIR_W]))
        store(l_s, g, dil, cls, blk, jnp.where(first, o[:STEPS, PAIR_W:], o[STEPS:, PAIR_W:]))
        store(m_s, g, dil, cls, blk, jnp.where(first, jnp.broadcast_to(m[:STEPS], (STEPS, PAIR_W)),
                                               jnp.broadcast_to(m[STEPS:], (STEPS, PAIR_W))))
        return k_own, v_own

    carried = {}
    for i in range(N_CLS):
        for g, (_, dil) in reversed(list(enumerate(DILATED_BRANCHES))):
            n_blk = N_CLS // dil
            cls, blk = i % dil, i // dil
            if blk == 0:
                prev = (load(kp_ref, dil, cls, n_blk - 1).astype(BF16), load(vp_ref, dil, cls, n_blk - 1).astype(BF16))
                bias = bias0_ref[0, g, 0]
            else:
                prev, bias = carried[g, cls], bias_ref[g, 0]
            carried[g, cls] = attend(g, dil, cls, blk, *prev, bias)

    quarter = N_CLS * STEPS // GATHER_STRIDE
    for r in range(N_CLS):
        m_all = jnp.maximum(jnp.maximum(m_s[0, r], m_s[1, r]), m_s[2, r])
        num = jnp.zeros(m_all.shape, F32)
        den = jnp.zeros(m_all.shape, F32)
        for g in range(len(DILATED_BRANCHES)):
            w = jnp.exp2(m_s[g, r] - m_all)
            num = num + w * num_s[g, r]
            den = den + w * l_s[g, r]
        start = (r % GATHER_STRIDE) * quarter + r // GATHER_STRIDE
        nat_s[pl.ds(start, STEPS, stride=GATHER_STRIDE), :] = num / den
    for res in range(GATHER_STRIDE):
        o_ref[pl.ds(res, quarter, stride=GATHER_STRIDE), :] = nat_s[res * quarter:(res + 1) * quarter, :]


def _sample_att_body(seq, q_ref, kn_ref, vn_ref, ck_ref, cv_ref, bias_ref, ok_ref, ov_ref, o_ref):
    buf_len = ck_ref.shape[2]
    t_new = q_ref.shape[0]
    tail = buf_len - NEW_LANES
    is_new = lax.broadcasted_iota(jnp.int32, (PAIR_W, NEW_LANES), 1) >= NEW_LANES - t_new
    first = lax.broadcasted_iota(jnp.int32, (t_new, PAIR_W), 1) < HEAD_DIM
    col0 = (seq % (NEW_LANES // t_new)) * t_new
    to_end = (NEW_LANES - t_new - col0) % NEW_LANES

    for j in range(ck_ref.shape[1] // PAIR_W):
        ch = slice(j * PAIR_W, (j + 1) * PAIR_W)

        def shift_in(c_ref, n_ref, o_ref):
            rot = pltpu.roll(c_ref[0, ch, :], buf_len - t_new, axis=1)
            new = jnp.where(is_new, pltpu.roll(n_ref[ch, :], to_end, axis=1), 0.0)
            o_ref[0, ch, :tail] = rot[:, :tail]
            o_ref[0, ch, tail:] = jnp.where(is_new, new, rot[:, tail:])
            return rot.astype(BF16), new.astype(BF16)

        rot_k, new_k = shift_in(ck_ref, kn_ref, ok_ref)
        rot_v, new_v = shift_in(cv_ref, vn_ref, ov_ref)

        q_st = _stack_heads(q_ref[:, ch], first).astype(BF16)
        s = jnp.concatenate([_dot(q_st, rot_k), _dot(q_st, new_k)], axis=1)
        sg = [s + bias_ref[g, j] for g in range(len(DILATED_BRANCHES))]
        m = functools.reduce(jnp.maximum, [jnp.max(x, axis=-1, keepdims=True) for x in sg])
        p = functools.reduce(jnp.add, [jnp.exp2(x - m) for x in sg])
        den = jnp.sum(p, axis=-1, keepdims=True)
        pb = p.astype(BF16)
        o = (_dot_nt(pb[:, :buf_len], rot_v) + _dot_nt(pb[:, buf_len:], new_v)) / den
        o_ref[:, ch] = jnp.where(first, o[:t_new], o[t_new:]).astype(o_ref.dtype)


def _att_both_kernel(q_ref, kc_ref, kp_ref, vc_ref, vp_ref, bias_ref, bias0_ref,
                     sq_ref, skn_ref, svn_ref, sck_ref, scv_ref, sbias_ref,
                     o_ref, sok_ref, sov_ref, so_ref, num_s, m_s, l_s, nat_s, *, groups_per_seq):
    step = ((pl.program_id(0) * pl.num_programs(1) + pl.program_id(1)) * pl.num_programs(2) + pl.program_id(2))
    _sample_att_body(step // groups_per_seq, sq_ref, skn_ref, svn_ref, sck_ref, scv_ref, sbias_ref,
                     sok_ref, sov_ref, so_ref)
    _att_kernel(q_ref, kc_ref, kp_ref, vc_ref, vp_ref, bias_ref, bias0_ref, o_ref, num_s, m_s, l_s, nat_s)


def _dilated_attention_both(q, k, v, bias, chunks_per_seq, sq, sk_new_t, sv_new_t, cache_k_t, cache_v_t, sbias, t_new):
    n_chunks = q.shape[0]
    n_br = len(DILATED_BRANCHES)
    nb, _, buf_len = cache_k_t.shape
    n_steps = n_chunks * N_PAIRS
    groups_per_seq = n_steps // nb
    assert nb * groups_per_seq == n_steps and N_PAIRS % groups_per_seq == 0, "sample steps must tile the prompt grid"
    width = N_PAIRS // groups_per_seq * PAIR_W

    cur = lambda b, c, p: (b * chunks_per_seq + c, 0, 0, p)
    prev = lambda b, c, p: (b * chunks_per_seq + jnp.maximum(c - 1, 0), 0, 0, p)
    blk = pl.BlockSpec((1, N_CLS, STEPS, PAIR_W), cur)
    blk_prev = pl.BlockSpec((1, N_CLS, STEPS, PAIR_W), prev)

    step = lambda b, c, p: (b * chunks_per_seq + c) * N_PAIRS + p
    s_seq = lambda b, c, p: step(b, c, p) // groups_per_seq
    s_grp = lambda b, c, p: step(b, c, p) % groups_per_seq
    tok = pl.BlockSpec((t_new, width), lambda b, c, p: (s_seq(b, c, p), s_grp(b, c, p)))
    new = pl.BlockSpec((width, NEW_LANES), lambda b, c, p: (s_grp(b, c, p), s_seq(b, c, p) // (NEW_LANES // t_new)))
    buf = pl.BlockSpec((1, width, buf_len), lambda b, c, p: (s_seq(b, c, p), s_grp(b, c, p), 0))
    return pl.pallas_call(
        functools.partial(_att_both_kernel, groups_per_seq=groups_per_seq),
        grid=(n_chunks // chunks_per_seq, chunks_per_seq, N_PAIRS),
        in_specs=[blk, blk, blk_prev, blk, blk_prev,
                  pl.BlockSpec((n_br, 1, 2 * STEPS, 2 * STEPS), lambda b, c, p: (0, p, 0, 0)),
                  pl.BlockSpec((1, n_br, 1, 2 * STEPS, 2 * STEPS),
                               lambda b, c, p: (jnp.where(c == 0, 1, 0), 0, p, 0, 0)),
                  tok, new, new, buf, buf,
                  pl.BlockSpec((n_br, width // PAIR_W, 2 * t_new, buf_len + NEW_LANES),
                               lambda b, c, p: (0, s_grp(b, c, p), 0, 0))],
        out_specs=[pl.BlockSpec((N_CLS * STEPS, PAIR_W), lambda b, c, p: (b * chunks_per_seq + c, p)), buf, buf, tok],
        out_shape=[jax.ShapeDtypeStruct((n_chunks * N_CLS * STEPS, ATT_WIDTH), F32),
                   jax.ShapeDtypeStruct(cache_k_t.shape, F32), jax.ShapeDtypeStruct(cache_v_t.shape, F32),
                   jax.ShapeDtypeStruct((nb * t_new, ATT_WIDTH), BF16)],
        scratch_shapes=[pltpu.VMEM((n_br, N_CLS, STEPS, PAIR_W), F32)] * 3 + [pltpu.VMEM((N_CLS * STEPS, PAIR_W), F32)],
        compiler_params=_params(3),
        name="dilated_attention",
    )(q, k, k, v, v, bias[0], bias, sq, sk_new_t, sv_new_t, cache_k_t, cache_v_t, sbias)


def _conv_tail(y, cb_ref, lg_ref, lb_ref):
    y = y + cb_ref[...]
    mu = jnp.mean(y, axis=-1, keepdims=True)
    yc = y - mu
    var = jnp.mean(yc * yc, axis=-1, keepdims=True)
    yn = yc * lax.rsqrt(var + EPS) * lg_ref[...] + lb_ref[...]
    return yn * _sigmoid(yn)


def _mix_out(x, att, c, wo_ref):
    aw = att.shape[1]
    return x + _dot(att.astype(BF16), wo_ref[:aw, :]) + _dot(c.astype(BF16), wo_ref[aw:, :])


def _cross_attend(qx, head_kv):
    hd = qx.shape[1] // X_HEADS
    outs = []
    for h in range(X_HEADS):
        mk, mv = head_kv(h)
        s = _dot_nt(qx[:, h * hd:(h + 1) * hd].astype(BF16), mk)
        p = jnp.exp2(s - jnp.max(s, axis=-1, keepdims=True))
        den = jnp.sum(p, axis=-1, keepdims=True)
        outs.append((_dot(p.astype(BF16), mv) / den).astype(BF16))
    return jnp.concatenate(outs, axis=1)


def _prompt_post_kernel(x_ref, att_ref, u_ref, uh_ref, cw_ref, cb_ref, lg_ref, lb_ref, wo_ref,
                        gx_ref, wq_ref, mk_ref, mv_ref, wxo_ref, o_ref, ubuf, *, tiles_per_seq):
    tm = x_ref.shape[0]
    seq_start = (pl.program_id(0) % tiles_per_seq) == 0
    ubuf[:CONV_HALO, :] = jnp.where(seq_start, 0.0, uh_ref[...])
    ubuf[CONV_HALO:, :] = u_ref[...]
    hist = ubuf[...]
    n_rows = hist.shape[0]
    first_tap = CONV_HALO - (CONV_K - 1)
    y = jnp.zeros(u_ref.shape, F32)
    for s in range(F32_SUBLANES):
        shifted = hist if s == 0 else pltpu.roll(hist, n_rows - s, axis=0)
        for j in range(CONV_K):
            if (first_tap + j) % F32_SUBLANES == s:
                lo = first_tap + j - s
                y = y + cw_ref[j:j + 1, :] * shifted[lo:lo + tm]
    c = _conv_tail(y, cb_ref, lg_ref, lb_ref)
    x1 = _mix_out(x_ref[...], att_ref[...], c, wo_ref)
    hd = wq_ref.shape[1] // X_HEADS
    qx = _dot(_rms(x1, gx_ref[...]).astype(BF16), wq_ref[...]) * (hd ** -0.5 * LOG2E)
    head_kv = lambda h: (mk_ref[:, h * hd:(h + 1) * hd], mv_ref[:, h * hd:(h + 1) * hd])
    o_ref[...] = x1 + _dot(_cross_attend(qx, head_kv), wxo_ref[...])


def _prompt_post(x, att, u, conv_w, conv_b, ln_g, ln_b, w_out, gx, w_xq, mk, mv, w_xo, seq, tm):
    n, d = x.shape
    cw = u.shape[1]
    n_mem = mk.shape[0] // (n // seq)
    tiles_per_seq = seq // tm
    row = lambda i: (i, 0)
    fixed = lambda i: (0, 0)
    halo = lambda i: (jnp.maximum(i * (tm // CONV_HALO) - 1, 0), 0)
    per_seq = lambda i: (i // tiles_per_seq, 0)
    full = _resident
    return pl.pallas_call(
        functools.partial(_prompt_post_kernel, tiles_per_seq=tiles_per_seq),
        grid=(n // tm,),
        in_specs=[pl.BlockSpec((tm, d), row), pl.BlockSpec((tm, ATT_WIDTH), row),
                  pl.BlockSpec((tm, cw), row), pl.BlockSpec((CONV_HALO, cw), halo),
                  full(conv_w), full(conv_b), full(ln_g), full(ln_b), full(w_out),
                  full(gx), full(w_xq),
                  pl.BlockSpec((n_mem, d), per_seq), pl.BlockSpec((n_mem, d), per_seq), full(w_xo)],
        out_specs=pl.BlockSpec((tm, d), row),
        out_shape=jax.ShapeDtypeStruct((n, d), F32),
        scratch_shapes=[pltpu.VMEM((CONV_HALO + tm, cw), F32)],
        compiler_params=_params(1),
        name="prompt_mix_cross",
    )(x, att, u, u, conv_w, conv_b, ln_g, ln_b, w_out, gx, w_xq, mk, mv, w_xo)


def _sample_mix_kernel(x_ref, att_ref, uf_ref, cw_ref, cb_ref, lg_ref, lb_ref, wo_ref, gx_ref, wq_ref,
                       x1_ref, qx_ref):
    nb, t_full, cw = uf_ref.shape
    t_new = t_full - (CONV_K - 1)
    y = jnp.zeros((nb, t_new, cw), F32)
    for j in range(CONV_K):
        y = y + cw_ref[j:j + 1, :] * uf_ref[:, j:j + t_new, :]
    c = _conv_tail(y.reshape(nb * t_new, cw), cb_ref, lg_ref, lb_ref)
    x1 = _mix_out(x_ref[...], att_ref[...], c, wo_ref)
    x1_ref[...] = x1
    hd = wq_ref.shape[1] // X_HEADS
    qx_ref[...] = _dot(_rms(x1, gx_ref[...]).astype(BF16), wq_ref[...]) * (hd ** -0.5 * LOG2E)


def _sample_mix(x, att, u_full, conv_w, conv_b, ln_g, ln_b, w_out, gx, w_xq):
    n, d = x.shape
    return pl.pallas_call(
        _sample_mix_kernel,
        out_shape=[jax.ShapeDtypeStruct((n, d), F32), jax.ShapeDtypeStruct((n, w_xq.shape[1]), F32)],
        compiler_params=pltpu.CompilerParams(vmem_limit_bytes=VMEM_LIMIT),
        name="sample_mix",
    )(x, att, u_full, conv_w, conv_b, ln_g, ln_b, w_out, gx, w_xq)


def _sample_cross_kernel(qx_ref, mk_ref, mv_ref, o_ref, *, n_mem):
    n_seq = mk_ref.shape[0]
    t_new = qx_ref.shape[0] // n_seq
    groups = mk_ref.shape[1] // (n_mem * X_HEADS)
    pitch = groups * X_HEADS

    def head_rows(ref, i, h):
        return jnp.concatenate([ref[i, pl.ds(j * X_HEADS + h, n_mem, stride=pitch), :] for j in range(groups)],
                               axis=1).astype(BF16)

    for i in range(n_seq):
        tok = slice(i * t_new, (i + 1) * t_new)
        o_ref[tok, :] = _cross_attend(qx_ref[tok, :],
                                      lambda h, i=i: (head_rows(mk_ref, i, h), head_rows(mv_ref, i, h)))


SAMPLE_CROSS_SEQS_PER_STEP = 4


def _sample_cross(qx, mem_k, mem_v, t_new):
    nb, n_mem, n_heads, hd = mem_k.shape
    sps = SAMPLE_CROSS_SEQS_PER_STEP
    rows = lambda a: jnp.transpose(a.reshape(nb, n_mem, n_heads, hd // PAIR_W, PAIR_W),
                                   (0, 1, 3, 2, 4)).reshape(nb, -1, PAIR_W)
    tok = pl.BlockSpec((sps * t_new, n_heads * hd), lambda n: (n, 0))
    mem = pl.BlockSpec((sps, n_mem * n_heads * hd // PAIR_W, PAIR_W), lambda n: (n, 0, 0))
    return pl.pallas_call(
        functools.partial(_sample_cross_kernel, n_mem=n_mem),
        grid=(nb // sps,),
        in_specs=[tok, mem, mem],
        out_specs=tok,
        out_shape=jax.ShapeDtypeStruct(qx.shape, BF16),
        compiler_params=_params(1),
        name="sample_cross",
    )(qx, rows(mem_k), rows(mem_v))


def _swiglu_final(x2, gf_ref, wg_ref, wu_ref, wd_ref, gfin_ref, ff_chunk):
    h = _rms(x2, gf_ref[...]).astype(BF16)
    acc = x2
    for lo in range(0, wg_ref.shape[1], ff_chunk):
        gate = _dot(h, wg_ref[:, lo:lo + ff_chunk])
        up = _dot(h, wu_ref[:, lo:lo + ff_chunk])
        acc = acc + _dot((gate * _sigmoid(gate) * up).astype(BF16), wd_ref[lo:lo + ff_chunk, :])
    return _rms(acc, gfin_ref[...])


def _ffn_kernel(x_ref, gf_ref, wg_ref, wu_ref, wd_ref, gfin_ref, o_ref, *, ff_chunk):
    o_ref[...] = _swiglu_final(x_ref[...], gf_ref, wg_ref, wu_ref, wd_ref, gfin_ref, ff_chunk)


def _sample_tail_kernel(x_ref, xo_ref, wxo_ref, gf_ref, wg_ref, wu_ref, wd_ref, gfin_ref, o_ref, *, ff_chunk):
    x2 = x_ref[...] + _dot(xo_ref[...], wxo_ref[...])
    o_ref[...] = _swiglu_final(x2, gf_ref, wg_ref, wu_ref, wd_ref, gfin_ref, ff_chunk)


FF_CHUNK = 256


def _resident(a):
    return pl.BlockSpec(a.shape, lambda *_: (0,) * a.ndim, pipeline_mode=pl.Buffered(1))


def _ffn(x, gf, wg, wu, wd, gfin, tm):
    n, d = x.shape
    row = lambda i: (i, 0)
    full = _resident
    return pl.pallas_call(
        functools.partial(_ffn_kernel, ff_chunk=FF_CHUNK),
        grid=(n // tm,),
        in_specs=[pl.BlockSpec((tm, d), row), full(gf), full(wg), full(wu), full(wd), full(gfin)],
        out_specs=pl.BlockSpec((tm, d), row),
        out_shape=jax.ShapeDtypeStruct((n, d), F32),
        compiler_params=_params(1),
        name="swiglu_final_norm",
    )(x, gf, wg, wu, wd, gfin)


def _sample_tail(x1, xo, w_xo, gf, wg, wu, wd, gfin):
    return pl.pallas_call(
        functools.partial(_sample_tail_kernel, ff_chunk=FF_CHUNK),
        out_shape=jax.ShapeDtypeStruct(x1.shape, F32),
        compiler_params=pltpu.CompilerParams(vmem_limit_bytes=VMEM_LIMIT),
        name="sample_cross_out_swiglu",
    )(x1, xo, w_xo, gf, wg, wu, wd, gfin)


def _mem_kv_kernel(m_ref, g_ref, wk_ref, wv_ref, k_ref, v_ref, kb_ref, vb_ref):
    n_mem = m_ref.shape[0]
    hd = wk_ref.shape[1] // X_HEADS
    groups = hd // PAIR_W
    h = _rms(m_ref[...], g_ref[...]).astype(BF16)
    for w_ref, o_ref, ob_ref in ((wk_ref, k_ref, kb_ref), (wv_ref, v_ref, vb_ref)):
        val = _dot(h, w_ref[...])
        ob_ref[...] = val.astype(BF16)
        for head in range(X_HEADS):
            for j in range(groups):
                lanes = slice(head * hd + j * PAIR_W, head * hd + (j + 1) * PAIR_W)
                o_ref[0, pl.ds(j * X_HEADS + head, n_mem, stride=groups * X_HEADS), :] = val[:, lanes]


def _mem_kv(mem, g, w_k, w_v, n_mem):
    n, d = mem.shape
    row = lambda i: (i, 0)
    full = lambda a: pl.BlockSpec(a.shape, lambda i: (0, 0))
    dk = w_k.shape[1]
    rows_spec = pl.BlockSpec((1, n_mem * dk // PAIR_W, PAIR_W), lambda i: (i, 0, 0))
    rows_shape = jax.ShapeDtypeStruct((n // n_mem, n_mem * dk // PAIR_W, PAIR_W), F32)
    return pl.pallas_call(
        _mem_kv_kernel,
        grid=(n // n_mem,),
        in_specs=[pl.BlockSpec((n_mem, d), row), full(g), full(w_k), full(w_v)],
        out_specs=[rows_spec, rows_spec, pl.BlockSpec((n_mem, dk), row), pl.BlockSpec((n_mem, dk), row)],
        out_shape=[rows_shape, rows_shape] + [jax.ShapeDtypeStruct((n, dk), BF16)] * 2,
        compiler_params=_params(1),
        name="mem_kv",
    )(mem, g, w_k, w_v)


ROW_TILE = 512
FFN_ROW_TILE = 1024
ATT_CHUNK = STEPS * DILATED_BRANCHES[-1][1]


def kernel(x_prompt, x_sample, mem_prompt, cache_win_k, cache_win_v, cache_conv, cache_mem_k, cache_mem_v,
           rpb_table, norm_mix_g, w_in, conv_w, conv_b, conv_ln_g, conv_ln_b, w_out, norm_x_g, norm_mem_g,
           w_xq, w_xk, w_xv, w_xo, norm_ffn_g, w_ffn_gate, w_ffn_up, w_ffn_down, norm_final_g):
    depth = w_in.shape[0]
    assert depth == 1, "single-layer stack"
    batch, seq, d = x_prompt.shape
    nb, t_new, _ = x_sample.shape
    buf_len = cache_win_k.shape[2]
    keep_p = min(MAX_DISTANCE, seq)
    n_mem = mem_prompt.shape[1]
    conv_hist = CONV_K - 1
    assert seq % ATT_CHUNK == 0 and keep_p % ATT_CHUNK == 0 and buf_len == MAX_DISTANCE

    row = lambda a: a.reshape(1, -1)
    bf = lambda a: a.astype(BF16)
    l = 0
    w_in_b = bf(w_in[l])
    g_mix, g_x, g_mem, g_ffn, g_fin = (row(norm_mix_g[l]), row(norm_x_g[l]), row(norm_mem_g[l]),
                                       row(norm_ffn_g[l]), row(norm_final_g))
    cv_w, cv_b, ln_g, ln_b = conv_w[l], row(conv_b[l]), row(conv_ln_g[l]), row(conv_ln_b[l])

    bias_p = _bias_tables(rpb_table, _prompt_bucket_index(), masked_cols=STEPS)
    bias_s = _bias_tables(rpb_table, _sample_bucket_index(buf_len, t_new))[0]

    xp = x_prompt.reshape(batch * seq, d)
    xs = x_sample.reshape(nb * t_new, d)
    (q, k, v, u, p_wk_t, p_wv_t,
     w_out_b, w_xq_b, w_xk_b, w_xv_b, w_xo_b, w_g_b, w_u_b, w_d_b) = _in_proj_prompt(
        xp, g_mix, w_in_b, [w_out[l], w_xq[l], w_xk[l], w_xv[l], w_xo[l], w_ffn_gate[l], w_ffn_up[l], w_ffn_down[l]],
        seq, keep_p, ROW_TILE)
    qs, us, ks_t, vs_t = _in_proj_sample(xs, g_mix, w_in_b)
    chan_major = lambda a: jnp.transpose(a, (0, 2, 3, 1)).reshape(nb, ATT_WIDTH, -1)
    att, s_wk_t, s_wv_t, att_s = _dilated_attention_both(
        q, k, v, bias_p, seq // ATT_CHUNK,
        qs, ks_t, vs_t, chan_major(cache_win_k[l]), chan_major(cache_win_v[l]), bias_s, t_new)

    mk, mv, mk_b, mv_b = _mem_kv(mem_prompt.reshape(batch * n_mem, d), g_mem, w_xk_b, w_xv_b, n_mem)
    x2 = _prompt_post(xp, att, u, cv_w, cv_b, ln_g, ln_b, w_out_b, g_x, w_xq_b, mk_b, mv_b, w_xo_b,
                      seq, FFN_ROW_TILE)
    y_prompt = _ffn(x2, g_ffn, w_g_b, w_u_b, w_d_b, g_fin, FFN_ROW_TILE).reshape(batch, seq, d)

    tok_major = lambda a: jnp.transpose(a.reshape(a.shape[0], N_ATT_HEADS, HEAD_DIM, -1), (0, 3, 1, 2))[None]
    p_wk, p_wv = tok_major(p_wk_t), tok_major(p_wv_t)
    u3 = u.reshape(batch, seq, -1)
    p_conv = u3[:, seq - conv_hist:][None]
    xh = lambda a: jnp.transpose(a.reshape(batch, n_mem, -1, X_HEADS, PAIR_W), (0, 1, 3, 2, 4)).reshape(
        1, batch, n_mem, X_HEADS, -1)
    p_mk, p_mv = xh(mk), xh(mv)

    u_full = jnp.concatenate([cache_conv[l], us.reshape(nb, t_new, -1)], axis=1)
    x1s, qxs = _sample_mix(xs, att_s, u_full, cv_w, cv_b, ln_g, ln_b, w_out_b, g_x, w_xq_b)
    xo_s = _sample_cross(qxs, cache_mem_k[l], cache_mem_v[l], t_new)
    y_sample = _sample_tail(x1s, xo_s, w_xo_b, g_ffn, w_g_b, w_u_b, w_d_b, g_fin).reshape(nb, t_new, d)

    return (y_prompt, y_sample, p_wk, p_wv, p_conv, p_mk, p_mv,
            tok_major(s_wk_t), tok_major(s_wv_t), u_full[:, t_new:][None])
```

```python
import functools
import math

import numpy as np
import jax
import jax.numpy as jnp
from jax import lax
from jax.experimental import pallas as pl
from jax.experimental.pallas import tpu as pltpu

F32 = jnp.float32
BF16 = jnp.bfloat16

HEAD_DIM = 64
N_ATT_HEADS = 12
ATT_WIDTH = N_ATT_HEADS * HEAD_DIM
PAIR_W = 2 * HEAD_DIM
N_PAIRS = N_ATT_HEADS // 2
CONV_K = 31
CONV_HALO = 32
F32_SUBLANES = 8
DILATED_BRANCHES = ((128, 1), (512, 4), (2048, 16))
STEPS = 128
N_CLS = DILATED_BRANCHES[-1][1]
GATHER_STRIDE = 4
NEW_LANES = 128
N_BUCKETS = 32
MAX_EXACT = N_BUCKETS // 2
MAX_DISTANCE = 2048
X_HEADS = 4
EPS = 1e-6
LOG2E = math.log2(math.e)
ATT_SCALE = HEAD_DIM ** -0.5 * LOG2E
MASKED = -1e30

V7X_VMEM_BYTES = 64 * 1024 * 1024
VMEM_LIMIT = V7X_VMEM_BYTES * 3 // 4


def _params(n_grid_dims):
    return pltpu.CompilerParams(dimension_semantics=("arbitrary",) * n_grid_dims,
                                vmem_limit_bytes=VMEM_LIMIT)


def _rms(x, g):
    return x * lax.rsqrt(jnp.mean(x * x, axis=-1, keepdims=True) + EPS) * g


def _sigmoid(x):
    return 1.0 / (1.0 + jnp.exp(-x))


def _dot(a, b):
    return jnp.dot(a, b, preferred_element_type=F32)


def _dot_nt(a, b):
    return lax.dot_general(a, b, (((1,), (1,)), ((), ())), preferred_element_type=F32)


def _t5_bucket_np(dist):
    n = dist.astype(np.int32)
    nf = np.maximum(n, MAX_EXACT).astype(np.float32)
    large = MAX_EXACT + (np.log(nf / np.float32(MAX_EXACT)) / np.float32(math.log(MAX_DISTANCE / MAX_EXACT))
                         * np.float32(N_BUCKETS - MAX_EXACT)).astype(np.int32)
    large = np.minimum(large, N_BUCKETS - 1)
    return np.where(n < MAX_EXACT, n, large)


def _block_order(dil):
    groups = N_CLS // dil
    per = STEPS // groups
    i = np.arange(STEPS)
    return (i % per) * groups + i // per


def _prompt_bucket_index():
    out = []
    for _, dil in DILATED_BRANCHES:
        n = _block_order(dil)
        kj = np.concatenate([n, STEPS + n])[None, :]
        sub = STEPS + n[:, None] - kj
        band = (sub >= 0) & (sub <= STEPS)
        out.append(np.where(band, _t5_bucket_np(dil * np.maximum(sub, 0)), -1))
    return np.stack(out).astype(np.int32)


def _sample_bucket_index(buf_len, t_new):
    col = np.arange(buf_len + NEW_LANES)
    key = np.where(col < buf_len, (col + t_new) % buf_len, col - NEW_LANES + t_new)
    keep = (col < buf_len) | (col >= buf_len + NEW_LANES - t_new)
    i = np.arange(t_new)[:, None]
    dist = buf_len + i - key[None, :]
    out = []
    for window, dil in DILATED_BRANCHES:
        ok = keep[None, :] & (dist >= 0) & (dist % dil == 0) & (dist // dil <= window // dil)
        out.append(np.where(ok, _t5_bucket_np(np.maximum(dist, 0)), -1))
    return np.stack(out).astype(np.int32)


def _bias_kernel(tab_ref, idx_ref, o_ref, *, masked_cols, buckets):
    pair = pl.program_id(0)
    for g, present in enumerate(buckets):
        idx = idx_ref[g]
        rows = idx.shape[0]
        for half in range(2):
            head = 2 * pair + half
            acc = jnp.full(idx.shape, MASKED, F32)
            for b in present:
                acc = jnp.where(idx == b, tab_ref[b, head] * LOG2E, acc)
            o_ref[0, g, 0, half * rows:(half + 1) * rows, :] = acc
            if masked_cols:
                col = lax.broadcasted_iota(jnp.int32, idx.shape, 1)
                o_ref[1, g, 0, half * rows:(half + 1) * rows, :] = jnp.where(col < masked_cols, MASKED, acc)


def _bias_tables(table, idx, masked_cols=0):
    n_var, rows, cols = idx.shape
    n_out = 2 if masked_cols else 1
    buckets = tuple(tuple(int(b) for b in np.unique(idx[g]) if b >= 0) for g in range(n_var))
    return pl.pallas_call(
        functools.partial(_bias_kernel, masked_cols=masked_cols, buckets=buckets),
        grid=(N_PAIRS,),
        in_specs=[pl.BlockSpec(memory_space=pltpu.SMEM),
                  pl.BlockSpec((n_var, rows, cols), lambda p: (0, 0, 0))],
        out_specs=pl.BlockSpec((n_out, n_var, 1, 2 * rows, cols), lambda p: (0, 0, p, 0, 0)),
        out_shape=jax.ShapeDtypeStruct((n_out, n_var, N_PAIRS, 2 * rows, cols), F32),
        compiler_params=_params(1),
        name="bias_tables",
    )(table, jnp.asarray(idx))


def _in_proj_sample_kernel(x_ref, g_ref, w_ref, q_ref, u_ref, kt_ref, vt_ref):
    h = _rms(x_ref[...], g_ref[...]).astype(BF16)
    aw = q_ref.shape[1]
    cw = u_ref.shape[1]
    q_ref[...] = _dot(h, w_ref[:, 0:aw]) * ATT_SCALE
    a = _dot(h, w_ref[:, 3 * aw:3 * aw + cw])
    gate = _dot(h, w_ref[:, 3 * aw + cw:3 * aw + 2 * cw])
    u_ref[...] = a * _sigmoid(gate)
    kt_ref[...] = _dot(h, w_ref[:, aw:2 * aw]).T
    vt_ref[...] = _dot(h, w_ref[:, 2 * aw:3 * aw]).T


def _in_proj_sample(x, g, w_bf16):
    n, d = x.shape
    aw = ATT_WIDTH
    cw = (w_bf16.shape[1] - 3 * aw) // 2
    new_shape = jax.ShapeDtypeStruct((aw, n), F32)
    return pl.pallas_call(
        _in_proj_sample_kernel,
        out_shape=[jax.ShapeDtypeStruct((n, aw), F32), jax.ShapeDtypeStruct((n, cw), F32), new_shape, new_shape],
        compiler_params=pltpu.CompilerParams(vmem_limit_bytes=VMEM_LIMIT),
        name="in_proj_sample",
    )(x, g, w_bf16)


def _in_proj_prompt_kernel(*refs, tiles_per_seq, tail_first, side_blocks):
    n_side = len(side_blocks)
    x_ref, g_ref, w_ref = refs[:3]
    side_in = refs[3:3 + n_side]
    q_ref, k_ref, v_ref, u_ref, kt_ref, vt_ref = refs[3 + n_side:9 + n_side]
    side_out = refs[9 + n_side:9 + 2 * n_side]
    xs, xs2 = refs[9 + 2 * n_side:]
    for src, dst, n_blk in zip(side_in, side_out, side_blocks):
        @pl.when(pl.program_id(0) < n_blk)
        def _(src=src, dst=dst):
            dst[...] = src[...].astype(dst.dtype)

    tm, d = x_ref.shape
    per = tm // N_CLS
    aw = q_ref.shape[3]
    cw = u_ref.shape[1]
    xn = _rms(x_ref[...], g_ref[...])
    hn = xn.astype(BF16)
    gate = _dot(hn, w_ref[:, 3 * aw + cw:3 * aw + 2 * cw])
    u_ref[...] = _dot(hn, w_ref[:, 3 * aw:3 * aw + cw]) * _sigmoid(gate)

    quarter = tm // GATHER_STRIDE
    for c in range(d // PAIR_W):
        xs[c] = xn[:, c * PAIR_W:(c + 1) * PAIR_W]
    for c in range(d // PAIR_W):
        for res in range(GATHER_STRIDE):
            xs2[c, res * quarter:(res + 1) * quarter, :] = xs[c, pl.ds(res, quarter, stride=GATHER_STRIDE), :]

    def class_rows(c, r):
        start = (r % GATHER_STRIDE) * quarter + r // GATHER_STRIDE
        return xs2[c, pl.ds(start, per, stride=GATHER_STRIDE), :]

    h = jnp.concatenate(
        [jnp.concatenate([class_rows(c, r) for r in range(N_CLS)], axis=0) for c in range(d // PAIR_W)],
        axis=1).astype(BF16)
    q_ref[0] = (_dot(h, w_ref[:, 0:aw]) * ATT_SCALE).reshape(N_CLS, per, aw)
    k_ref[0] = _dot(h, w_ref[:, aw:2 * aw]).reshape(N_CLS, per, aw)
    v_ref[0] = _dot(h, w_ref[:, 2 * aw:3 * aw]).reshape(N_CLS, per, aw)

    @pl.when(pl.program_id(0) % tiles_per_seq >= tail_first)
    def _():
        kt_ref[0] = _dot(hn, w_ref[:, aw:2 * aw]).T
        vt_ref[0] = _dot(hn, w_ref[:, 2 * aw:3 * aw]).T


BF16_SUBLANES = 16


def _side_block_rows(rows, n_steps):
    return next(br for br in range(BF16_SUBLANES, rows + 1, BF16_SUBLANES)
                if rows % br == 0 and rows // br <= n_steps)


def _in_proj_prompt(x, g, w_bf16, side_weights, seq, keep, tm):
    n, d = x.shape
    n_steps = n // tm
    side_rows = [_side_block_rows(w.shape[0], n_steps) for w in side_weights]
    side_blocks = tuple(w.shape[0] // br for w, br in zip(side_weights, side_rows))
    side_specs = [pl.BlockSpec((br, w.shape[1]), lambda i, last=nb - 1: (jnp.minimum(i, last), 0))
                  for w, br, nb in zip(side_weights, side_rows, side_blocks)]
    aw = ATT_WIDTH
    cw = (w_bf16.shape[1] - 3 * aw) // 2
    chunk = N_CLS * STEPS
    tiles_per_chunk = chunk // tm
    tiles_per_seq = seq // tm
    tail_first = (seq - keep) // tm
    row = lambda i: (i, 0)
    fixed = lambda i: (0, 0)
    cls = pl.BlockSpec((1, N_CLS, tm // N_CLS, aw), lambda i: (i // tiles_per_chunk, 0, i % tiles_per_chunk, 0))
    tail = pl.BlockSpec((1, aw, tm),
                        lambda i: (i // tiles_per_seq, 0, jnp.maximum(i % tiles_per_seq - tail_first, 0)))
    cls_shape = jax.ShapeDtypeStruct((n // chunk, N_CLS, STEPS, aw), F32)
    tail_shape = jax.ShapeDtypeStruct((n // seq, aw, keep), F32)
    return pl.pallas_call(
        functools.partial(_in_proj_prompt_kernel, tiles_per_seq=tiles_per_seq, tail_first=tail_first,
                          side_blocks=side_blocks),
        grid=(n_steps,),
        in_specs=[pl.BlockSpec((tm, d), row), pl.BlockSpec((1, d), fixed),
                  pl.BlockSpec(w_bf16.shape, fixed)] + side_specs,
        out_specs=[cls, cls, cls, pl.BlockSpec((tm, cw), row), tail, tail] + side_specs,
        out_shape=[cls_shape, cls_shape, cls_shape, jax.ShapeDtypeStruct((n, cw), F32), tail_shape, tail_shape]
                  + [jax.ShapeDtypeStruct(w.shape, BF16) for w in side_weights],
        scratch_shapes=[pltpu.VMEM((d // PAIR_W, tm, PAIR_W), F32)] * 2,
        compiler_params=_params(1),
        name="in_proj_prompt",
    )(x, g, w_bf16, *side_weights)


def _stack_heads(x, first):
    zero = jnp.zeros_like(x)
    return jnp.concatenate([jnp.where(first, x, zero), jnp.where(first, zero, x)], axis=0)


def _att_kernel(q_ref, kc_ref, kp_ref, vc_ref, vp_ref, bias_ref, bias0_ref, o_ref, num_s, m_s, l_s, nat_s):
    first = lax.broadcasted_iota(jnp.int32, (STEPS, PAIR_W), 1) < HEAD_DIM
    ones = jnp.ones((2 * STEPS, PAIR_W), BF16)

    def pieces(dil, cls, blk):
        groups = N_CLS // dil
        per = STEPS // groups
        return [(cls + dil * j, pl.ds(blk * per, per)) for j in range(groups)], per

    def load(ref, dil, cls, blk):
        idx, _ = pieces(dil, cls, blk)
        return jnp.concatenate([ref[0, r, rows, :] for r, rows in idx], axis=0)

    def store(ref, g, dil, cls, blk, val):
        idx, per = pieces(dil, cls, blk)
        for j, (r, rows) in enumerate(idx):
            ref[g, r, rows, :] = val[j * per:(j + 1) * per]

    def attend(g, dil, cls, blk, k_prev, v_prev, bias):
        q_st = _stack_heads(load(q_ref, dil, cls, blk), first).astype(BF16)
        k_own = load(kc_ref, dil, cls, blk).astype(BF16)
        v_own = load(vc_ref, dil, cls, blk).astype(BF16)
        kcat = jnp.concatenate([k_prev, k_own], axis=0)
        vcat = jnp.concatenate([v_prev, v_own], axis=0)
        s = _dot_nt(q_st, kcat) + bias
        m = jnp.max(s, axis=-1, keepdims=True)
        p = jnp.exp2(s - m).astype(BF16)
        o = _dot(p, jnp.concatenate([vcat, ones], axis=1))
        store(num_s, g, dil, cls, blk, jnp.where(first, o[:STEPS, :PAIR_W], o[STEPS:, :PAIR_W]))
        store(l_s, g, dil, cls, blk, jnp.where(first, o[:STEPS, PAIR_W:], o[STEPS:, PAIR_W:]))
        store(m_s, g, dil, cls, blk, jnp.where(first, jnp.broadcast_to(m[:STEPS], (STEPS, PAIR_W)),
                                               jnp.broadcast_to(m[STEPS:], (STEPS, PAIR_W))))
        return k_own, v_own

    carried = {}
    for i in range(N_CLS):
        for g, (_, dil) in reversed(list(enumerate(DILATED_BRANCHES))):
            n_blk = N_CLS // dil
            cls, blk = i % dil, i // dil
            if blk == 0:
                prev = (load(kp_ref, dil, cls, n_blk - 1).astype(BF16), load(vp_ref, dil, cls, n_blk - 1).astype(BF16))
                bias = bias0_ref[0, g, 0]
            else:
                prev, bias = carried[g, cls], bias_ref[g, 0]
            carried[g, cls] = attend(g, dil, cls, blk, *prev, bias)

    quarter = N_CLS * STEPS // GATHER_STRIDE
    for r in range(N_CLS):
        m_all = jnp.maximum(jnp.maximum(m_s[0, r], m_s[1, r]), m_s[2, r])
        num = jnp.zeros(m_all.shape, F32)
        den = jnp.zeros(m_all.shape, F32)
        for g in range(len(DILATED_BRANCHES)):
            w = jnp.exp2(m_s[g, r] - m_all)
            num = num + w * num_s[g, r]
            den = den + w * l_s[g, r]
        start = (r % GATHER_STRIDE) * quarter + r // GATHER_STRIDE
        nat_s[pl.ds(start, STEPS, stride=GATHER_STRIDE), :] = num / den
    for res in range(GATHER_STRIDE):
        o_ref[pl.ds(res, quarter, stride=GATHER_STRIDE), :] = nat_s[res * quarter:(res + 1) * quarter, :]


def _sample_att_body(seq, q_ref, kn_ref, vn_ref, ck_ref, cv_ref, bias_ref, ok_ref, ov_ref, o_ref):
    buf_len = ck_ref.shape[2]
    t_new = q_ref.shape[0]
    tail = buf_len - NEW_LANES
    is_new = lax.broadcasted_iota(jnp.int32, (PAIR_W, NEW_LANES), 1) >= NEW_LANES - t_new
    first = lax.broadcasted_iota(jnp.int32, (t_new, PAIR_W), 1) < HEAD_DIM
    col0 = (seq % (NEW_LANES // t_new)) * t_new
    to_end = (NEW_LANES - t_new - col0) % NEW_LANES

    for j in range(ck_ref.shape[1] // PAIR_W):
        ch = slice(j * PAIR_W, (j + 1) * PAIR_W)

        def shift_in(c_ref, n_ref, o_ref):
            rot = pltpu.roll(c_ref[0, ch, :], buf_len - t_new, axis=1)
            new = jnp.where(is_new, pltpu.roll(n_ref[ch, :], to_end, axis=1), 0.0)
            o_ref[0, ch, :tail] = rot[:, :tail]
            o_ref[0, ch, tail:] = jnp.where(is_new, new, rot[:, tail:])
            return rot.astype(BF16), new.astype(BF16)

        rot_k, new_k = shift_in(ck_ref, kn_ref, ok_ref)
        rot_v, new_v = shift_in(cv_ref, vn_ref, ov_ref)

        q_st = _stack_heads(q_ref[:, ch], first).astype(BF16)
        s = jnp.concatenate([_dot(q_st, rot_k), _dot(q_st, new_k)], axis=1)
        sg = [s + bias_ref[g, j] for g in range(len(DILATED_BRANCHES))]
        m = functools.reduce(jnp.maximum, [jnp.max(x, axis=-1, keepdims=True) for x in sg])
        p = functools.reduce(jnp.add, [jnp.exp2(x - m) for x in sg])
        den = jnp.sum(p, axis=-1, keepdims=True)
        pb = p.astype(BF16)
        o = (_dot_nt(pb[:, :buf_len], rot_v) + _dot_nt(pb[:, buf_len:], new_v)) / den
        o_ref[:, ch] = jnp.where(first, o[:t_new], o[t_new:]).astype(o_ref.dtype)


def _att_both_kernel(q_ref, kc_ref, kp_ref, vc_ref, vp_ref, bias_ref, bias0_ref,
                     sq_ref, skn_ref, svn_ref, sck_ref, scv_ref, sbias_ref,
                     o_ref, sok_ref, sov_ref, so_ref, num_s, m_s, l_s, nat_s, *, groups_per_seq):
    step = ((pl.program_id(0) * pl.num_programs(1) + pl.program_id(1)) * pl.num_programs(2) + pl.program_id(2))
    _sample_att_body(step // groups_per_seq, sq_ref, skn_ref, svn_ref, sck_ref, scv_ref, sbias_ref,
                     sok_ref, sov_ref, so_ref)
    _att_kernel(q_ref, kc_ref, kp_ref, vc_ref, vp_ref, bias_ref, bias0_ref, o_ref, num_s, m_s, l_s, nat_s)


def _dilated_attention_both(q, k, v, bias, chunks_per_seq, sq, sk_new_t, sv_new_t, cache_k_t, cache_v_t, sbias, t_new):
    n_chunks = q.shape[0]
    n_br = len(DILATED_BRANCHES)
    nb, _, buf_len = cache_k_t.shape
    n_steps = n_chunks * N_PAIRS
    groups_per_seq = n_steps // nb
    assert nb * groups_per_seq == n_steps and N_PAIRS % groups_per_seq == 0, "sample steps must tile the prompt grid"
    width = N_PAIRS // groups_per_seq * PAIR_W

    cur = lambda b, c, p: (b * chunks_per_seq + c, 0, 0, p)
    prev = lambda b, c, p: (b * chunks_per_seq + jnp.maximum(c - 1, 0), 0, 0, p)
    blk = pl.BlockSpec((1, N_CLS, STEPS, PAIR_W), cur)
    blk_prev = pl.BlockSpec((1, N_CLS, STEPS, PAIR_W), prev)

    step = lambda b, c, p: (b * chunks_per_seq + c) * N_PAIRS + p
    s_seq = lambda b, c, p: step(b, c, p) // groups_per_seq
    s_grp = lambda b, c, p: step(b, c, p) % groups_per_seq
    tok = pl.BlockSpec((t_new, width), lambda b, c, p: (s_seq(b, c, p), s_grp(b, c, p)))
    new = pl.BlockSpec((width, NEW_LANES), lambda b, c, p: (s_grp(b, c, p), s_seq(b, c, p) // (NEW_LANES // t_new)))
    buf = pl.BlockSpec((1, width, buf_len), lambda b, c, p: (s_seq(b, c, p), s_grp(b, c, p), 0))
    return pl.pallas_call(
        functools.partial(_att_both_kernel, groups_per_seq=groups_per_seq),
        grid=(n_chunks // chunks_per_seq, chunks_per_seq, N_PAIRS),
        in_specs=[blk, blk, blk_prev, blk, blk_prev,
                  pl.BlockSpec((n_br, 1, 2 * STEPS, 2 * STEPS), lambda b, c, p: (0, p, 0, 0)),
                  pl.BlockSpec((1, n_br, 1, 2 * STEPS, 2 * STEPS),
                               lambda b, c, p: (jnp.where(c == 0, 1, 0), 0, p, 0, 0)),
                  tok, new, new, buf, buf,
                  pl.BlockSpec((n_br, width // PAIR_W, 2 * t_new, buf_len + NEW_LANES),
                               lambda b, c, p: (0, s_grp(b, c, p), 0, 0))],
        out_specs=[pl.BlockSpec((N_CLS * STEPS, PAIR_W), lambda b, c, p: (b * chunks_per_seq + c, p)), buf, buf, tok],
        out_shape=[jax.ShapeDtypeStruct((n_chunks * N_CLS * STEPS, ATT_WIDTH), F32),
                   jax.ShapeDtypeStruct(cache_k_t.shape, F32), jax.ShapeDtypeStruct(cache_v_t.shape, F32),
                   jax.ShapeDtypeStruct((nb * t_new, ATT_WIDTH), BF16)],
        scratch_shapes=[pltpu.VMEM((n_br, N_CLS, STEPS, PAIR_W), F32)] * 3 + [pltpu.VMEM((N_CLS * STEPS, PAIR_W), F32)],
        compiler_params=_params(3),
        name="dilated_attention",
    )(q, k, k, v, v, bias[0], bias, sq, sk_new_t, sv_new_t, cache_k_t, cache_v_t, sbias)


def _conv_tail(y, cb_ref, lg_ref, lb_ref):
    y = y + cb_ref[...]
    mu = jnp.mean(y, axis=-1, keepdims=True)
    yc = y - mu
    var = jnp.mean(yc * yc, axis=-1, keepdims=True)
    yn = yc * lax.rsqrt(var + EPS) * lg_ref[...] + lb_ref[...]
    return yn * _sigmoid(yn)


def _mix_out(x, att, c, wo_ref):
    aw = att.shape[1]
    return x + _dot(att.astype(BF16), wo_ref[:aw, :]) + _dot(c.astype(BF16), wo_ref[aw:, :])


def _cross_attend(qx, head_kv):
    hd = qx.shape[1] // X_HEADS
    outs = []
    for h in range(X_HEADS):
        mk, mv = head_kv(h)
        s = _dot_nt(qx[:, h * hd:(h + 1) * hd].astype(BF16), mk)
        p = jnp.exp2(s - jnp.max(s, axis=-1, keepdims=True))
        den = jnp.sum(p, axis=-1, keepdims=True)
        outs.append((_dot(p.astype(BF16), mv) / den).astype(BF16))
    return jnp.concatenate(outs, axis=1)


def _prompt_post_kernel(x_ref, att_ref, u_ref, uh_ref, cw_ref, cb_ref, lg_ref, lb_ref, wo_ref,
                        gx_ref, wq_ref, mk_ref, mv_ref, wxo_ref, o_ref, ubuf, *, tiles_per_seq):
    tm = x_ref.shape[0]
    seq_start = (pl.program_id(0) % tiles_per_seq) == 0
    ubuf[:CONV_HALO, :] = jnp.where(seq_start, 0.0, uh_ref[...])
    ubuf[CONV_HALO:, :] = u_ref[...]
    hist = ubuf[...]
    n_rows = hist.shape[0]
    first_tap = CONV_HALO - (CONV_K - 1)
    y = jnp.zeros(u_ref.shape, F32)
    for s in range(F32_SUBLANES):
        shifted = hist if s == 0 else pltpu.roll(hist, n_rows - s, axis=0)
        for j in range(CONV_K):
            if (first_tap + j) % F32_SUBLANES == s:
                lo = first_tap + j - s
                y = y + cw_ref[j:j + 1, :] * shifted[lo:lo + tm]
    c = _conv_tail(y, cb_ref, lg_ref, lb_ref)
    x1 = _mix_out(x_ref[...], att_ref[...], c, wo_ref)
    hd = wq_ref.shape[1] // X_HEADS
    qx = _dot(_rms(x1, gx_ref[...]).astype(BF16), wq_ref[...]) * (hd ** -0.5 * LOG2E)
    head_kv = lambda h: (mk_ref[:, h * hd:(h + 1) * hd], mv_ref[:, h * hd:(h + 1) * hd])
    o_ref[...] = x1 + _dot(_cross_attend(qx, head_kv), wxo_ref[...])


def _prompt_post(x, att, u, conv_w, conv_b, ln_g, ln_b, w_out, gx, w_xq, mk, mv, w_xo, seq, tm):
    n, d = x.shape
    cw = u.shape[1]
    n_mem = mk.shape[0] // (n // seq)
    tiles_per_seq = seq // tm
    row = lambda i: (i, 0)
    fixed = lambda i: (0, 0)
    halo = lambda i: (jnp.maximum(i * (tm // CONV_HALO) - 1, 0), 0)
    per_seq = lambda i: (i // tiles_per_seq, 0)
    full = _resident
    return pl.pallas_call(
        functools.partial(_prompt_post_kernel, tiles_per_seq=tiles_per_seq),
        grid=(n // tm,),
        in_specs=[pl.BlockSpec((tm, d), row), pl.BlockSpec((tm, ATT_WIDTH), row),
                  pl.BlockSpec((tm, cw), row), pl.BlockSpec((CONV_HALO, cw), halo),
                  full(conv_w), full(conv_b), full(ln_g), full(ln_b), full(w_out),
                  full(gx), full(w_xq),
                  pl.BlockSpec((n_mem, d), per_seq), pl.BlockSpec((n_mem, d), per_seq), full(w_xo)],
        out_specs=pl.BlockSpec((tm, d), row),
        out_shape=jax.ShapeDtypeStruct((n, d), F32),
        scratch_shapes=[pltpu.VMEM((CONV_HALO + tm, cw), F32)],
        compiler_params=_params(1),
        name="prompt_mix_cross",
    )(x, att, u, u, conv_w, conv_b, ln_g, ln_b, w_out, gx, w_xq, mk, mv, w_xo)


def _sample_mix_kernel(x_ref, att_ref, uf_ref, cw_ref, cb_ref, lg_ref, lb_ref, wo_ref, gx_ref, wq_ref,
                       x1_ref, qx_ref):
    nb, t_full, cw = uf_ref.shape
    t_new = t_full - (CONV_K - 1)
    y = jnp.zeros((nb, t_new, cw), F32)
    for j in range(CONV_K):
        y = y + cw_ref[j:j + 1, :] * uf_ref[:, j:j + t_new, :]
    c = _conv_tail(y.reshape(nb * t_new, cw), cb_ref, lg_ref, lb_ref)
    x1 = _mix_out(x_ref[...], att_ref[...], c, wo_ref)
    x1_ref[...] = x1
    hd = wq_ref.shape[1] // X_HEADS
    qx_ref[...] = _dot(_rms(x1, gx_ref[...]).astype(BF16), wq_ref[...]) * (hd ** -0.5 * LOG2E)


def _sample_mix(x, att, u_full, conv_w, conv_b, ln_g, ln_b, w_out, gx, w_xq):
    n, d = x.shape
    return pl.pallas_call(
        _sample_mix_kernel,
        out_shape=[jax.ShapeDtypeStruct((n, d), F32), jax.ShapeDtypeStruct((n, w_xq.shape[1]), F32)],
        compiler_params=pltpu.CompilerParams(vmem_limit_bytes=VMEM_LIMIT),
        name="sample_mix",
    )(x, att, u_full, conv_w, conv_b, ln_g, ln_b, w_out, gx, w_xq)


def _sample_cross_kernel(qx_ref, mk_ref, mv_ref, o_ref, *, n_mem):
    n_seq = mk_ref.shape[0]
    t_new = qx_ref.shape[0] // n_seq
    groups = mk_ref.shape[1] // (n_mem * X_HEADS)
    pitch = groups * X_HEADS

    def head_rows(ref, i, h):
        return jnp.concatenate([ref[i, pl.ds(j * X_HEADS + h, n_mem, stride=pitch), :] for j in range(groups)],
                               axis=1).astype(BF16)

    for i in range(n_seq):
        tok = slice(i * t_new, (i + 1) * t_new)
        o_ref[tok, :] = _cross_attend(qx_ref[tok, :],
                                      lambda h, i=i: (head_rows(mk_ref, i, h), head_rows(mv_ref, i, h)))


def _swiglu_final(x2, gf_ref, wg_ref, wu_ref, wd_ref, gfin_ref, ff_chunk):
    h = _rms(x2, gf_ref[...]).astype(BF16)
    acc = x2
    for lo in range(0, wg_ref.shape[1], ff_chunk):
        gate = _dot(h, wg_ref[:, lo:lo + ff_chunk])
        up = _dot(h, wu_ref[:, lo:lo + ff_chunk])
        acc = acc + _dot((gate * _sigmoid(gate) * up).astype(BF16), wd_ref[lo:lo + ff_chunk, :])
    return _rms(acc, gfin_ref[...])


def _ffn_kernel(x_ref, gf_ref, wg_ref, wu_ref, wd_ref, gfin_ref, sqx_ref, smk_ref, smv_ref, o_ref, sxo_ref, *,
                ff_chunk, n_mem):
    _sample_cross_kernel(sqx_ref, smk_ref, smv_ref, sxo_ref, n_mem=n_mem)
    o_ref[...] = _swiglu_final(x_ref[...], gf_ref, wg_ref, wu_ref, wd_ref, gfin_ref, ff_chunk)


def _sample_tail_kernel(x_ref, xo_ref, wxo_ref, gf_ref, wg_ref, wu_ref, wd_ref, gfin_ref, o_ref, *, ff_chunk):
    x2 = x_ref[...] + _dot(xo_ref[...], wxo_ref[...])
    o_ref[...] = _swiglu_final(x2, gf_ref, wg_ref, wu_ref, wd_ref, gfin_ref, ff_chunk)


FF_CHUNK = 256


def _resident(a):
    return pl.BlockSpec(a.shape, lambda *_: (0,) * a.ndim, pipeline_mode=pl.Buffered(1))


def _ffn(x, gf, wg, wu, wd, gfin, tm, sqx, mem_k, mem_v, t_new):
    n, d = x.shape
    nb, n_mem, n_heads, hd = mem_k.shape
    assert nb == n // tm, "one sample sequence per prompt row tile"
    rows = lambda a: jnp.transpose(a.reshape(nb, n_mem, n_heads, hd // PAIR_W, PAIR_W),
                                   (0, 1, 3, 2, 4)).reshape(nb, -1, PAIR_W)
    row = lambda i: (i, 0)
    full = _resident
    tok = pl.BlockSpec((t_new, n_heads * hd), row)
    mem = pl.BlockSpec((1, n_mem * n_heads * hd // PAIR_W, PAIR_W), lambda i: (i, 0, 0))
    return pl.pallas_call(
        functools.partial(_ffn_kernel, ff_chunk=FF_CHUNK, n_mem=n_mem),
        grid=(n // tm,),
        in_specs=[pl.BlockSpec((tm, d), row), full(gf), full(wg), full(wu), full(wd), full(gfin), tok, mem, mem],
        out_specs=[pl.BlockSpec((tm, d), row), tok],
        out_shape=[jax.ShapeDtypeStruct((n, d), F32), jax.ShapeDtypeStruct(sqx.shape, BF16)],
        compiler_params=_params(1),
        name="swiglu_final_norm",
    )(x, gf, wg, wu, wd, gfin, sqx, rows(mem_k), rows(mem_v))


def _sample_tail(x1, xo, w_xo, gf, wg, wu, wd, gfin):
    return pl.pallas_call(
        functools.partial(_sample_tail_kernel, ff_chunk=FF_CHUNK),
        out_shape=jax.ShapeDtypeStruct(x1.shape, F32),
        compiler_params=pltpu.CompilerParams(vmem_limit_bytes=VMEM_LIMIT),
        name="sample_cross_out_swiglu",
    )(x1, xo, w_xo, gf, wg, wu, wd, gfin)


def _mem_kv_kernel(m_ref, g_ref, wk_ref, wv_ref, k_ref, v_ref, kb_ref, vb_ref):
    n_mem = m_ref.shape[0]
    hd = wk_ref.shape[1] // X_HEADS
    groups = hd // PAIR_W
    h = _rms(m_ref[...], g_ref[...]).astype(BF16)
    for w_ref, o_ref, ob_ref in ((wk_ref, k_ref, kb_ref), (wv_ref, v_ref, vb_ref)):
        val = _dot(h, w_ref[...])
        ob_ref[...] = val.astype(BF16)
        for head in range(X_HEADS):
            for j in range(groups):
                lanes = slice(head * hd + j * PAIR_W, head * hd + (j + 1) * PAIR_W)
                o_ref[0, pl.ds(j * X_HEADS + head, n_mem, stride=groups * X_HEADS), :] = val[:, lanes]


def _mem_kv(mem, g, w_k, w_v, n_mem):
    n, d = mem.shape
    row = lambda i: (i, 0)
    full = lambda a: pl.BlockSpec(a.shape, lambda i: (0, 0))
    dk = w_k.shape[1]
    rows_spec = pl.BlockSpec((1, n_mem * dk // PAIR_W, PAIR_W), lambda i: (i, 0, 0))
    rows_shape = jax.ShapeDtypeStruct((n // n_mem, n_mem * dk // PAIR_W, PAIR_W), F32)
    return pl.pallas_call(
        _mem_kv_kernel,
        grid=(n // n_mem,),
        in_specs=[pl.BlockSpec((n_mem, d), row), full(g), full(w_k), full(w_v)],
        out_specs=[rows_spec, rows_spec, pl.BlockSpec((n_mem, dk), row), pl.BlockSpec((n_mem, dk), row)],
        out_shape=[rows_shape, rows_shape] + [jax.ShapeDtypeStruct((n, dk), BF16)] * 2,
        compiler_params=_params(1),
        name="mem_kv",
    )(mem, g, w_k, w_v)


ROW_TILE = 512
FFN_ROW_TILE = 1024
ATT_CHUNK = STEPS * DILATED_BRANCHES[-1][1]


def kernel(x_prompt, x_sample, mem_prompt, cache_win_k, cache_win_v, cache_conv, cache_mem_k, cache_mem_v,
           rpb_table, norm_mix_g, w_in, conv_w, conv_b, conv_ln_g, conv_ln_b, w_out, norm_x_g, norm_mem_g,
           w_xq, w_xk, w_xv, w_xo, norm_ffn_g, w_ffn_gate, w_ffn_up, w_ffn_down, norm_final_g):
    depth = w_in.shape[0]
    assert depth == 1, "single-layer stack"
    batch, seq, d = x_prompt.shape
    nb, t_new, _ = x_sample.shape
    buf_len = cache_win_k.shape[2]
    keep_p = min(MAX_DISTANCE, seq)
    n_mem = mem_prompt.shape[1]
    conv_hist = CONV_K - 1
    assert seq % ATT_CHUNK == 0 and keep_p % ATT_CHUNK == 0 and buf_len == MAX_DISTANCE

    row = lambda a: a.reshape(1, -1)
    bf = lambda a: a.astype(BF16)
    l = 0
    w_in_b = bf(w_in[l])
    g_mix, g_x, g_mem, g_ffn, g_fin = (row(norm_mix_g[l]), row(norm_x_g[l]), row(norm_mem_g[l]),
                                       row(norm_ffn_g[l]), row(norm_final_g))
    cv_w, cv_b, ln_g, ln_b = conv_w[l], row(conv_b[l]), row(conv_ln_g[l]), row(conv_ln_b[l])

    bias_p = _bias_tables(rpb_table, _prompt_bucket_index(), masked_cols=STEPS)
    bias_s = _bias_tables(rpb_table, _sample_bucket_index(buf_len, t_new))[0]

    xp = x_prompt.reshape(batch * seq, d)
    xs = x_sample.reshape(nb * t_new, d)
    (q, k, v, u, p_wk_t, p_wv_t,
     w_out_b, w_xq_b, w_xk_b, w_xv_b, w_xo_b, w_g_b, w_u_b, w_d_b) = _in_proj_prompt(
        xp, g_mix, w_in_b, [w_out[l], w_xq[l], w_xk[l], w_xv[l], w_xo[l], w_ffn_gate[l], w_ffn_up[l], w_ffn_down[l]],
        seq, keep_p, ROW_TILE)
    qs, us, ks_t, vs_t = _in_proj_sample(xs, g_mix, w_in_b)
    chan_major = lambda a: jnp.transpose(a, (0, 2, 3, 1)).reshape(nb, ATT_WIDTH, -1)
    att, s_wk_t, s_wv_t, att_s = _dilated_attention_both(
        q, k, v, bias_p, seq // ATT_CHUNK,
        qs, ks_t, vs_t, chan_major(cache_win_k[l]), chan_major(cache_win_v[l]), bias_s, t_new)

    mk, mv, mk_b, mv_b = _mem_kv(mem_prompt.reshape(batch * n_mem, d), g_mem, w_xk_b, w_xv_b, n_mem)
    x2 = _prompt_post(xp, att, u, cv_w, cv_b, ln_g, ln_b, w_out_b, g_x, w_xq_b, mk_b, mv_b, w_xo_b,
                      seq, FFN_ROW_TILE)

    tok_major = lambda a: jnp.transpose(a.reshape(a.shape[0], N_ATT_HEADS, HEAD_DIM, -1), (0, 3, 1, 2))[None]
    p_wk, p_wv = tok_major(p_wk_t), tok_major(p_wv_t)
    u3 = u.reshape(batch, seq, -1)
    p_conv = u3[:, seq - conv_hist:][None]
    xh = lambda a: jnp.transpose(a.reshape(batch, n_mem, -1, X_HEADS, PAIR_W), (0, 1, 3, 2, 4)).reshape(
        1, batch, n_mem, X_HEADS, -1)
    p_mk, p_mv = xh(mk), xh(mv)

    u_full = jnp.concatenate([cache_conv[l], us.reshape(nb, t_new, -1)], axis=1)
    x1s, qxs = _sample_mix(xs, att_s, u_full, cv_w, cv_b, ln_g, ln_b, w_out_b, g_x, w_xq_b)
    y_prompt, xo_s = _ffn(x2, g_ffn, w_g_b, w_u_b, w_d_b, g_fin, FFN_ROW_TILE,
                          qxs, cache_mem_k[l], cache_mem_v[l], t_new)
    y_prompt = y_prompt.reshape(batch, seq, d)
    y_sample = _sample_tail(x1s, xo_s, w_xo_b, g_ffn, w_g_b, w_u_b, w_d_b, g_fin).reshape(nb, t_new, d)

    return (y_prompt, y_sample, p_wk, p_wv, p_conv, p_mk, p_mv,
            tok_major(s_wk_t), tok_major(s_wv_t), u_full[:, t_new:][None])
```

```python
import functools
import math

import numpy as np
import jax
import jax.numpy as jnp
from jax import lax
from jax.experimental import pallas as pl
from jax.experimental.pallas import tpu as pltpu

F32 = jnp.float32
BF16 = jnp.bfloat16

HEAD_DIM = 64
N_ATT_HEADS = 12
ATT_WIDTH = N_ATT_HEADS * HEAD_DIM
PAIR_W = 2 * HEAD_DIM
N_PAIRS = N_ATT_HEADS // 2
CONV_K = 31
CONV_HALO = 32
F32_SUBLANES = 8
DILATED_BRANCHES = ((128, 1), (512, 4), (2048, 16))
STEPS = 128
N_CLS = DILATED_BRANCHES[-1][1]
GATHER_STRIDE = 4
NEW_LANES = 128
N_BUCKETS = 32
MAX_EXACT = N_BUCKETS // 2
MAX_DISTANCE = 2048
X_HEADS = 4
EPS = 1e-6
LOG2E = math.log2(math.e)
ATT_SCALE = HEAD_DIM ** -0.5 * LOG2E
MASKED = -1e30

V7X_VMEM_BYTES = 64 * 1024 * 1024
VMEM_LIMIT = V7X_VMEM_BYTES * 3 // 4


def _params(n_grid_dims):
    return pltpu.CompilerParams(dimension_semantics=("arbitrary",) * n_grid_dims,
                                vmem_limit_bytes=VMEM_LIMIT)


def _rms(x, g):
    return x * lax.rsqrt(jnp.mean(x * x, axis=-1, keepdims=True) + EPS) * g


def _sigmoid(x):
    return 1.0 / (1.0 + jnp.exp(-x))


def _dot(a, b):
    return jnp.dot(a, b, preferred_element_type=F32)


def _dot_nt(a, b):
    return lax.dot_general(a, b, (((1,), (1,)), ((), ())), preferred_element_type=F32)


def _t5_bucket_np(dist):
    n = dist.astype(np.int32)
    nf = np.maximum(n, MAX_EXACT).astype(np.float32)
    large = MAX_EXACT + (np.log(nf / np.float32(MAX_EXACT)) / np.float32(math.log(MAX_DISTANCE / MAX_EXACT))
                         * np.float32(N_BUCKETS - MAX_EXACT)).astype(np.int32)
    large = np.minimum(large, N_BUCKETS - 1)
    return np.where(n < MAX_EXACT, n, large)


def _block_order(dil):
    groups = N_CLS // dil
    per = STEPS // groups
    i = np.arange(STEPS)
    return (i % per) * groups + i // per


def _prompt_bucket_index():
    out = []
    for _, dil in DILATED_BRANCHES:
        n = _block_order(dil)
        kj = np.concatenate([n, STEPS + n])[None, :]
        sub = STEPS + n[:, None] - kj
        band = (sub >= 0) & (sub <= STEPS)
        out.append(np.where(band, _t5_bucket_np(dil * np.maximum(sub, 0)), -1))
    return np.stack(out).astype(np.int32)


def _sample_bucket_index(buf_len, t_new):
    col = np.arange(buf_len + NEW_LANES)
    key = np.where(col < buf_len, (col + t_new) % buf_len, col - NEW_LANES + t_new)
    keep = (col < buf_len) | (col >= buf_len + NEW_LANES - t_new)
    i = np.arange(t_new)[:, None]
    dist = buf_len + i - key[None, :]
    out = []
    for window, dil in DILATED_BRANCHES:
        ok = keep[None, :] & (dist >= 0) & (dist % dil == 0) & (dist // dil <= window // dil)
        out.append(np.where(ok, _t5_bucket_np(np.maximum(dist, 0)), -1))
    return np.stack(out).astype(np.int32)


def _bias_kernel(tab_ref, idx_ref, o_ref, *, masked_cols, buckets, fold):
    pair = pl.program_id(0)
    rows = idx_ref.shape[1]
    for half in range(2):
        head = 2 * pair + half
        out_rows = slice(half * rows, (half + 1) * rows)
        per_branch = []
        for g, present in enumerate(buckets):
            idx = idx_ref[g]
            acc = jnp.full(idx.shape, MASKED, F32)
            for b in present:
                acc = jnp.where(idx == b, tab_ref[b, head] * LOG2E, acc)
            per_branch.append(acc)
            if not fold:
                o_ref[0, g, 0, out_rows, :] = acc
                if masked_cols:
                    col = lax.broadcasted_iota(jnp.int32, idx.shape, 1)
                    o_ref[1, g, 0, out_rows, :] = jnp.where(col < masked_cols, MASKED, acc)
        if fold:
            top = functools.reduce(jnp.maximum, per_branch)
            total = functools.reduce(jnp.add, [jnp.exp2(b - top) for b in per_branch])
            o_ref[0, 0, 0, out_rows, :] = top + jnp.log(total) * LOG2E


def _bias_tables(table, idx, masked_cols=0, fold=False):
    n_var, rows, cols = idx.shape
    n_out = 2 if masked_cols else 1
    buckets = tuple(tuple(int(b) for b in np.unique(idx[g]) if b >= 0) for g in range(n_var))
    n_tab = 1 if fold else n_var
    return pl.pallas_call(
        functools.partial(_bias_kernel, masked_cols=masked_cols, buckets=buckets, fold=fold),
        grid=(N_PAIRS,),
        in_specs=[pl.BlockSpec(memory_space=pltpu.SMEM),
                  pl.BlockSpec((n_var, rows, cols), lambda p: (0, 0, 0))],
        out_specs=pl.BlockSpec((n_out, n_tab, 1, 2 * rows, cols), lambda p: (0, 0, p, 0, 0)),
        out_shape=jax.ShapeDtypeStruct((n_out, n_tab, N_PAIRS, 2 * rows, cols), F32),
        compiler_params=_params(1),
        name="bias_tables",
    )(table, jnp.asarray(idx))


def _in_proj_sample_kernel(x_ref, g_ref, w_ref, q_ref, u_ref, kt_ref, vt_ref):
    h = _rms(x_ref[...], g_ref[...]).astype(BF16)
    aw = q_ref.shape[1]
    cw = u_ref.shape[1]
    q_ref[...] = _dot(h, w_ref[:, 0:aw]) * ATT_SCALE
    a = _dot(h, w_ref[:, 3 * aw:3 * aw + cw])
    gate = _dot(h, w_ref[:, 3 * aw + cw:3 * aw + 2 * cw])
    u_ref[...] = a * _sigmoid(gate)
    kt_ref[...] = _dot(h, w_ref[:, aw:2 * aw]).T
    vt_ref[...] = _dot(h, w_ref[:, 2 * aw:3 * aw]).T


def _in_proj_sample(x, g, w_bf16):
    n, d = x.shape
    aw = ATT_WIDTH
    cw = (w_bf16.shape[1] - 3 * aw) // 2
    new_shape = jax.ShapeDtypeStruct((aw, n), F32)
    return pl.pallas_call(
        _in_proj_sample_kernel,
        out_shape=[jax.ShapeDtypeStruct((n, aw), F32), jax.ShapeDtypeStruct((n, cw), F32), new_shape, new_shape],
        compiler_params=pltpu.CompilerParams(vmem_limit_bytes=VMEM_LIMIT),
        name="in_proj_sample",
    )(x, g, w_bf16)


def _in_proj_prompt_kernel(*refs, tiles_per_seq, tail_first, side_blocks):
    n_side = len(side_blocks)
    x_ref, g_ref, w_ref = refs[:3]
    side_in = refs[3:3 + n_side]
    q_ref, k_ref, v_ref, u_ref, kt_ref, vt_ref = refs[3 + n_side:9 + n_side]
    side_out = refs[9 + n_side:9 + 2 * n_side]
    xs, xs2 = refs[9 + 2 * n_side:]
    for src, dst, n_blk in zip(side_in, side_out, side_blocks):
        @pl.when(pl.program_id(0) < n_blk)
        def _(src=src, dst=dst):
            dst[...] = src[...].astype(dst.dtype)

    tm, d = x_ref.shape
    per = tm // N_CLS
    aw = q_ref.shape[3]
    cw = u_ref.shape[1]
    xn = _rms(x_ref[...], g_ref[...])
    hn = xn.astype(BF16)
    gate = _dot(hn, w_ref[:, 3 * aw + cw:3 * aw + 2 * cw])
    u_ref[...] = _dot(hn, w_ref[:, 3 * aw:3 * aw + cw]) * _sigmoid(gate)

    quarter = tm // GATHER_STRIDE
    for c in range(d // PAIR_W):
        xs[c] = xn[:, c * PAIR_W:(c + 1) * PAIR_W]
    for c in range(d // PAIR_W):
        for res in range(GATHER_STRIDE):
            xs2[c, res * quarter:(res + 1) * quarter, :] = xs[c, pl.ds(res, quarter, stride=GATHER_STRIDE), :]

    def class_rows(c, r):
        start = (r % GATHER_STRIDE) * quarter + r // GATHER_STRIDE
        return xs2[c, pl.ds(start, per, stride=GATHER_STRIDE), :]

    h = jnp.concatenate(
        [jnp.concatenate([class_rows(c, r) for r in range(N_CLS)], axis=0) for c in range(d // PAIR_W)],
        axis=1).astype(BF16)
    q_ref[0] = (_dot(h, w_ref[:, 0:aw]) * ATT_SCALE).reshape(N_CLS, per, aw)
    k_ref[0] = _dot(h, w_ref[:, aw:2 * aw]).reshape(N_CLS, per, aw)
    v_ref[0] = _dot(h, w_ref[:, 2 * aw:3 * aw]).reshape(N_CLS, per, aw)

    @pl.when(pl.program_id(0) % tiles_per_seq >= tail_first)
    def _():
        kt_ref[0] = _dot(hn, w_ref[:, aw:2 * aw]).T
        vt_ref[0] = _dot(hn, w_ref[:, 2 * aw:3 * aw]).T


BF16_SUBLANES = 16


def _side_block_rows(rows, n_steps):
    return next(br for br in range(BF16_SUBLANES, rows + 1, BF16_SUBLANES)
                if rows % br == 0 and rows // br <= n_steps)


def _in_proj_prompt(x, g, w_bf16, side_weights, seq, keep, tm):
    n, d = x.shape
    n_steps = n // tm
    side_rows = [_side_block_rows(w.shape[0], n_steps) for w in side_weights]
    side_blocks = tuple(w.shape[0] // br for w, br in zip(side_weights, side_rows))
    side_specs = [pl.BlockSpec((br, w.shape[1]), lambda i, last=nb - 1: (jnp.minimum(i, last), 0))
                  for w, br, nb in zip(side_weights, side_rows, side_blocks)]
    aw = ATT_WIDTH
    cw = (w_bf16.shape[1] - 3 * aw) // 2
    chunk = N_CLS * STEPS
    tiles_per_chunk = chunk // tm
    tiles_per_seq = seq // tm
    tail_first = (seq - keep) // tm
    row = lambda i: (i, 0)
    fixed = lambda i: (0, 0)
    cls = pl.BlockSpec((1, N_CLS, tm // N_CLS, aw), lambda i: (i // tiles_per_chunk, 0, i % tiles_per_chunk, 0))
    tail = pl.BlockSpec((1, aw, tm),
                        lambda i: (i // tiles_per_seq, 0, jnp.maximum(i % tiles_per_seq - tail_first, 0)))
    cls_shape = jax.ShapeDtypeStruct((n // chunk, N_CLS, STEPS, aw), F32)
    tail_shape = jax.ShapeDtypeStruct((n // seq, aw, keep), F32)
    return pl.pallas_call(
        functools.partial(_in_proj_prompt_kernel, tiles_per_seq=tiles_per_seq, tail_first=tail_first,
                          side_blocks=side_blocks),
        grid=(n_steps,),
        in_specs=[pl.BlockSpec((tm, d), row), pl.BlockSpec((1, d), fixed),
                  pl.BlockSpec(w_bf16.shape, fixed)] + side_specs,
        out_specs=[cls, cls, cls, pl.BlockSpec((tm, cw), row), tail, tail] + side_specs,
        out_shape=[cls_shape, cls_shape, cls_shape, jax.ShapeDtypeStruct((n, cw), F32), tail_shape, tail_shape]
                  + [jax.ShapeDtypeStruct(w.shape, BF16) for w in side_weights],
        scratch_shapes=[pltpu.VMEM((d // PAIR_W, tm, PAIR_W), F32)] * 2,
        compiler_params=_params(1),
        name="in_proj_prompt",
    )(x, g, w_bf16, *side_weights)


def _stack_heads(x, first):
    zero = jnp.zeros_like(x)
    return jnp.concatenate([jnp.where(first, x, zero), jnp.where(first, zero, x)], axis=0)


def _att_kernel(q_ref, kc_ref, kp_ref, vc_ref, vp_ref, bias_ref, bias0_ref, o_ref, num_s, m_s, l_s, nat_s):
    first = lax.broadcasted_iota(jnp.int32, (STEPS, PAIR_W), 1) < HEAD_DIM
    ones = jnp.ones((2 * STEPS, PAIR_W), BF16)

    def pieces(dil, cls, blk):
        groups = N_CLS // dil
        per = STEPS // groups
        return [(cls + dil * j, pl.ds(blk * per, per)) for j in range(groups)], per

    def load(ref, dil, cls, blk):
        idx, _ = pieces(dil, cls, blk)
        return jnp.concatenate([ref[0, r, rows, :] for r, rows in idx], axis=0)

    def store(ref, g, dil, cls, blk, val):
        idx, per = pieces(dil, cls, blk)
        for j, (r, rows) in enumerate(idx):
            ref[g, r, rows, :] = val[j * per:(j + 1) * per]

    def attend(g, dil, cls, blk, k_prev, v_prev, bias):
        q_st = _stack_heads(load(q_ref, dil, cls, blk), first).astype(BF16)
        k_own = load(kc_ref, dil, cls, blk).astype(BF16)
        v_own = load(vc_ref, dil, cls, blk).astype(BF16)
        kcat = jnp.concatenate([k_prev, k_own], axis=0)
        vcat = jnp.concatenate([v_prev, v_own], axis=0)
        s = _dot_nt(q_st, kcat) + bias
        m = jnp.max(s, axis=-1, keepdims=True)
        p = jnp.exp2(s - m).astype(BF16)
        o = _dot(p, jnp.concatenate([vcat, ones], axis=1))
        store(num_s, g, dil, cls, blk, jnp.where(first, o[:STEPS, :PAIR_W], o[STEPS:, :PAIR_W]))
        store(l_s, g, dil, cls, blk, jnp.where(first, o[:STEPS, PAIR_W:], o[STEPS:, PAIR_W:]))
        store(m_s, g, dil, cls, blk, jnp.where(first, jnp.broadcast_to(m[:STEPS], (STEPS, PAIR_W)),
                                               jnp.broadcast_to(m[STEPS:], (STEPS, PAIR_W))))
        return k_own, v_own

    carried = {}
    for i in range(N_CLS):
        for g, (_, dil) in reversed(list(enumerate(DILATED_BRANCHES))):
            n_blk = N_CLS // dil
            cls, blk = i % dil, i // dil
            if blk == 0:
                prev = (load(kp_ref, dil, cls, n_blk - 1).astype(BF16), load(vp_ref, dil, cls, n_blk - 1).astype(BF16))
                bias = bias0_ref[0, g, 0]
            else:
                prev, bias = carried[g, cls], bias_ref[g, 0]
            carried[g, cls] = attend(g, dil, cls, blk, *prev, bias)

    quarter = N_CLS * STEPS // GATHER_STRIDE
    for r in range(N_CLS):
        m_all = jnp.maximum(jnp.maximum(m_s[0, r], m_s[1, r]), m_s[2, r])
        num = jnp.zeros(m_all.shape, F32)
        den = jnp.zeros(m_all.shape, F32)
        for g in range(len(DILATED_BRANCHES)):
            w = jnp.exp2(m_s[g, r] - m_all)
            num = num + w * num_s[g, r]
            den = den + w * l_s[g, r]
        start = (r % GATHER_STRIDE) * quarter + r // GATHER_STRIDE
        nat_s[pl.ds(start, STEPS, stride=GATHER_STRIDE), :] = num / den
    for res in range(GATHER_STRIDE):
        o_ref[pl.ds(res, quarter, stride=GATHER_STRIDE), :] = nat_s[res * quarter:(res + 1) * quarter, :]


def _sample_att_body(seq, q_ref, kn_ref, vn_ref, ck_ref, cv_ref, bias_ref, ok_ref, ov_ref, o_ref):
    buf_len = ck_ref.shape[2]
    t_new = q_ref.shape[0]
    tail = buf_len - NEW_LANES
    is_new = lax.broadcasted_iota(jnp.int32, (PAIR_W, NEW_LANES), 1) >= NEW_LANES - t_new
    first = lax.broadcasted_iota(jnp.int32, (t_new, PAIR_W), 1) < HEAD_DIM
    col0 = (seq % (NEW_LANES // t_new)) * t_new
    to_end = (NEW_LANES - t_new - col0) % NEW_LANES

    for j in range(ck_ref.shape[1] // PAIR_W):
        ch = slice(j * PAIR_W, (j + 1) * PAIR_W)

        def shift_in(c_ref, n_ref, o_ref):
            rot = pltpu.roll(c_ref[0, ch, :], buf_len - t_new, axis=1)
            new = jnp.where(is_new, pltpu.roll(n_ref[ch, :], to_end, axis=1), 0.0)
            o_ref[0, ch, :tail] = rot[:, :tail]
            o_ref[0, ch, tail:] = jnp.where(is_new, new, rot[:, tail:])
            return rot.astype(BF16), new.astype(BF16)

        rot_k, new_k = shift_in(ck_ref, kn_ref, ok_ref)
        rot_v, new_v = shift_in(cv_ref, vn_ref, ov_ref)

        q_st = _stack_heads(q_ref[:, ch], first).astype(BF16)
        s = jnp.concatenate([_dot(q_st, rot_k), _dot(q_st, new_k)], axis=1)
        s = s + bias_ref[0, j]
        p = jnp.exp2(s - jnp.max(s, axis=-1, keepdims=True))
        den = jnp.sum(p, axis=-1, keepdims=True)
        pb = p.astype(BF16)
        o = (_dot_nt(pb[:, :buf_len], rot_v) + _dot_nt(pb[:, buf_len:], new_v)) / den
        o_ref[:, ch] = jnp.where(first, o[:t_new], o[t_new:]).astype(o_ref.dtype)


def _att_both_kernel(q_ref, kc_ref, kp_ref, vc_ref, vp_ref, bias_ref, bias0_ref,
                     sq_ref, skn_ref, svn_ref, sck_ref, scv_ref, sbias_ref,
                     o_ref, sok_ref, sov_ref, so_ref, num_s, m_s, l_s, nat_s, *, groups_per_seq):
    step = ((pl.program_id(0) * pl.num_programs(1) + pl.program_id(1)) * pl.num_programs(2) + pl.program_id(2))
    _sample_att_body(step // groups_per_seq, sq_ref, skn_ref, svn_ref, sck_ref, scv_ref, sbias_ref,
                     sok_ref, sov_ref, so_ref)
    _att_kernel(q_ref, kc_ref, kp_ref, vc_ref, vp_ref, bias_ref, bias0_ref, o_ref, num_s, m_s, l_s, nat_s)


def _dilated_attention_both(q, k, v, bias, chunks_per_seq, sq, sk_new_t, sv_new_t, cache_k_t, cache_v_t, sbias, t_new):
    n_chunks = q.shape[0]
    n_br = len(DILATED_BRANCHES)
    nb, _, buf_len = cache_k_t.shape
    n_steps = n_chunks * N_PAIRS
    groups_per_seq = n_steps // nb
    assert nb * groups_per_seq == n_steps and N_PAIRS % groups_per_seq == 0, "sample steps must tile the prompt grid"
    width = N_PAIRS // groups_per_seq * PAIR_W

    cur = lambda b, c, p: (b * chunks_per_seq + c, 0, 0, p)
    prev = lambda b, c, p: (b * chunks_per_seq + jnp.maximum(c - 1, 0), 0, 0, p)
    blk = pl.BlockSpec((1, N_CLS, STEPS, PAIR_W), cur)
    blk_prev = pl.BlockSpec((1, N_CLS, STEPS, PAIR_W), prev)

    step = lambda b, c, p: (b * chunks_per_seq + c) * N_PAIRS + p
    s_seq = lambda b, c, p: step(b, c, p) // groups_per_seq
    s_grp = lambda b, c, p: step(b, c, p) % groups_per_seq
    tok = pl.BlockSpec((t_new, width), lambda b, c, p: (s_seq(b, c, p), s_grp(b, c, p)))
    new = pl.BlockSpec((width, NEW_LANES), lambda b, c, p: (s_grp(b, c, p), s_seq(b, c, p) // (NEW_LANES // t_new)))
    buf = pl.BlockSpec((1, width, buf_len), lambda b, c, p: (s_seq(b, c, p), s_grp(b, c, p), 0))
    return pl.pallas_call(
        functools.partial(_att_both_kernel, groups_per_seq=groups_per_seq),
        grid=(n_chunks // chunks_per_seq, chunks_per_seq, N_PAIRS),
        in_specs=[blk, blk, blk_prev, blk, blk_prev,
                  pl.BlockSpec((n_br, 1, 2 * STEPS, 2 * STEPS), lambda b, c, p: (0, p, 0, 0)),
                  pl.BlockSpec((1, n_br, 1, 2 * STEPS, 2 * STEPS),
                               lambda b, c, p: (jnp.where(c == 0, 1, 0), 0, p, 0, 0)),
                  tok, new, new, buf, buf,
                  pl.BlockSpec((sbias.shape[0], width // PAIR_W, 2 * t_new, buf_len + NEW_LANES),
                               lambda b, c, p: (0, s_grp(b, c, p), 0, 0))],
        out_specs=[pl.BlockSpec((N_CLS * STEPS, PAIR_W), lambda b, c, p: (b * chunks_per_seq + c, p)), buf, buf, tok],
        out_shape=[jax.ShapeDtypeStruct((n_chunks * N_CLS * STEPS, ATT_WIDTH), F32),
                   jax.ShapeDtypeStruct(cache_k_t.shape, F32), jax.ShapeDtypeStruct(cache_v_t.shape, F32),
                   jax.ShapeDtypeStruct((nb * t_new, ATT_WIDTH), BF16)],
        scratch_shapes=[pltpu.VMEM((n_br, N_CLS, STEPS, PAIR_W), F32)] * 3 + [pltpu.VMEM((N_CLS * STEPS, PAIR_W), F32)],
        compiler_params=_params(3),
        name="dilated_attention",
    )(q, k, k, v, v, bias[0], bias, sq, sk_new_t, sv_new_t, cache_k_t, cache_v_t, sbias)


def _conv_tail(y, cb_ref, lg_ref, lb_ref):
    y = y + cb_ref[...]
    mu = jnp.mean(y, axis=-1, keepdims=True)
    yc = y - mu
    var = jnp.mean(yc * yc, axis=-1, keepdims=True)
    yn = yc * lax.rsqrt(var + EPS) * lg_ref[...] + lb_ref[...]
    return yn * _sigmoid(yn)


def _mix_out(x, att, c, wo_ref):
    aw = att.shape[1]
    return x + _dot(att.astype(BF16), wo_ref[:aw, :]) + _dot(c.astype(BF16), wo_ref[aw:, :])


def _cross_attend(qx, head_kv):
    hd = qx.shape[1] // X_HEADS
    outs = []
    for h in range(X_HEADS):
        mk, mv = head_kv(h)
        s = _dot_nt(qx[:, h * hd:(h + 1) * hd].astype(BF16), mk)
        p = jnp.exp2(s - jnp.max(s, axis=-1, keepdims=True))
        den = jnp.sum(p, axis=-1, keepdims=True)
        outs.append((_dot(p.astype(BF16), mv) / den).astype(BF16))
    return jnp.concatenate(outs, axis=1)


def _prompt_post_kernel(x_ref, att_ref, u_ref, uh_ref, cw_ref, cb_ref, lg_ref, lb_ref, wo_ref,
                        gx_ref, wq_ref, mk_ref, mv_ref, wxo_ref, o_ref, ubuf, *, tiles_per_seq):
    tm = x_ref.shape[0]
    seq_start = (pl.program_id(0) % tiles_per_seq) == 0
    ubuf[:CONV_HALO, :] = jnp.where(seq_start, 0.0, uh_ref[...])
    ubuf[CONV_HALO:, :] = u_ref[...]
    hist = ubuf[...]
    n_rows = hist.shape[0]
    first_tap = CONV_HALO - (CONV_K - 1)
    y = jnp.zeros(u_ref.shape, F32)
    for s in range(F32_SUBLANES):
        shifted = hist if s == 0 else pltpu.roll(hist, n_rows - s, axis=0)
        for j in range(CONV_K):
            if (first_tap + j) % F32_SUBLANES == s:
                lo = first_tap + j - s
                y = y + cw_ref[j:j + 1, :] * shifted[lo:lo + tm]
    c = _conv_tail(y, cb_ref, lg_ref, lb_ref)
    x1 = _mix_out(x_ref[...], att_ref[...], c, wo_ref)
    hd = wq_ref.shape[1] // X_HEADS
    qx = _dot(_rms(x1, gx_ref[...]).astype(BF16), wq_ref[...]) * (hd ** -0.5 * LOG2E)
    head_kv = lambda h: (mk_ref[:, h * hd:(h + 1) * hd], mv_ref[:, h * hd:(h + 1) * hd])
    o_ref[...] = x1 + _dot(_cross_attend(qx, head_kv), wxo_ref[...])


def _prompt_post(x, att, u, conv_w, conv_b, ln_g, ln_b, w_out, gx, w_xq, mk, mv, w_xo, seq, tm):
    n, d = x.shape
    cw = u.shape[1]
    n_mem = mk.shape[0] // (n // seq)
    tiles_per_seq = seq // tm
    row = lambda i: (i, 0)
    fixed = lambda i: (0, 0)
    halo = lambda i: (jnp.maximum(i * (tm // CONV_HALO) - 1, 0), 0)
    per_seq = lambda i: (i // tiles_per_seq, 0)
    full = _resident
    return pl.pallas_call(
        functools.partial(_prompt_post_kernel, tiles_per_seq=tiles_per_seq),
        grid=(n // tm,),
        in_specs=[pl.BlockSpec((tm, d), row), pl.BlockSpec((tm, ATT_WIDTH), row),
                  pl.BlockSpec((tm, cw), row), pl.BlockSpec((CONV_HALO, cw), halo),
                  full(conv_w), full(conv_b), full(ln_g), full(ln_b), full(w_out),
                  full(gx), full(w_xq),
                  pl.BlockSpec((n_mem, d), per_seq), pl.BlockSpec((n_mem, d), per_seq), full(w_xo)],
        out_specs=pl.BlockSpec((tm, d), row),
        out_shape=jax.ShapeDtypeStruct((n, d), F32),
        scratch_shapes=[pltpu.VMEM((CONV_HALO + tm, cw), F32)],
        compiler_params=_params(1),
        name="prompt_mix_cross",
    )(x, att, u, u, conv_w, conv_b, ln_g, ln_b, w_out, gx, w_xq, mk, mv, w_xo)


def _sample_mix_kernel(x_ref, att_ref, uf_ref, cw_ref, cb_ref, lg_ref, lb_ref, wo_ref, gx_ref, wq_ref,
                       x1_ref, qx_ref):
    nb, t_full, cw = uf_ref.shape
    t_new = t_full - (CONV_K - 1)
    y = jnp.zeros((nb, t_new, cw), F32)
    for j in range(CONV_K):
        y = y + cw_ref[j:j + 1, :] * uf_ref[:, j:j + t_new, :]
    c = _conv_tail(y.reshape(nb * t_new, cw), cb_ref, lg_ref, lb_ref)
    x1 = _mix_out(x_ref[...], att_ref[...], c, wo_ref)
    x1_ref[...] = x1
    hd = wq_ref.shape[1] // X_HEADS
    qx_ref[...] = _dot(_rms(x1, gx_ref[...]).astype(BF16), wq_ref[...]) * (hd ** -0.5 * LOG2E)


def _sample_mix(x, att, u_full, conv_w, conv_b, ln_g, ln_b, w_out, gx, w_xq):
    n, d = x.shape
    return pl.pallas_call(
        _sample_mix_kernel,
        out_shape=[jax.ShapeDtypeStruct((n, d), F32), jax.ShapeDtypeStruct((n, w_xq.shape[1]), F32)],
        compiler_params=pltpu.CompilerParams(vmem_limit_bytes=VMEM_LIMIT),
        name="sample_mix",
    )(x, att, u_full, conv_w, conv_b, ln_g, ln_b, w_out, gx, w_xq)


def _sample_cross_kernel(qx_ref, mk_ref, mv_ref, o_ref, *, n_mem):
    n_seq = mk_ref.shape[0]
    t_new = qx_ref.shape[0] // n_seq
    groups = mk_ref.shape[1] // (n_mem * X_HEADS)
    pitch = groups * X_HEADS

    def head_rows(ref, i, h):
        return jnp.concatenate([ref[i, pl.ds(j * X_HEADS + h, n_mem, stride=pitch), :] for j in range(groups)],
                               axis=1).astype(BF16)

    for i in range(n_seq):
        tok = slice(i * t_new, (i + 1) * t_new)
        o_ref[tok, :] = _cross_attend(qx_ref[tok, :],
                                      lambda h, i=i: (head_rows(mk_ref, i, h), head_rows(mv_ref, i, h)))


def _swiglu_final(x2, gf_ref, wg_ref, wu_ref, wd_ref, gfin_ref, ff_chunk):
    h = _rms(x2, gf_ref[...]).astype(BF16)
    acc = x2
    for lo in range(0, wg_ref.shape[1], ff_chunk):
        gate = _dot(h, wg_ref[:, lo:lo + ff_chunk])
        up = _dot(h, wu_ref[:, lo:lo + ff_chunk])
        acc = acc + _dot((gate * _sigmoid(gate) * up).astype(BF16), wd_ref[lo:lo + ff_chunk, :])
    return _rms(acc, gfin_ref[...])


def _ffn_kernel(x_ref, gf_ref, wg_ref, wu_ref, wd_ref, gfin_ref, sqx_ref, smk_ref, smv_ref, o_ref, sxo_ref, *,
                ff_chunk, n_mem):
    _sample_cross_kernel(sqx_ref, smk_ref, smv_ref, sxo_ref, n_mem=n_mem)
    o_ref[...] = _swiglu_final(x_ref[...], gf_ref, wg_ref, wu_ref, wd_ref, gfin_ref, ff_chunk)


def _sample_tail_kernel(x_ref, xo_ref, wxo_ref, gf_ref, wg_ref, wu_ref, wd_ref, gfin_ref, o_ref, *, ff_chunk):
    x2 = x_ref[...] + _dot(xo_ref[...], wxo_ref[...])
    o_ref[...] = _swiglu_final(x2, gf_ref, wg_ref, wu_ref, wd_ref, gfin_ref, ff_chunk)


FF_CHUNK = 256


def _resident(a):
    return pl.BlockSpec(a.shape, lambda *_: (0,) * a.ndim, pipeline_mode=pl.Buffered(1))


def _ffn(x, gf, wg, wu, wd, gfin, tm, sqx, mem_k, mem_v, t_new):
    n, d = x.shape
    nb, n_mem, n_heads, hd = mem_k.shape
    assert nb == n // tm, "one sample sequence per prompt row tile"
    rows = lambda a: jnp.transpose(a.reshape(nb, n_mem, n_heads, hd // PAIR_W, PAIR_W),
                                   (0, 1, 3, 2, 4)).reshape(nb, -1, PAIR_W)
    row = lambda i: (i, 0)
    full = _resident
    tok = pl.BlockSpec((t_new, n_heads * hd), row)
    mem = pl.BlockSpec((1, n_mem * n_heads * hd // PAIR_W, PAIR_W), lambda i: (i, 0, 0))
    return pl.pallas_call(
        functools.partial(_ffn_kernel, ff_chunk=FF_CHUNK, n_mem=n_mem),
        grid=(n // tm,),
        in_specs=[pl.BlockSpec((tm, d), row), full(gf), full(wg), full(wu), full(wd), full(gfin), tok, mem, mem],
        out_specs=[pl.BlockSpec((tm, d), row), tok],
        out_shape=[jax.ShapeDtypeStruct((n, d), F32), jax.ShapeDtypeStruct(sqx.shape, BF16)],
        compiler_params=_params(1),
        name="swiglu_final_norm",
    )(x, gf, wg, wu, wd, gfin, sqx, rows(mem_k), rows(mem_v))


def _sample_tail(x1, xo, w_xo, gf, wg, wu, wd, gfin):
    return pl.pallas_call(
        functools.partial(_sample_tail_kernel, ff_chunk=FF_CHUNK),
        out_shape=jax.ShapeDtypeStruct(x1.shape, F32),
        compiler_params=pltpu.CompilerParams(vmem_limit_bytes=VMEM_LIMIT),
        name="sample_cross_out_swiglu",
    )(x1, xo, w_xo, gf, wg, wu, wd, gfin)


def _mem_kv_kernel(m_ref, g_ref, wk_ref, wv_ref, k_ref, v_ref, kb_ref, vb_ref):
    n_mem = m_ref.shape[0]
    hd = wk_ref.shape[1] // X_HEADS
    groups = hd // PAIR_W
    h = _rms(m_ref[...], g_ref[...]).astype(BF16)
    for w_ref, o_ref, ob_ref in ((wk_ref, k_ref, kb_ref), (wv_ref, v_ref, vb_ref)):
        val = _dot(h, w_ref[...])
        ob_ref[...] = val.astype(BF16)
        for head in range(X_HEADS):
            for j in range(groups):
                lanes = slice(head * hd + j * PAIR_W, head * hd + (j + 1) * PAIR_W)
                o_ref[0, pl.ds(j * X_HEADS + head, n_mem, stride=groups * X_HEADS), :] = val[:, lanes]


def _mem_kv(mem, g, w_k, w_v, n_mem):
    n, d = mem.shape
    row = lambda i: (i, 0)
    full = lambda a: pl.BlockSpec(a.shape, lambda i: (0, 0))
    dk = w_k.shape[1]
    rows_spec = pl.BlockSpec((1, n_mem * dk // PAIR_W, PAIR_W), lambda i: (i, 0, 0))
    rows_shape = jax.ShapeDtypeStruct((n // n_mem, n_mem * dk // PAIR_W, PAIR_W), F32)
    return pl.pallas_call(
        _mem_kv_kernel,
        grid=(n // n_mem,),
        in_specs=[pl.BlockSpec((n_mem, d), row), full(g), full(w_k), full(w_v)],
        out_specs=[rows_spec, rows_spec, pl.BlockSpec((n_mem, dk), row), pl.BlockSpec((n_mem, dk), row)],
        out_shape=[rows_shape, rows_shape] + [jax.ShapeDtypeStruct((n, dk), BF16)] * 2,
        compiler_params=_params(1),
        name="mem_kv",
    )(mem, g, w_k, w_v)


ROW_TILE = 512
FFN_ROW_TILE = 1024
ATT_CHUNK = STEPS * DILATED_BRANCHES[-1][1]


def kernel(x_prompt, x_sample, mem_prompt, cache_win_k, cache_win_v, cache_conv, cache_mem_k, cache_mem_v,
           rpb_table, norm_mix_g, w_in, conv_w, conv_b, conv_ln_g, conv_ln_b, w_out, norm_x_g, norm_mem_g,
           w_xq, w_xk, w_xv, w_xo, norm_ffn_g, w_ffn_gate, w_ffn_up, w_ffn_down, norm_final_g):
    depth = w_in.shape[0]
    assert depth == 1, "single-layer stack"
    batch, seq, d = x_prompt.shape
    nb, t_new, _ = x_sample.shape
    buf_len = cache_win_k.shape[2]
    keep_p = min(MAX_DISTANCE, seq)
    n_mem = mem_prompt.shape[1]
    conv_hist = CONV_K - 1
    assert seq % ATT_CHUNK == 0 and keep_p % ATT_CHUNK == 0 and buf_len == MAX_DISTANCE

    row = lambda a: a.reshape(1, -1)
    bf = lambda a: a.astype(BF16)
    l = 0
    w_in_b = bf(w_in[l])
    g_mix, g_x, g_mem, g_ffn, g_fin = (row(norm_mix_g[l]), row(norm_x_g[l]), row(norm_mem_g[l]),
                                       row(norm_ffn_g[l]), row(norm_final_g))
    cv_w, cv_b, ln_g, ln_b = conv_w[l], row(conv_b[l]), row(conv_ln_g[l]), row(conv_ln_b[l])

    bias_p = _bias_tables(rpb_table, _prompt_bucket_index(), masked_cols=STEPS)
    bias_s = _bias_tables(rpb_table, _sample_bucket_index(buf_len, t_new), fold=True)[0]

    xp = x_prompt.reshape(batch * seq, d)
    xs = x_sample.reshape(nb * t_new, d)
    (q, k, v, u, p_wk_t, p_wv_t,
     w_out_b, w_xq_b, w_xk_b, w_xv_b, w_xo_b, w_g_b, w_u_b, w_d_b) = _in_proj_prompt(
        xp, g_mix, w_in_b, [w_out[l], w_xq[l], w_xk[l], w_xv[l], w_xo[l], w_ffn_gate[l], w_ffn_up[l], w_ffn_down[l]],
        seq, keep_p, ROW_TILE)
    qs, us, ks_t, vs_t = _in_proj_sample(xs, g_mix, w_in_b)
    chan_major = lambda a: jnp.transpose(a, (0, 2, 3, 1)).reshape(nb, ATT_WIDTH, -1)
    att, s_wk_t, s_wv_t, att_s = _dilated_attention_both(
        q, k, v, bias_p, seq // ATT_CHUNK,
        qs, ks_t, vs_t, chan_major(cache_win_k[l]), chan_major(cache_win_v[l]), bias_s, t_new)

    mk, mv, mk_b, mv_b = _mem_kv(mem_prompt.reshape(batch * n_mem, d), g_mem, w_xk_b, w_xv_b, n_mem)
    x2 = _prompt_post(xp, att, u, cv_w, cv_b, ln_g, ln_b, w_out_b, g_x, w_xq_b, mk_b, mv_b, w_xo_b,
                      seq, FFN_ROW_TILE)

    tok_major = lambda a: jnp.transpose(a.reshape(a.shape[0], N_ATT_HEADS, HEAD_DIM, -1), (0, 3, 1, 2))[None]
    p_wk, p_wv = tok_major(p_wk_t), tok_major(p_wv_t)
    u3 = u.reshape(batch, seq, -1)
    p_conv = u3[:, seq - conv_hist:][None]
    xh = lambda a: jnp.transpose(a.reshape(batch, n_mem, -1, X_HEADS, PAIR_W), (0, 1, 3, 2, 4)).reshape(
        1, batch, n_mem, X_HEADS, -1)
    p_mk, p_mv = xh(mk), xh(mv)

    u_full = jnp.concatenate([cache_conv[l], us.reshape(nb, t_new, -1)], axis=1)
    x1s, qxs = _sample_mix(xs, att_s, u_full, cv_w, cv_b, ln_g, ln_b, w_out_b, g_x, w_xq_b)
    y_prompt, xo_s = _ffn(x2, g_ffn, w_g_b, w_u_b, w_d_b, g_fin, FFN_ROW_TILE,
                          qxs, cache_mem_k[l], cache_mem_v[l], t_new)
    y_prompt = y_prompt.reshape(batch, seq, d)
    y_sample = _sample_tail(x1s, xo_s, w_xo_b, g_ffn, w_g_b, w_u_b, w_d_b, g_fin).reshape(nb, t_new, d)

    return (y_prompt, y_sample, p_wk, p_wv, p_conv, p_mk, p_mv,
            tok_major(s_wk_t), tok_major(s_wv_t), u_full[:, t_new:][None])
```
